```python
import math
import jax
import jax.numpy as jnp
from jax import lax
import numpy as np

D_MODEL = 2048
BATCH = 2
SEQ = 4096
DEPTH = 2
DEC_BATCH = 128
DEC_SEQ = 4
PAST_LEN = 16384
PAGE_SIZE = 128

F32 = jnp.float32
EPS = 1e-6
MIX_WIDTH = D_MODEL
BRANCH = MIX_WIDTH // 4
CONV_K = 4

S5_CH = 16
S5_GROUPS = BRANCH // S5_CH
S5_STATE = 64
SSD_HEADDIM = 64
SSD_HEADS = BRANCH // SSD_HEADDIM
SSD_GROUPS = 2
SSD_STATE = 128
SSD_CHUNK = 128
SSD_CONV_DIM = BRANCH + 2 * SSD_GROUPS * SSD_STATE
MLA_NOPE = 128
MLA_ROPE = 64
MLA_QK = MLA_NOPE + MLA_ROPE
MLA_V = 128
MLA_HEADS = BRANCH // MLA_V
MLA_Q_LORA = 384
MLA_KV_LORA = 256
ROPE_THETA = 10000.0
Q_BLOCK = 128
DN_DK = 128
DN_DV = 128
DN_HEADS = BRANCH // DN_DV
DN_CHUNK = 64
DN_CONV_DIM = 2 * DN_HEADS * DN_DK + DN_HEADS * DN_DV

IN_SIZES = (
    BRANCH, BRANCH,
    BRANCH, SSD_CONV_DIM, SSD_HEADS,
    MLA_Q_LORA, MLA_KV_LORA, MLA_ROPE, BRANCH,
    DN_CONV_DIM, BRANCH, DN_HEADS, DN_HEADS,
)
IN_COLS = sum(IN_SIZES)
IN_SPLITS = tuple(int(s) for s in np.cumsum(IN_SIZES)[:-1])

kernel_name = 'hybrid_s5_ssd_mla_gdn_decode_step'


def rmsnorm(x, g):
    xf = x.astype(F32)
    return xf * lax.rsqrt(jnp.mean(xf * xf, axis=-1, keepdims=True) + EPS) * g.astype(F32)


def l2norm(x):
    return x * lax.rsqrt(jnp.sum(x * x, axis=-1, keepdims=True) + EPS)


def causal_conv(x, buf, w):
    t = x.shape[1]
    xp = jnp.concatenate([buf.astype(x.dtype), x], axis=1)
    y = xp[:, 0:t] * w[0]
    for j in range(1, CONV_K):
        y = y + xp[:, j:j + t] * w[j]
    return y, xp[:, xp.shape[1] - (CONV_K - 1):]


def rope(x, pos):
    half = x.shape[-1] // 2
    inv = ROPE_THETA ** (-jnp.arange(half, dtype=F32) / half)
    ang = pos[:, None] * inv[None, :]
    ang = ang.reshape(ang.shape[:1] + (1,) * (x.ndim - 3) + (half,))
    cos, sin = jnp.cos(ang), jnp.sin(ang)
    x1, x2 = x[..., :half], x[..., half:]
    return jnp.concatenate([x1 * cos - x2 * sin, x2 * cos + x1 * sin], axis=-1)


def s5_branch(u, h0, p):
    bsz, t, _ = u.shape
    uf = u.astype(F32).reshape(bsz, t, S5_GROUPS, S5_CH)
    lam = lax.complex(p['s5_lam_re'].astype(F32), p['s5_lam_im'].astype(F32))
    delta = jnp.exp(p['s5_log_dt'].astype(F32))[:, None]
    a_bar = jnp.exp(lam * delta)
    b_mat = lax.complex(p['s5_b_re'].astype(F32), p['s5_b_im'].astype(F32))
    b_bar = ((a_bar - 1.0) / lam)[..., None] * b_mat
    bu = jnp.einsum('btgc,gpc->btgp', uf.astype(jnp.complex64), b_bar)
    h_init = lax.complex(h0[..., 0].astype(F32), h0[..., 1].astype(F32))
    bu = bu.at[:, 0].add(a_bar * h_init)
    a_seq = jnp.broadcast_to(a_bar, bu.shape)

    def combine(e1, e2):
        a1, b1 = e1
        a2, b2 = e2
        return a1 * a2, a2 * b1 + b2

    _, h = lax.associative_scan(combine, (a_seq, bu), axis=1)
    c_mat = lax.complex(p['s5_c_re'].astype(F32), p['s5_c_im'].astype(F32))
    y = jnp.einsum('gcp,btgp->btgc', c_mat, h).real
    y = y + p['s5_d'].astype(F32).reshape(S5_GROUPS, S5_CH) * uf
    y = jax.nn.gelu(y.reshape(bsz, t, BRANCH))
    y = y * jax.nn.sigmoid(y @ p['s5_glu_w'].astype(F32) + p['s5_glu_b'].astype(F32))
    h_last = h[:, -1]
    return y, jnp.stack([h_last.real, h_last.imag], axis=-1)


def segsum(x):
    n = x.shape[-1]
    xx = jnp.broadcast_to(x[..., None], x.shape + (n,))
    xx = jnp.where(jnp.tril(jnp.ones((n, n), bool), -1), xx, 0.0)
    ss = jnp.cumsum(xx, axis=-2)
    return jnp.where(jnp.tril(jnp.ones((n, n), bool), 0), ss, -jnp.inf)


def ssd_chunked(x, dt, a, bm, cm, h0):
    bsz, t, nh, hp = x.shape
    ns = bm.shape[-1]
    ln = SSD_CHUNK if t % SSD_CHUNK == 0 else t
    nc = t // ln
    xdt = (x * dt[..., None]).reshape(bsz, nc, ln, nh, hp)
    da = (dt * a).reshape(bsz, nc, ln, nh).transpose(0, 3, 1, 2)
    bm = bm.reshape(bsz, nc, ln, nh, ns)
    cm = cm.reshape(bsz, nc, ln, nh, ns)
    da_cum = jnp.cumsum(da, axis=-1)
    lmat = jnp.exp(segsum(da))
    y_diag = jnp.einsum('bclhn,bcshn,bhcls,bcshp->bclhp', cm, bm, lmat, xdt)
    decay_states = jnp.exp(da_cum[..., -1:] - da_cum)
    states = jnp.einsum('bclhn,bhcl,bclhp->bchpn', bm, decay_states, xdt)
    states = jnp.concatenate([h0[:, None], states], axis=1)
    decay_chunk = jnp.exp(segsum(jnp.pad(da_cum[..., -1], ((0, 0), (0, 0), (1, 0)))))
    new_states = jnp.einsum('bhzc,bchpn->bzhpn', decay_chunk, states)
    states, h_last = new_states[:, :-1], new_states[:, -1]
    y_off = jnp.einsum('bclhn,bchpn,bhcl->bclhp', cm, states, jnp.exp(da_cum))
    return (y_diag + y_off).reshape(bsz, t, nh, hp), h_last


def ssd_branch(z, xbc, dt_raw, buf, h0, p):
    bsz, t, _ = xbc.shape
    conv, new_buf = causal_conv(xbc, buf, p['ssd_conv_w'])
    conv = jax.nn.silu((conv + p['ssd_conv_b']).astype(F32))
    gn = SSD_GROUPS * SSD_STATE
    rep = SSD_HEADS // SSD_GROUPS
    xs = conv[..., :BRANCH].reshape(bsz, t, SSD_HEADS, SSD_HEADDIM)
    bm = jnp.repeat(conv[..., BRANCH:BRANCH + gn].reshape(bsz, t, SSD_GROUPS, SSD_STATE), rep, axis=2)
    cm = jnp.repeat(conv[..., BRANCH + gn:].reshape(bsz, t, SSD_GROUPS, SSD_STATE), rep, axis=2)
    dt = jax.nn.softplus(dt_raw.astype(F32) + p['ssd_dt_bias'].astype(F32))
    a = -jnp.exp(p['ssd_a_log'].astype(F32))
    y, h_last = ssd_chunked(xs, dt, a, bm, cm, h0.astype(F32))
    y = y + p['ssd_d'].astype(F32)[:, None] * xs
    y = y.reshape(bsz, t, BRANCH) * jax.nn.silu(z.astype(F32))
    y = rmsnorm(y.reshape(bsz, t, SSD_GROUPS, BRANCH // SSD_GROUPS),
                p['ssd_norm_g'].reshape(SSD_GROUPS, BRANCH // SSD_GROUPS)).reshape(bsz, t, BRANCH)
    return y, h_last, new_buf


def mla_keys_values(lat, kpe, w_ukv, k_g):
    kv = (lat @ w_ukv.astype(F32)).reshape(lat.shape[:-1] + (MLA_HEADS, MLA_NOPE + MLA_V))
    k_nope, v = kv[..., :MLA_NOPE], kv[..., MLA_NOPE:]
    k_pe = jnp.broadcast_to(kpe[..., None, :], k_nope.shape[:-1] + (MLA_ROPE,))
    k = rmsnorm(jnp.concatenate([k_nope, k_pe], axis=-1), k_g)
    return k, v


def attend_prompt(q, k, v):
    bsz, t, nh, dq = q.shape
    qb = Q_BLOCK if t % Q_BLOCK == 0 else t
    nb = t // qb
    q_blocks = q.reshape(bsz, nb, qb, nh, dq).transpose(1, 0, 2, 3, 4)
    k_pos = jnp.arange(t)
    scale = MLA_QK ** -0.5

    def block(args):
        q_blk, start = args
        s = jnp.einsum('bqhd,bkhd->bhqk', q_blk, k) * scale
        q_pos = start + jnp.arange(qb)
        s = jnp.where(k_pos[None, :] <= q_pos[:, None], s, -jnp.inf)
        return jnp.einsum('bhqk,bkhd->bqhd', jax.nn.softmax(s, axis=-1), v)

    out = lax.map(block, (q_blocks, jnp.arange(nb) * qb))
    return out.transpose(1, 0, 2, 3, 4).reshape(bsz, t, nh, MLA_V)


def attend_sample(q, k_new, v_new, pool_lat, pool_pe, page_table, w_ukv, k_g):
    bd, s_len, nh, _ = q.shape
    scale = MLA_QK ** -0.5

    def update(carry, s, v):
        m, l, acc = carry
        m_new = jnp.maximum(m, jnp.max(s, axis=-1))
        corr = jnp.exp(m - m_new)
        pr = jnp.exp(s - m_new[..., None])
        return (m_new, l * corr + jnp.sum(pr, axis=-1),
                acc * corr[..., None] + jnp.einsum('bhqk,bkhd->bhqd', pr, v))

    def page_step(carry, phys):
        k, v = mla_keys_values(pool_lat[phys].astype(F32), pool_pe[phys].astype(F32), w_ukv, k_g)
        s = jnp.einsum('bqhd,bkhd->bhqk', q, k) * scale
        return update(carry, s, v), None

    init = (jnp.full((bd, nh, s_len), -1e30, F32), jnp.zeros((bd, nh, s_len), F32),
            jnp.zeros((bd, nh, s_len, MLA_V), F32))
    carry, _ = lax.scan(page_step, init, page_table.T)
    s = jnp.einsum('bqhd,bkhd->bhqk', q, k_new) * scale
    s = jnp.where(jnp.tril(jnp.ones((s_len, s_len), bool)), s, -1e30)
    _, l, acc = update(carry, s, v_new)
    return (acc / l[..., None]).transpose(0, 2, 1, 3)


def mla_branch(c_q, c_kv, k_pe, z, pos, p, past):
    bsz, t, _ = c_q.shape
    q = (rmsnorm(c_q, p['mla_q_norm_g']) @ p['mla_w_uq'].astype(F32)).reshape(bsz, t, MLA_HEADS, MLA_QK)
    q = jnp.concatenate([q[..., :MLA_NOPE], rope(q[..., MLA_NOPE:], pos)], axis=-1)
    q = rmsnorm(q, p['mla_q_g'])
    lat = rmsnorm(c_kv, p['mla_kv_norm_g'])
    kpe = rope(k_pe.astype(F32), pos)
    k, v = mla_keys_values(lat, kpe, p['mla_w_ukv'], p['mla_k_g'])
    if past is None:
        o = attend_prompt(q, k, v)
    else:
        pool_lat, pool_pe, page_table = past
        o = attend_sample(q, k, v, pool_lat, pool_pe, page_table, p['mla_w_ukv'], p['mla_k_g'])
    y = o.reshape(bsz, t, BRANCH) * jax.nn.silu(z.astype(F32))
    return y, lat, kpe


def gated_delta_chunked(q, k, v, g, beta, s0):
    bsz, t, nh, dk = k.shape
    dv = v.shape[-1]
    ln = DN_CHUNK if t % DN_CHUNK == 0 else t
    n = t // ln

    def chunks(a):
        return a.reshape(bsz, n, ln, nh, a.shape[-1]).transpose(0, 3, 1, 2, 4)

    q = chunks(q) * dk ** -0.5
    k = chunks(k)
    v = chunks(v)
    beta = chunks(beta[..., None])
    g = jnp.cumsum(chunks(g[..., None])[..., 0], axis=-1)
    tri = jnp.tril(jnp.ones((ln, ln), bool))
    gamma = jnp.exp(jnp.where(tri, g[..., :, None] - g[..., None, :], -jnp.inf))
    kb = k * beta
    m = jnp.where(jnp.tril(jnp.ones((ln, ln), bool), -1),
                  jnp.einsum('bhnid,bhnjd->bhnij', kb, k) * gamma, 0.0)
    rhs = jnp.concatenate([v * beta, kb * jnp.exp(g)[..., None]], axis=-1)
    sol = lax.linalg.triangular_solve(m + jnp.eye(ln, dtype=m.dtype), rhs, left_side=True,
                                      lower=True, unit_diagonal=True)
    u, w = sol[..., :dv], sol[..., dv:]
    attn = jnp.where(tri, jnp.einsum('bhnid,bhnjd->bhnij', q, k) * gamma, 0.0)
    g_last = g[..., -1]
    k_dec = k * jnp.exp(g_last[..., None] - g)[..., None]
    qg = q * jnp.exp(g)[..., None]
    xs = tuple(jnp.moveaxis(a, 2, 0) for a in (qg, attn, u, w, k_dec, g_last))

    def step(s, inp):
        qg_i, attn_i, u_i, w_i, kd_i, gl_i = inp
        v_new = u_i - jnp.einsum('bhlk,bhkv->bhlv', w_i, s)
        o = jnp.einsum('bhlk,bhkv->bhlv', qg_i, s) + jnp.einsum('bhls,bhsv->bhlv', attn_i, v_new)
        s = s * jnp.exp(gl_i)[..., None, None] + jnp.einsum('bhlk,bhlv->bhkv', kd_i, v_new)
        return s, o

    s_last, o = lax.scan(step, s0, xs)
    o = jnp.moveaxis(o, 0, 2)
    return o.transpose(0, 2, 3, 1, 4).reshape(bsz, t, nh, dv), s_last


def dn_branch(qkv, z, a_raw, b_raw, buf, s0, p):
    bsz, t, _ = qkv.shape
    conv, new_buf = causal_conv(qkv, buf, p['dn_conv_w'])
    conv = jax.nn.silu(conv.astype(F32))
    kd = DN_HEADS * DN_DK
    q = l2norm(conv[..., :kd].reshape(bsz, t, DN_HEADS, DN_DK))
    k = l2norm(conv[..., kd:2 * kd].reshape(bsz, t, DN_HEADS, DN_DK))
    v = conv[..., 2 * kd:].reshape(bsz, t, DN_HEADS, DN_DV)
    g = -jnp.exp(p['dn_a_log'].astype(F32)) * jax.nn.softplus(a_raw.astype(F32) + p['dn_dt_bias'].astype(F32))
    beta = jax.nn.sigmoid(b_raw.astype(F32))
    o, s_last = gated_delta_chunked(q, k, v, g, beta, s0.astype(F32))
    o = rmsnorm(o, p['dn_norm_g']) * jax.nn.silu(z.astype(F32).reshape(bsz, t, DN_HEADS, DN_DV))
    return o.reshape(bsz, t, BRANCH), s_last, new_buf


def layer(x, c, pos, p, s5_h0, ssd_h0, ssd_buf, dn_s0, dn_buf, past):
    mod = jax.nn.silu(c.astype(F32)) @ p['ada_w'].astype(F32) + p['ada_b'].astype(F32)
    shift, scale, gate = jnp.split(mod[:, None, :], 3, axis=-1)
    h = rmsnorm(x, p['norm_g']) * (1.0 + scale) + shift
    proj = h @ p['w_in'].astype(F32)
    (s5_u, s5_z, ssd_z, ssd_xbc, ssd_dt, mla_cq, mla_ckv, mla_kpe, mla_z,
     dn_qkv, dn_z, dn_a, dn_b) = jnp.split(proj, IN_SPLITS, axis=-1)
    s5_y, s5_h = s5_branch(s5_u, s5_h0, p)
    s5_y = s5_y * jax.nn.silu(s5_z)
    ssd_y, ssd_h, ssd_buf_new = ssd_branch(ssd_z, ssd_xbc, ssd_dt, ssd_buf, ssd_h0, p)
    mla_y, lat, kpe = mla_branch(mla_cq, mla_ckv, mla_kpe, mla_z, pos, p, past)
    dn_y, dn_s, dn_buf_new = dn_branch(dn_qkv, dn_z, dn_a, dn_b, dn_buf, dn_s0, p)
    mixed = jnp.concatenate([s5_y, ssd_y, mla_y, dn_y], axis=-1) @ p['w_out'].astype(F32)
    y = (x.astype(F32) + gate * mixed).astype(x.dtype)
    return y, (lat, kpe, s5_h, ssd_h, ssd_buf_new, dn_s, dn_buf_new)


def setup_inputs(seed: int = 0) -> dict:
    key = jax.random.key(seed)
    keys = jax.random.split(key, 64)
    counter = iter(range(64))

    def nrm(shape, scale=1.0):
        return scale * jax.random.normal(keys[next(counter)], shape, F32)

    def unif(shape, lo, hi):
        return jax.random.uniform(keys[next(counter)], shape, F32, lo, hi)

    def dt_bias(shape):
        dt = jnp.exp(unif(shape, math.log(1e-3), math.log(1e-1)))
        return dt + jnp.log(-jnp.expm1(-dt))

    n_pages = PAST_LEN // PAGE_SIZE
    n_phys = (5 * DEC_BATCH * n_pages) // 4
    nl = DEPTH
    page_table = jax.random.permutation(keys[next(counter)], n_phys)[:DEC_BATCH * n_pages]
    page_table = page_table.reshape(DEC_BATCH, n_pages).astype(jnp.int32)
    s5_im_base = jnp.pi * jnp.arange(S5_STATE, dtype=F32)
    return {
        'x_prompt': nrm((BATCH, SEQ, D_MODEL)),
        'x_sample': nrm((DEC_BATCH, DEC_SEQ, D_MODEL)),
        'c_prompt': nrm((BATCH, D_MODEL)),
        'c_sample': nrm((DEC_BATCH, D_MODEL)),
        'cache_kv_latent': nrm((nl, n_phys, PAGE_SIZE, MLA_KV_LORA)),
        'cache_k_rope': nrm((nl, n_phys, PAGE_SIZE, MLA_ROPE)),
        'state_s5': nrm((nl, DEC_BATCH, S5_GROUPS, S5_STATE, 2), 0.5),
        'state_ssd': nrm((nl, DEC_BATCH, SSD_HEADS, SSD_HEADDIM, SSD_STATE), 0.1),
        'state_ssd_conv': nrm((nl, DEC_BATCH, CONV_K - 1, SSD_CONV_DIM)),
        'state_dn': nrm((nl, DEC_BATCH, DN_HEADS, DN_DK, DN_DV), 0.1),
        'state_dn_conv': nrm((nl, DEC_BATCH, CONV_K - 1, DN_CONV_DIM)),
        'page_table': page_table,
        'norm_g': 1.0 + nrm((nl, D_MODEL), 0.02),
        'ada_w': nrm((nl, D_MODEL, 3 * D_MODEL), 0.5 * D_MODEL ** -0.5),
        'ada_b': nrm((nl, 3 * D_MODEL), 0.01),
        'w_in': nrm((nl, D_MODEL, IN_COLS), D_MODEL ** -0.5),
        'w_out': nrm((nl, MIX_WIDTH, D_MODEL), MIX_WIDTH ** -0.5),
        's5_lam_re': -0.5 + nrm((nl, S5_GROUPS, S5_STATE), 0.01),
        's5_lam_im': s5_im_base + nrm((nl, S5_GROUPS, S5_STATE), 0.01),
        's5_log_dt': unif((nl, S5_GROUPS), math.log(1e-3), math.log(1e-1)),
        's5_b_re': nrm((nl, S5_GROUPS, S5_STATE, S5_CH), (2 * S5_CH) ** -0.5),
        's5_b_im': nrm((nl, S5_GROUPS, S5_STATE, S5_CH), (2 * S5_CH) ** -0.5),
        's5_c_re': nrm((nl, S5_GROUPS, S5_CH, S5_STATE), (2 * S5_STATE) ** -0.5),
        's5_c_im': nrm((nl, S5_GROUPS, S5_CH, S5_STATE), (2 * S5_STATE) ** -0.5),
        's5_d': nrm((nl, BRANCH), 0.5),
        's5_glu_w': nrm((nl, BRANCH, BRANCH), BRANCH ** -0.5),
        's5_glu_b': nrm((nl, BRANCH), 0.01),
        'ssd_conv_w': nrm((nl, CONV_K, SSD_CONV_DIM), 0.5),
        'ssd_conv_b': nrm((nl, SSD_CONV_DIM), 0.01),
        'ssd_dt_bias': dt_bias((nl, SSD_HEADS)),
        'ssd_a_log': jnp.log(unif((nl, SSD_HEADS), 1.0, 16.0)),
        'ssd_d': 1.0 + nrm((nl, SSD_HEADS), 0.1),
        'ssd_norm_g': 1.0 + nrm((nl, BRANCH), 0.02),
        'mla_q_norm_g': 1.0 + nrm((nl, MLA_Q_LORA), 0.02),
        'mla_kv_norm_g': 1.0 + nrm((nl, MLA_KV_LORA), 0.02),
        'mla_w_uq': nrm((nl, MLA_Q_LORA, MLA_HEADS * MLA_QK), MLA_Q_LORA ** -0.5),
        'mla_w_ukv': nrm((nl, MLA_KV_LORA, MLA_HEADS * (MLA_NOPE + MLA_V)), MLA_KV_LORA ** -0.5),
        'mla_q_g': 1.0 + nrm((nl, MLA_QK), 0.02),
        'mla_k_g': 1.0 + nrm((nl, MLA_QK), 0.02),
        'dn_conv_w': nrm((nl, CONV_K, DN_CONV_DIM), 0.5),
        'dn_a_log': jnp.log(unif((nl, DN_HEADS), 1.0, 16.0)),
        'dn_dt_bias': dt_bias((nl, DN_HEADS)),
        'dn_norm_g': 1.0 + nrm((nl, DN_DV), 0.02),
    }


def reference(x_prompt, x_sample, c_prompt, c_sample,
              cache_kv_latent, cache_k_rope, state_s5, state_ssd, state_ssd_conv, state_dn, state_dn_conv,
              page_table,
              norm_g, ada_w, ada_b, w_in, w_out,
              s5_lam_re, s5_lam_im, s5_log_dt, s5_b_re, s5_b_im, s5_c_re, s5_c_im, s5_d, s5_glu_w, s5_glu_b,
              ssd_conv_w, ssd_conv_b, ssd_dt_bias, ssd_a_log, ssd_d, ssd_norm_g,
              mla_q_norm_g, mla_kv_norm_g, mla_w_uq, mla_w_ukv, mla_q_g, mla_k_g,
              dn_conv_w, dn_a_log, dn_dt_bias, dn_norm_g):
    weights = (
        ('norm_g', norm_g), ('ada_w', ada_w), ('ada_b', ada_b), ('w_in', w_in), ('w_out', w_out),
        ('s5_lam_re', s5_lam_re), ('s5_lam_im', s5_lam_im), ('s5_log_dt', s5_log_dt),
        ('s5_b_re', s5_b_re), ('s5_b_im', s5_b_im), ('s5_c_re', s5_c_re), ('s5_c_im', s5_c_im),
        ('s5_d', s5_d), ('s5_glu_w', s5_glu_w), ('s5_glu_b', s5_glu_b),
        ('ssd_conv_w', ssd_conv_w), ('ssd_conv_b', ssd_conv_b), ('ssd_dt_bias', ssd_dt_bias),
        ('ssd_a_log', ssd_a_log), ('ssd_d', ssd_d), ('ssd_norm_g', ssd_norm_g),
        ('mla_q_norm_g', mla_q_norm_g), ('mla_kv_norm_g', mla_kv_norm_g), ('mla_w_uq', mla_w_uq),
        ('mla_w_ukv', mla_w_ukv), ('mla_q_g', mla_q_g), ('mla_k_g', mla_k_g),
        ('dn_conv_w', dn_conv_w), ('dn_a_log', dn_a_log), ('dn_dt_bias', dn_dt_bias), ('dn_norm_g', dn_norm_g),
    )
    bp, tp, _ = x_prompt.shape
    td = x_sample.shape[1]
    pos_p = jnp.arange(tp, dtype=F32)
    pos_s = jnp.arange(td, dtype=F32) + PAST_LEN
    y_p, y_s = x_prompt, x_sample
    p_states, s_states = [], []
    for li in range(DEPTH):
        p = {name: w[li] for name, w in weights}
        y_p, st_p = layer(y_p, c_prompt, pos_p, p,
                          jnp.zeros((bp, S5_GROUPS, S5_STATE, 2), F32),
                          jnp.zeros((bp, SSD_HEADS, SSD_HEADDIM, SSD_STATE), F32),
                          jnp.zeros((bp, CONV_K - 1, SSD_CONV_DIM), F32),
                          jnp.zeros((bp, DN_HEADS, DN_DK, DN_DV), F32),
                          jnp.zeros((bp, CONV_K - 1, DN_CONV_DIM), F32),
                          None)
        p_states.append(st_p)
        y_s, st_s = layer(y_s, c_sample, pos_s, p,
                          state_s5[li], state_ssd[li], state_ssd_conv[li], state_dn[li], state_dn_conv[li],
                          (cache_kv_latent[li], cache_k_rope[li], page_table))
        s_states.append(st_s)

    def stack(states, i):
        return jnp.stack([s[i] for s in states], axis=0)

    p_kv_latent, p_k_rope, p_s5 = stack(p_states, 0), stack(p_states, 1), stack(p_states, 2)
    p_ssd, p_ssd_conv = stack(p_states, 3), stack(p_states, 4)
    p_dn, p_dn_conv = stack(p_states, 5), stack(p_states, 6)
    s_kv_latent, s_k_rope, s_s5 = stack(s_states, 0), stack(s_states, 1), stack(s_states, 2)
    s_ssd, s_ssd_conv = stack(s_states, 3), stack(s_states, 4)
    s_dn, s_dn_conv = stack(s_states, 5), stack(s_states, 6)
    return (y_p, y_s,
            p_kv_latent, p_k_rope, p_s5, p_ssd, p_ssd_conv, p_dn, p_dn_conv,
            s_kv_latent, s_k_rope, s_s5, s_ssd, s_ssd_conv, s_dn, s_dn_conv)
```

```python
import functools
import math

import numpy as np
import jax
import jax.numpy as jnp
from jax import lax
from jax.experimental import pallas as pl
from jax.experimental.pallas import tpu as pltpu

F32 = jnp.float32
BF16 = jnp.bfloat16
EPS = 1e-6

D_MODEL = 2048
BRANCH = 512
CONV_K = 4
S5_CH = 16
S5_GROUPS = 32
S5_STATE = 64
S5_CHUNK = 16
SSD_HEADDIM = 64
SSD_HEADS = 8
SSD_GROUPS = 2
SSD_STATE = 128
SSD_CHUNK = 128
SSD_CONV_DIM = BRANCH + 2 * SSD_GROUPS * SSD_STATE
MLA_NOPE = 128
MLA_ROPE = 64
MLA_QK = MLA_NOPE + MLA_ROPE
MLA_V = 128
MLA_HEADS = 4
MLA_Q_LORA = 384
MLA_KV_LORA = 256
ROPE_THETA = 10000.0
DN_DK = 128
DN_DV = 128
DN_HEADS = 4
DN_CHUNK = 64
DN_CONV_DIM = 2 * DN_HEADS * DN_DK + DN_HEADS * DN_DV
PAGE_SIZE = 128

IN_SIZES = (BRANCH, BRANCH, BRANCH, SSD_CONV_DIM, SSD_HEADS, MLA_Q_LORA, MLA_KV_LORA, MLA_ROPE, BRANCH,
            DN_CONV_DIM, BRANCH, DN_HEADS, DN_HEADS)

ROWS_S = 8
BLK = 512
NBLK = 12
NCOL = NBLK * BLK
B_S5U, B_S5Z, B_SSDZ, B_SSDX, B_SSDBC, B_MLAZ, B_DNQ, B_DNK, B_DNV, B_DNZ, B_MLAQ, B_MLAKV = range(NBLK)
SMALL_OFF = MLA_Q_LORA + MLA_ROPE
NEG = -1e30
VMEM_LIMIT = 52 * 1024 * 1024


def _cparams(sem, vmem=VMEM_LIMIT):
    return pltpu.CompilerParams(dimension_semantics=sem, vmem_limit_bytes=vmem)


def _bdot(a, b):
    return jnp.dot(a.astype(BF16), b.astype(BF16), preferred_element_type=F32)


def _nt(a, b):
    return lax.dot_general(a.astype(BF16), b.astype(BF16), (((1,), (1,)), ((), ())), preferred_element_type=F32)


def _tn(a, b):
    return lax.dot_general(a.astype(BF16), b.astype(BF16), (((0,), (0,)), ((), ())), preferred_element_type=F32)


def _split(x, n):
    parts = []
    r = x
    for _ in range(n):
        p = r.astype(BF16)
        parts.append(p)
        r = r - p.astype(F32)
    return parts


def _dot_exact_l(t01, x):
    t = t01.astype(BF16)
    return sum(jnp.dot(t, p, preferred_element_type=F32) for p in _split(x, 3))


def _dot_exact_r(x, t01):
    t = t01.astype(BF16)
    return sum(jnp.dot(p, t, preferred_element_type=F32) for p in _split(x, 3))


def _dot3(a, b):
    a1, a2 = _split(a, 2)
    b1, b2 = _split(b, 2)
    d = functools.partial(jnp.dot, preferred_element_type=F32)
    return d(a1, b1) + (d(a1, b2) + d(a2, b1))


def _silu(x):
    return x * jax.nn.sigmoid(x)


def _iota(shape, dim):
    return lax.broadcasted_iota(jnp.int32, shape, dim)


def _ada_kernel(c_ref, w_ref, b_ref, o_ref):
    o_ref[0] = _bdot(_silu(c_ref[...]), w_ref[0]) + b_ref[0]


def _ada(c_all, ada_w, ada_b):
    depth, d, n3 = ada_w.shape
    rows = c_all.shape[0]
    tn = 1024
    return pl.pallas_call(
        _ada_kernel,
        grid=(depth, n3 // tn),
        in_specs=[pl.BlockSpec((rows, d), lambda l, j: (0, 0)),
                  pl.BlockSpec((1, d, tn), lambda l, j: (l, 0, j)),
                  pl.BlockSpec((1, 1, tn), lambda l, j: (l, 0, j))],
        out_specs=pl.BlockSpec((1, rows, tn), lambda l, j: (l, 0, j)),
        out_shape=jax.ShapeDtypeStruct((depth, rows, n3), F32),
        compiler_params=_cparams(("arbitrary", "arbitrary")),
        name="ada",
    )(c_all, ada_w, ada_b.reshape(depth, 1, n3))


def _inproj_kernel(x_ref, mod_ref, g_ref, w_ref, o_ref, h_ref):
    @pl.when(pl.program_id(1) == 0)
    def _():
        x = x_ref[...]
        xn = x * lax.rsqrt(jnp.mean(x * x, axis=-1, keepdims=True) + EPS) * g_ref[...]
        mod = mod_ref[...].reshape(-1, 2 * D_MODEL)
        h_ref[...] = (xn * (1.0 + mod[:, D_MODEL:]) + mod[:, :D_MODEL]).astype(BF16)

    o_ref[...] = jnp.dot(h_ref[...], w_ref[...], preferred_element_type=F32)


def _in_proj(x, mod, norm_g, w, *, tm, rows_per_mod):
    t, d = x.shape
    tn = 1536
    if rows_per_mod == 1:
        mod_spec = pl.BlockSpec((tm, 2 * d), lambda i, j: (i, 0))
    else:
        per = rows_per_mod // tm
        mod_spec = pl.BlockSpec((1, 1, 2 * d), lambda i, j: (i // per, 0, 0))
    return pl.pallas_call(
        _inproj_kernel,
        grid=(t // tm, NCOL // tn),
        in_specs=[pl.BlockSpec((tm, d), lambda i, j: (i, 0)),
                  mod_spec,
                  pl.BlockSpec((1, d), lambda i, j: (0, 0)),
                  pl.BlockSpec((d, tn), lambda i, j: (0, j))],
        out_specs=pl.BlockSpec((tm, tn), lambda i, j: (i, j)),
        out_shape=jax.ShapeDtypeStruct((t, NCOL), F32),
        scratch_shapes=[pltpu.VMEM((tm, d), BF16)],
        compiler_params=_cparams(("arbitrary", "arbitrary")),
        name="in_proj",
    )(x, mod, norm_g.reshape(1, d), w)


def _prep_w_in(w):
    o = np.concatenate([[0], np.cumsum(IN_SIZES)])
    s5_u, s5_z, ssd_z, xbc, dt, cq, ckv, kpe, mla_z, qkv, dn_z, dn_a, dn_b = [
        w[:, int(o[i]):int(o[i + 1])] for i in range(len(IN_SIZES))]
    half = MLA_ROPE // 2
    partner = jnp.concatenate([-kpe[:, half:], kpe[:, :half]], axis=1)
    zeros = lambda n: jnp.zeros((w.shape[0], n), w.dtype)
    cols = [s5_u, s5_z, ssd_z, xbc, mla_z, qkv, dn_z,
            cq, kpe, dt, dn_a, dn_b, zeros(BLK - SMALL_OFF - 16),
            ckv, partner, zeros(BLK - MLA_KV_LORA - MLA_ROPE)]
    return jnp.concatenate(cols, axis=1).astype(BF16)


def _cmul(x, tr, ti):
    return x * tr + pltpu.roll(x, S5_STATE, axis=1) * ti


def _s5_kernel(u_ref, toep_ref, w_ref, v_ref, h0_ref, tr_ref, ti_ref, y_ref, hl_ref, *, nbatch, nch, scan):
    u = u_ref[0]
    s_in = jnp.dot(u, w_ref[0], preferred_element_type=F32)
    tr = tr_ref[0]
    ti = ti_ref[0]
    if scan:
        rows = _iota((nch, 2 * S5_STATE), 0)
        h_in, h_last = [], []
        for b in range(nbatch):
            x = s_in[b * nch:(b + 1) * nch]
            for k in range(int(math.log2(nch))):
                d = 1 << k
                xs = jnp.where(rows >= d, pltpu.roll(x, d, axis=0), 0.0)
                x = x + _cmul(xs, tr[k:k + 1], ti[k:k + 1])
            h_in.append(jnp.where(rows >= 1, pltpu.roll(x, 1, axis=0), 0.0))
            h_last.append(x[nch - 1:nch])
        h_in = jnp.concatenate(h_in, axis=0)
        hl_ref[0] = jnp.concatenate(h_last, axis=0)
    else:
        h_in = h0_ref[0]
        hl_ref[0] = _cmul(h_in, tr[0:1], ti[0:1]) + s_in
    y_ref[0] = jnp.dot(u, toep_ref[0], preferred_element_type=F32) + _bdot(h_in, v_ref[0])


def _s5_tables(p, L, nsteps):
    lam = lax.complex(p['s5_lam_re'], p['s5_lam_im'])
    delta = jnp.exp(p['s5_log_dt'])[:, None]
    a_bar = jnp.exp(lam * delta)
    b_bar = ((a_bar - 1.0) / lam)[..., None] * lax.complex(p['s5_b_re'], p['s5_b_im'])
    c_mat = lax.complex(p['s5_c_re'], p['s5_c_im'])
    k_idx = jnp.arange(L + 1, dtype=F32)
    apow = jnp.exp((lam * delta)[:, None, :] * k_idx[None, :, None])
    kern = jnp.einsum('gcp,gtp,gpd->gtcd', c_mat, apow[:, :L], b_bar).real
    s_i = np.arange(L)[:, None]
    t_i = np.arange(L)[None, :]
    tau = np.clip(t_i - s_i, 0, L - 1)
    toep = jnp.where((t_i >= s_i)[None, :, :, None, None], kern[:, tau], 0.0)
    toep = toep.transpose(0, 1, 4, 2, 3).reshape(S5_GROUPS, L * S5_CH, L * S5_CH)
    w_c = apow[:, L - 1 - np.arange(L)][:, :, None, :] * jnp.swapaxes(b_bar, 1, 2)[:, None]
    w_c = w_c.reshape(S5_GROUPS, L * S5_CH, S5_STATE)
    w_mat = jnp.concatenate([w_c.real, w_c.imag], axis=-1)
    v_c = c_mat[:, None] * apow[:, 1:L + 1][:, :, None, :]
    v_c = v_c.reshape(S5_GROUPS, L * S5_CH, S5_STATE)
    v_mat = jnp.swapaxes(jnp.concatenate([v_c.real, -v_c.imag], axis=-1), 1, 2)
    steps = (2.0 ** jnp.arange(nsteps, dtype=F32)) * L
    pw = jnp.exp((lam * delta)[:, None, :] * steps[None, :, None])
    tr = jnp.concatenate([pw.real, pw.real], axis=-1)
    ti = jnp.concatenate([-pw.imag, pw.imag], axis=-1)
    return toep.astype(BF16), w_mat.astype(BF16), v_mat.astype(BF16), tr, ti


def _s5_core(u_g, h0_g, tables, *, nbatch, nch, scan):
    toep, w_mat, v_mat, tr, ti = tables
    g, r, lc = u_g.shape
    nk = tr.shape[1]
    nb_out = nbatch if scan else r
    blk = lambda shape: pl.BlockSpec((1,) + shape, lambda i: (i, 0, 0))
    kern = functools.partial(_s5_kernel, nbatch=nbatch, nch=nch, scan=scan)
    return pl.pallas_call(
        kern,
        grid=(g,),
        in_specs=[blk((r, lc)), blk((lc, lc)), blk((lc, 2 * S5_STATE)), blk((2 * S5_STATE, lc)),
                  blk(h0_g.shape[1:]), blk((nk, 2 * S5_STATE)), blk((nk, 2 * S5_STATE))],
        out_specs=[blk((r, lc)), blk((nb_out, 2 * S5_STATE))],
        out_shape=[jax.ShapeDtypeStruct((g, r, lc), F32), jax.ShapeDtypeStruct((g, nb_out, 2 * S5_STATE), F32)],
        compiler_params=_cparams(("arbitrary",)),
        name="s5_core",
    )(u_g, toep, w_mat, v_mat, h0_g, tr, ti)


def _s5_post_kernel(yc_ref, u_ref, z_ref, d_ref, w_ref, b_ref, o_ref):
    y = jax.nn.gelu(yc_ref[...] + d_ref[...] * u_ref[...])
    y = y * jax.nn.sigmoid(_bdot(y, w_ref[...]) + b_ref[...])
    o_ref[...] = (y * _silu(z_ref[...])).astype(BF16)


def _s5_post(ycore, proj, p, *, tm):
    t = ycore.shape[0]
    row = lambda blk: pl.BlockSpec((tm, BLK), lambda i, blk=blk: (i, blk))
    full = lambda shape: pl.BlockSpec(shape, lambda i: (0, 0))
    return pl.pallas_call(
        _s5_post_kernel,
        grid=(t // tm,),
        in_specs=[row(0), row(B_S5U), row(B_S5Z), full((1, BLK)), full((BLK, BLK)), full((1, BLK))],
        out_specs=row(0),
        out_shape=jax.ShapeDtypeStruct((t, BLK), BF16),
        compiler_params=_cparams(("arbitrary",)),
        name="s5_post",
    )(ycore, proj, proj, p['s5_d'].reshape(1, BLK), p['s5_glu_w'].astype(BF16), p['s5_glu_b'].reshape(1, BLK))


def _causal_conv(tail, cur, w_ref, L):
    ext = jnp.concatenate([tail, cur], axis=0)
    base = 8 - (CONV_K - 1)
    out = ext[base:base + L] * w_ref[0:1, :]
    for j in range(1, CONV_K):
        out = out + ext[base + j:base + j + L] * w_ref[j:j + 1, :]
    return out


def _tri(L, strict=False):
    r = _iota((L, L), 0)
    c = _iota((L, L), 1)
    return (r > c) if strict else (r >= c)


def _pair_select(L, a, b):
    return jnp.where(_iota((L, 128), 1) < 64, a, b)


def _ssd_kernel(z_ref, x_ref, bc_ref, sm_ref, smt_ref, buf_ref, h0_ref,
                cw_ref, cb_ref, dtb_ref, dtbt_ref, al_ref, alt_ref, drow_ref, ng_ref,
                y_ref, hl_ref, tail_ref, st_ref, *, L, nvalid):
    c = pl.program_id(1)

    @pl.when(c == 0)
    def _():
        tail_ref[...] = buf_ref[0]
        st_ref[...] = h0_ref[0]

    xbc = jnp.concatenate([x_ref[...], bc_ref[...]], axis=1)
    conv = _silu(_causal_conv(tail_ref[...], xbc, cw_ref, L) + cb_ref[...])
    tail_ref[...] = xbc[L - 8:L]
    xs = conv[:, :BRANCH]
    gn = SSD_GROUPS * SSD_STATE

    dtc = jax.nn.softplus(sm_ref[:, 0:SSD_HEADS] + dtb_ref[...])
    dtr = jax.nn.softplus(smt_ref[0, 0, 0:SSD_HEADS, :] + dtbt_ref[...])
    if nvalid < L:
        dtc = jnp.where(_iota(dtc.shape, 0) < nvalid, dtc, 0.0)
        dtr = jnp.where(_iota(dtr.shape, 1) < nvalid, dtr, 0.0)
    dac = dtc * (-jnp.exp(al_ref[...]))
    dar = dtr * (-jnp.exp(alt_ref[...]))
    tri = _tri(L)
    cumc = _dot_exact_l(tri.astype(F32), dac)
    cumr = _dot_exact_r(dar, (_iota((L, L), 0) <= _iota((L, L), 1)).astype(F32))
    ecum = jnp.exp(cumc)
    etot = jnp.exp(cumc[L - 1:L])
    edec = jnp.exp(cumc[L - 1:L] - cumc)

    rows128 = _iota((128, SSD_STATE), 0)
    y_parts = []
    for g in range(SSD_GROUPS):
        bm = conv[:, BRANCH + g * SSD_STATE:BRANCH + (g + 1) * SSD_STATE]
        cm = conv[:, BRANCH + gn + g * SSD_STATE:BRANCH + gn + (g + 1) * SSD_STATE]
        cb = _nt(cm, bm)
        for pr in range(2):
            h0 = g * 4 + pr * 2
            h1 = h0 + 1
            xp = xs[:, h0 * SSD_HEADDIM:(h0 + 2) * SSD_HEADDIM]
            xdt = xp * _pair_select(L, dtc[:, h0:h0 + 1], dtc[:, h1:h1 + 1])
            yd = []
            for h in (h0, h1):
                lm = jnp.exp(jnp.where(tri, cumc[:, h:h + 1] - cumr[h:h + 1, :], NEG))
                yd.append(_bdot(cb * lm, xdt))
            y_diag = jnp.where(_iota((L, 128), 1) < 64, yd[0], yd[1])
            st = st_ref[h0:h0 + 2].reshape(2 * SSD_HEADDIM, SSD_STATE)
            y_off = _nt(cm, st) * _pair_select(L, ecum[:, h0:h0 + 1], ecum[:, h1:h1 + 1])
            upd = _tn(xdt * _pair_select(L, edec[:, h0:h0 + 1], edec[:, h1:h1 + 1]), bm)
            st_new = st * jnp.where(rows128 < 64, etot[:, h0:h0 + 1], etot[:, h1:h1 + 1]) + upd
            st_ref[h0:h0 + 2] = st_new.reshape(2, SSD_HEADDIM, SSD_STATE)
            y_parts.append(y_diag + y_off)
    y = jnp.concatenate(y_parts, axis=1) + drow_ref[...] * xs
    y = y * _silu(z_ref[...])
    half = BRANCH // SSD_GROUPS
    outs = []
    for g in range(SSD_GROUPS):
        yg = y[:, g * half:(g + 1) * half]
        outs.append(yg * lax.rsqrt(jnp.mean(yg * yg, axis=-1, keepdims=True) + EPS))
    y_ref[...] = (jnp.concatenate(outs, axis=1) * ng_ref[...]).astype(y_ref.dtype)

    @pl.when(c == pl.num_programs(1) - 1)
    def _():
        hl_ref[0] = st_ref[...]


def _ssd(proj, small, small_t, buf8, h0, p, *, nbatch, nch, L, nvalid):
    t = proj.shape[0]
    row = lambda blk: pl.BlockSpec((L, BLK), lambda b, c, blk=blk: (b * nch + c, blk))
    full = lambda shape: pl.BlockSpec(shape, lambda b, c: (0,) * len(shape))
    hd = (SSD_HEADS, SSD_HEADDIM, SSD_STATE)
    kern = functools.partial(_ssd_kernel, L=L, nvalid=nvalid)
    return pl.pallas_call(
        kern,
        grid=(nbatch, nch),
        in_specs=[row(B_SSDZ), row(B_SSDX), row(B_SSDBC),
                  pl.BlockSpec((L, 16), lambda b, c: (b * nch + c, 0)),
                  pl.BlockSpec((1, 1, 16, L), lambda b, c: (b, c, 0, 0)),
                  pl.BlockSpec((1, 8, SSD_CONV_DIM), lambda b, c: (b, 0, 0)),
                  pl.BlockSpec((1,) + hd, lambda b, c: (b, 0, 0, 0)),
                  full((CONV_K, SSD_CONV_DIM)), full((1, SSD_CONV_DIM)),
                  full((1, SSD_HEADS)), full((SSD_HEADS, 1)), full((1, SSD_HEADS)), full((SSD_HEADS, 1)),
                  full((1, BRANCH)), full((1, BRANCH))],
        out_specs=[pl.BlockSpec((L, BLK), lambda b, c: (b * nch + c, 0)),
                   pl.BlockSpec((1,) + hd, lambda b, c: (b, 0, 0, 0))],
        out_shape=[jax.ShapeDtypeStruct((t, BLK), BF16 if L % 16 == 0 else F32), jax.ShapeDtypeStruct((nbatch,) + hd, F32)],
        scratch_shapes=[pltpu.VMEM((8, SSD_CONV_DIM), F32), pltpu.VMEM(hd, F32)],
        compiler_params=_cparams(("arbitrary", "arbitrary")),
        name="ssd",
    )(proj, proj, proj, small, small_t, buf8, h0,
      p['ssd_conv_w'], p['ssd_conv_b'].reshape(1, -1),
      p['ssd_dt_bias'].reshape(1, -1), p['ssd_dt_bias'].reshape(-1, 1),
      p['ssd_a_log'].reshape(1, -1), p['ssd_a_log'].reshape(-1, 1),
      jnp.repeat(p['ssd_d'], SSD_HEADDIM).reshape(1, BRANCH), p['ssd_norm_g'].reshape(1, BRANCH))


def _dn_kernel(q_ref, k_ref, v_ref, z_ref, sm_ref, smt_ref, buf_ref, s0_ref,
               cw_ref, al_ref, alt_ref, dtb_ref, dtbt_ref, ng_ref,
               y_ref, sl_ref, tail_ref, st_ref, *, L, nvalid):
    c = pl.program_id(1)

    @pl.when(c == 0)
    def _():
        tail_ref[...] = buf_ref[0]
        st_ref[...] = s0_ref[0]

    qkv = jnp.concatenate([q_ref[...], k_ref[...], v_ref[...]], axis=1)
    conv = _silu(_causal_conv(tail_ref[...], qkv, cw_ref, L))
    tail_ref[...] = qkv[L - 8:L]

    a_off = SSD_HEADS
    b_off = SSD_HEADS + DN_HEADS
    gc = -jnp.exp(al_ref[...]) * jax.nn.softplus(sm_ref[:, a_off:a_off + DN_HEADS] + dtb_ref[...])
    gr = -jnp.exp(alt_ref[...]) * jax.nn.softplus(smt_ref[0, 0, a_off:a_off + DN_HEADS, :] + dtbt_ref[...])
    beta = jax.nn.sigmoid(sm_ref[:, b_off:b_off + DN_HEADS])
    if nvalid < L:
        vc = _iota(gc.shape, 0) < nvalid
        gc = jnp.where(vc, gc, 0.0)
        beta = jnp.where(vc, beta, 0.0)
        gr = jnp.where(_iota(gr.shape, 1) < nvalid, gr, 0.0)
    tri = _tri(L)
    stri = _tri(L, strict=True)
    gcc = _dot_exact_l(tri.astype(F32), gc)
    gcr = _dot_exact_r(gr, (_iota((L, L), 0) <= _iota((L, L), 1)).astype(F32))
    eye = (_iota((L, L), 0) == _iota((L, L), 1)).astype(F32)
    kd = DN_HEADS * DN_DK
    outs = []
    for h in range(DN_HEADS):
        qh = conv[:, h * DN_DK:(h + 1) * DN_DK]
        kh = conv[:, kd + h * DN_DK:kd + (h + 1) * DN_DK]
        vh = conv[:, 2 * kd + h * DN_DV:2 * kd + (h + 1) * DN_DV]
        qh = qh * lax.rsqrt(jnp.sum(qh * qh, axis=-1, keepdims=True) + EPS) * (DN_DK ** -0.5)
        kh = kh * lax.rsqrt(jnp.sum(kh * kh, axis=-1, keepdims=True) + EPS)
        gci = gcc[:, h:h + 1]
        bcol = beta[:, h:h + 1]
        gamma = jnp.exp(jnp.where(tri, gci - gcr[h:h + 1, :], NEG))
        n = jnp.where(stri, -(bcol * _nt(kh, kh) * gamma), 0.0)
        inv = eye + n
        pw = n
        for _ in range(int(math.log2(L)) - 1):
            pw = _dot3(pw, pw)
            inv = inv + _dot3(inv, pw)
        rhs = jnp.concatenate([vh * bcol, kh * (bcol * jnp.exp(gci))], axis=1)
        sol = _dot3(inv, rhs)
        u, w = sol[:, :DN_DV], sol[:, DN_DV:]
        attn = jnp.where(tri, _nt(qh, kh) * gamma, 0.0)
        g_last = gci[L - 1:L]
        k_dec = kh * jnp.exp(g_last - gci)
        qg = qh * jnp.exp(gci)
        s = st_ref[h]
        v_new = u - _bdot(w, s)
        o = _bdot(qg, s) + _bdot(attn, v_new)
        st_ref[h] = s * jnp.exp(g_last) + _tn(k_dec, v_new)
        o = o * lax.rsqrt(jnp.mean(o * o, axis=-1, keepdims=True) + EPS) * ng_ref[...]
        outs.append(o * _silu(z_ref[:, h * DN_DV:(h + 1) * DN_DV]))
    y_ref[...] = jnp.concatenate(outs, axis=1).astype(y_ref.dtype)

    @pl.when(c == pl.num_programs(1) - 1)
    def _():
        sl_ref[0] = st_ref[...]


def _dn(proj, small, small_t, buf8, s0, p, *, nbatch, nch, L, nvalid):
    t = proj.shape[0]
    row = lambda blk: pl.BlockSpec((L, BLK), lambda b, c, blk=blk: (b * nch + c, blk))
    full = lambda shape: pl.BlockSpec(shape, lambda b, c: (0,) * len(shape))
    hd = (DN_HEADS, DN_DK, DN_DV)
    kern = functools.partial(_dn_kernel, L=L, nvalid=nvalid)
    return pl.pallas_call(
        kern,
        grid=(nbatch, nch),
        in_specs=[row(B_DNQ), row(B_DNK), row(B_DNV), row(B_DNZ),
                  pl.BlockSpec((L, 16), lambda b, c: (b * nch + c, 0)),
                  pl.BlockSpec((1, 1, 16, L), lambda b, c: (b, c, 0, 0)),
                  pl.BlockSpec((1, 8, DN_CONV_DIM), lambda b, c: (b, 0, 0)),
                  pl.BlockSpec((1,) + hd, lambda b, c: (b, 0, 0, 0)),
                  full((CONV_K, DN_CONV_DIM)),
                  full((1, DN_HEADS)), full((DN_HEADS, 1)), full((1, DN_HEADS)), full((DN_HEADS, 1)),
                  full((1, DN_DV))],
        out_specs=[pl.BlockSpec((L, BLK), lambda b, c: (b * nch + c, 0)),
                   pl.BlockSpec((1,) + hd, lambda b, c: (b, 0, 0, 0))],
        out_shape=[jax.ShapeDtypeStruct((t, BLK), BF16 if L % 16 == 0 else F32), jax.ShapeDtypeStruct((nbatch,) + hd, F32)],
        scratch_shapes=[pltpu.VMEM((8, DN_CONV_DIM), F32), pltpu.VMEM(hd, F32)],
        compiler_params=_cparams(("arbitrary", "arbitrary")),
        name="dn",
    )(proj, proj, proj, proj, small, small_t, buf8, s0,
      p['dn_conv_w'],
      p['dn_a_log'].reshape(1, -1), p['dn_a_log'].reshape(-1, 1),
      p['dn_dt_bias'].reshape(1, -1), p['dn_dt_bias'].reshape(-1, 1),
      p['dn_norm_g'].reshape(1, -1))


def _mla_pre_kernel(a_ref, b_ref, cos_ref, sin_ref, qng_ref, kvng_ref, wuq_ref, gq_ref, *rest, expand_kv):
    if expand_kv:
        wukv_ref, gk_ref, q_ref, lat_ref, kpe_ref, k_ref, v_ref = rest
    else:
        q_ref, lat_ref, kpe_ref = rest
    cos = cos_ref[...]
    sin = sin_ref[...]
    cq = a_ref[:, :MLA_Q_LORA]
    cqn = cq * lax.rsqrt(jnp.mean(cq * cq, axis=-1, keepdims=True) + EPS) * qng_ref[...]
    qp = _bdot(cqn, wuq_ref[...])
    for h in range(MLA_HEADS):
        nope = qp[:, h * 384:h * 384 + 128]
        ro = qp[:, h * 384 + 128:h * 384 + 256] * cos + qp[:, h * 384 + 256:h * 384 + 384] * sin
        ms = (jnp.sum(nope * nope, axis=-1, keepdims=True) + jnp.sum(ro * ro, axis=-1, keepdims=True)) / MLA_QK
        rinv = lax.rsqrt(ms + EPS)
        q_ref[:, h * 256:h * 256 + 128] = (nope * rinv * gq_ref[:, :128]).astype(q_ref.dtype)
        q_ref[:, h * 256 + 128:(h + 1) * 256] = (ro * rinv * gq_ref[:, 128:]).astype(q_ref.dtype)
    ckv = b_ref[:, :MLA_KV_LORA]
    lat = ckv * lax.rsqrt(jnp.mean(ckv * ckv, axis=-1, keepdims=True) + EPS) * kvng_ref[...]
    lat_ref[...] = lat
    kpe = a_ref[:, MLA_Q_LORA:] * cos + b_ref[:, MLA_KV_LORA:MLA_KV_LORA + 128] * sin
    kpe_ref[...] = kpe[:, :MLA_ROPE]
    if expand_kv:
        kv = _bdot(lat, wukv_ref[...])
        pe_sq = jnp.sum(kpe * kpe, axis=-1, keepdims=True)
        for h in range(MLA_HEADS):
            kn = kv[:, h * 256:h * 256 + 128]
            rinv = lax.rsqrt((jnp.sum(kn * kn, axis=-1, keepdims=True) + pe_sq) / MLA_QK + EPS)
            k_ref[:, h * 256:h * 256 + 128] = (kn * rinv * gk_ref[:, :128]).astype(BF16)
            k_ref[:, h * 256 + 128:(h + 1) * 256] = (kpe * rinv * gk_ref[:, 128:]).astype(BF16)
            v_ref[:, h * 128:(h + 1) * 128] = kv[:, h * 256 + 128:(h + 1) * 256].astype(BF16)


def _pad_gain(g, scale=1.0):
    return jnp.concatenate([g * scale, jnp.zeros((256 - MLA_QK,), F32)]).reshape(1, 256)


def _mla_pre(proj, cos, sin, p, *, tm, pos_blocks, expand_kv):
    t = proj.shape[0]
    half = MLA_ROPE // 2
    wq = p['mla_w_uq'].reshape(MLA_Q_LORA, MLA_HEADS, MLA_QK)
    nope, ro = wq[..., :MLA_NOPE], wq[..., MLA_NOPE:]
    zeros = jnp.zeros((MLA_Q_LORA, MLA_HEADS, 128 - MLA_ROPE), F32)
    partner = jnp.concatenate([-ro[..., half:], ro[..., :half]], axis=-1)
    wuq = jnp.concatenate([nope, ro, zeros, partner, zeros], axis=-1).reshape(MLA_Q_LORA, MLA_HEADS * 384).astype(BF16)
    row = lambda blk: pl.BlockSpec((tm, BLK), lambda i, blk=blk: (i, blk))
    tab = pl.BlockSpec((tm, 128), lambda i: (i % pos_blocks, 0))
    full = lambda shape: pl.BlockSpec(shape, lambda i: (0, 0))
    in_specs = [row(B_MLAQ), row(B_MLAKV), tab, tab, full((1, MLA_Q_LORA)), full((1, MLA_KV_LORA)),
                full((MLA_Q_LORA, MLA_HEADS * 384)), full((1, 256))]
    args = [proj, proj, cos, sin, p['mla_q_norm_g'].reshape(1, -1), p['mla_kv_norm_g'].reshape(1, -1),
            wuq, _pad_gain(p['mla_q_g'], MLA_QK ** -0.5)]
    out_specs = [pl.BlockSpec((tm, 1024), lambda i: (i, 0)), pl.BlockSpec((tm, MLA_KV_LORA), lambda i: (i, 0)),
                 pl.BlockSpec((tm, MLA_ROPE), lambda i: (i, 0))]
    out_shape = [jax.ShapeDtypeStruct((t, 1024), BF16 if expand_kv else F32), jax.ShapeDtypeStruct((t, MLA_KV_LORA), F32),
                 jax.ShapeDtypeStruct((t, MLA_ROPE), F32)]
    if expand_kv:
        in_specs += [full((MLA_KV_LORA, MLA_HEADS * 256)), full((1, 256))]
        args += [p['mla_w_ukv'].astype(BF16), _pad_gain(p['mla_k_g'])]
        out_specs += [pl.BlockSpec((tm, 1024), lambda i: (i, 0)), pl.BlockSpec((tm, BRANCH), lambda i: (i, 0))]
        out_shape += [jax.ShapeDtypeStruct((t, 1024), BF16), jax.ShapeDtypeStruct((t, BRANCH), BF16)]
    return pl.pallas_call(
        functools.partial(_mla_pre_kernel, expand_kv=expand_kv),
        grid=(t // tm,),
        in_specs=in_specs, out_specs=out_specs, out_shape=out_shape,
        compiler_params=_cparams(("arbitrary",)),
        name="mla_pre",
    )(*args)


def _rope_tables(pos):
    half = MLA_ROPE // 2
    inv = ROPE_THETA ** (-jnp.arange(half, dtype=F32) / half)
    ang = pos[:, None] * inv[None, :]
    z = jnp.zeros((pos.shape[0], 128 - MLA_ROPE), F32)
    cos, sin = jnp.cos(ang), jnp.sin(ang)
    return jnp.concatenate([cos, cos, z], axis=1), jnp.concatenate([sin, sin, z], axis=1)


def _flash_kernel(q_ref, k_ref, v_ref, z_ref, o_ref, m_ref, l_ref, acc_ref, *, tq, tk):
    i = pl.program_id(2)
    m_ref[...] = jnp.full(m_ref.shape, NEG, F32)
    l_ref[...] = jnp.zeros(l_ref.shape, F32)
    acc_ref[...] = jnp.zeros(acc_ref.shape, F32)
    q = q_ref[...]
    row = _iota((tq, tk), 0)
    col = _iota((tq, tk), 1)

    def step(j, masked):
        k = k_ref[pl.ds(pl.multiple_of(j * tk, tk), tk), :]
        v = v_ref[pl.ds(pl.multiple_of(j * tk, tk), tk), :]
        s = lax.dot_general(q, k, (((1,), (1,)), ((), ())), preferred_element_type=F32)
        if masked:
            s = jnp.where(col <= row, s, NEG)
        m_prev = m_ref[...]
        m_new = jnp.maximum(m_prev, jnp.max(s, axis=-1, keepdims=True))
        pr = jnp.exp(s - m_new)
        corr = jnp.exp(m_prev - m_new)
        l_ref[...] = l_ref[...] * corr + jnp.sum(pr, axis=-1, keepdims=True)
        acc_ref[...] = acc_ref[...] * corr + jnp.dot(pr.astype(BF16), v, preferred_element_type=F32)
        m_ref[...] = m_new

    def body(j, carry):
        step(j, False)
        return carry

    lax.fori_loop(0, i, body, 0)
    step(i, True)
    o_ref[...] = (acc_ref[...] / l_ref[...] * _silu(z_ref[...])).astype(BF16)


def _flash(q, k, v, proj, *, nbatch, seq, tq):
    t = q.shape[0]
    nq = seq // tq
    kern = functools.partial(_flash_kernel, tq=tq, tk=tq)
    return pl.pallas_call(
        kern,
        grid=(nbatch, MLA_HEADS, nq),
        in_specs=[pl.BlockSpec((tq, 256), lambda b, h, i: (b * nq + i, h)),
                  pl.BlockSpec((seq, 256), lambda b, h, i: (b, h)),
                  pl.BlockSpec((seq, MLA_V), lambda b, h, i: (b, h)),
                  pl.BlockSpec((tq, MLA_V), lambda b, h, i: (b * nq + i, B_MLAZ * (BLK // MLA_V) + h))],
        out_specs=pl.BlockSpec((tq, MLA_V), lambda b, h, i: (b * nq + i, h)),
        out_shape=jax.ShapeDtypeStruct((t, BRANCH), BF16),
        scratch_shapes=[pltpu.VMEM((tq, 1), F32), pltpu.VMEM((tq, 1), F32), pltpu.VMEM((tq, MLA_V), F32)],
        compiler_params=_cparams(("arbitrary", "arbitrary", "arbitrary")),
        name="flash",
    )(q, k, v, proj)


def _decode_kernel(pt_ref, q_ref, latn_ref, pen_ref, wukt_ref, gk_ref, lat_hbm, pe_hbm, o_ref,
                   latbuf, pebuf, sem, *, li, pps, ts, n_pages):
    b = pl.program_id(0)
    ng = n_pages // pps
    nrow = MLA_HEADS * ROWS_S

    def copies(g, slot, k):
        ph = pt_ref[b, g * pps + k]
        return (pltpu.make_async_copy(lat_hbm.at[li, ph], latbuf.at[slot, k], sem.at[slot]),
                pltpu.make_async_copy(pe_hbm.at[li, ph], pebuf.at[slot, k], sem.at[slot]))

    def fetch(g, slot):
        for k in range(pps):
            for cp in copies(g, slot, k):
                cp.start()

    def wait(g, slot):
        for k in range(pps):
            for cp in copies(g, slot, k):
                cp.wait()

    fetch(0, 0)

    q = q_ref[...]
    gk = gk_ref[...]
    qa, qpe = [], []
    for h in range(MLA_HEADS):
        qn = q[:, h * 256:h * 256 + 128] * gk[:, :128]
        qa.append(_bdot(qn, wukt_ref[h * 128:(h + 1) * 128, :]))
        qpe.append(q[:, h * 256 + 128:h * 256 + 128 + MLA_ROPE] * gk[:, 128:128 + MLA_ROPE])
    amat = jnp.concatenate([wukt_ref[...], jnp.concatenate(qa, axis=0).astype(BF16)], axis=0)
    qpe = jnp.concatenate(qpe, axis=0).astype(BF16)
    ones = jnp.ones((8, MLA_ROPE), BF16)
    nk = MLA_HEADS * MLA_NOPE

    def attend(lat, pe, carry, mask):
        m, l, acc = carry
        lat_b = lat.astype(BF16)
        pe_b = pe.astype(BF16)
        r = lax.dot_general(amat, lat_b, (((1,), (1,)), ((), ())), preferred_element_type=F32)
        s_pe = lax.dot_general(qpe, pe_b, (((1,), (1,)), ((), ())), preferred_element_type=F32)
        pe_sq = lax.dot_general(ones, (pe * pe).astype(BF16), (((1,), (1,)), ((), ())),
                                preferred_element_type=F32)[0:1]
        s = []
        for h in range(MLA_HEADS):
            kn = r[h * MLA_NOPE:(h + 1) * MLA_NOPE]
            rinv = lax.rsqrt((jnp.sum(kn * kn, axis=0, keepdims=True) + pe_sq) / MLA_QK + EPS)
            s.append((r[nk + h * ROWS_S:nk + (h + 1) * ROWS_S] + s_pe[h * ROWS_S:(h + 1) * ROWS_S]) * rinv)
        s = jnp.concatenate(s, axis=0)
        if mask is not None:
            s = jnp.where(mask, s, NEG)
        m_new = jnp.maximum(m, jnp.max(s, axis=-1, keepdims=True))
        pr = jnp.exp(s - m_new)
        corr = jnp.exp(m - m_new)
        l = l * corr + jnp.sum(pr, axis=-1, keepdims=True)
        acc = acc * corr + jnp.dot(pr.astype(BF16), lat_b, preferred_element_type=F32)
        return m_new, l, acc

    carry = (jnp.full((nrow, 1), NEG, F32), jnp.zeros((nrow, 1), F32), jnp.zeros((nrow, MLA_KV_LORA), F32))
    qrow = _iota((nrow, ROWS_S), 0) % ROWS_S
    carry = attend(latn_ref[...], pen_ref[...], carry, _iota((nrow, ROWS_S), 1) <= qrow)

    def group(g, slot, carry):
        @pl.when(g + 1 < ng)
        def _():
            fetch(g + 1, 1 - slot)

        wait(g, slot)
        for k in range(pps * PAGE_SIZE // ts):
            per = ts // PAGE_SIZE
            lat = latbuf[slot, k * per:(k + 1) * per].reshape(ts, MLA_KV_LORA)
            pe = pebuf[slot, k * per:(k + 1) * per].reshape(ts, MLA_ROPE)
            carry = attend(lat, pe, carry, None)
        return carry

    def two_groups(gg, carry):
        carry = group(2 * gg, 0, carry)
        return group(2 * gg + 1, 1, carry)

    m, l, acc = lax.fori_loop(0, ng // 2, two_groups, carry)
    o_ref[0] = acc / l


def _decode(page_table, q, lat_new, pe_new, cache_lat, cache_pe, p, *, li, pps, ts):
    nseq, n_pages = page_table.shape
    nrow = MLA_HEADS * ROWS_S
    wukv = p['mla_w_ukv'].reshape(MLA_KV_LORA, MLA_HEADS, MLA_NOPE + MLA_V)
    wukt = wukv[..., :MLA_NOPE].transpose(1, 2, 0).reshape(MLA_HEADS * MLA_NOPE, MLA_KV_LORA).astype(BF16)
    kern = functools.partial(_decode_kernel, li=li, pps=pps, ts=ts, n_pages=n_pages)
    grid_spec = pltpu.PrefetchScalarGridSpec(
        num_scalar_prefetch=1,
        grid=(nseq,),
        in_specs=[pl.BlockSpec((ROWS_S, 1024), lambda b, pt: (b, 0)),
                  pl.BlockSpec((ROWS_S, MLA_KV_LORA), lambda b, pt: (b, 0)),
                  pl.BlockSpec((ROWS_S, MLA_ROPE), lambda b, pt: (b, 0)),
                  pl.BlockSpec((MLA_HEADS * MLA_NOPE, MLA_KV_LORA), lambda b, pt: (0, 0)),
                  pl.BlockSpec((1, 256), lambda b, pt: (0, 0)),
                  pl.BlockSpec(memory_space=pl.ANY),
                  pl.BlockSpec(memory_space=pl.ANY)],
        out_specs=pl.BlockSpec((1, nrow, MLA_KV_LORA), lambda b, pt: (b, 0, 0)),
        scratch_shapes=[pltpu.VMEM((2, pps, PAGE_SIZE, MLA_KV_LORA), F32),
                        pltpu.VMEM((2, pps, PAGE_SIZE, MLA_ROPE), F32),
                        pltpu.SemaphoreType.DMA((2,))],
    )
    return pl.pallas_call(
        kern,
        grid_spec=grid_spec,
        out_shape=jax.ShapeDtypeStruct((nseq, nrow, MLA_KV_LORA), F32),
        compiler_params=_cparams(("arbitrary",)),
        name="decode",
    )(page_table, q, lat_new, pe_new, wukt, _pad_gain(p['mla_k_g']), cache_lat, cache_pe)


def _mla_post_kernel(o_ref, z_ref, wuv_ref, y_ref, *, sb):
    for h in range(MLA_HEADS):
        x = o_ref[:, h * ROWS_S:(h + 1) * ROWS_S, :].reshape(sb * ROWS_S, MLA_KV_LORA)
        y = _bdot(x, wuv_ref[h]) * _silu(z_ref[:, h * MLA_V:(h + 1) * MLA_V])
        y_ref[:, h * MLA_V:(h + 1) * MLA_V] = y.astype(BF16)


def _mla_post(o_lat, proj, p, *, sb):
    nseq = o_lat.shape[0]
    nrow = MLA_HEADS * ROWS_S
    wukv = p['mla_w_ukv'].reshape(MLA_KV_LORA, MLA_HEADS, MLA_NOPE + MLA_V)
    wuv = wukv[..., MLA_NOPE:].transpose(1, 0, 2).astype(BF16)
    return pl.pallas_call(
        functools.partial(_mla_post_kernel, sb=sb),
        grid=(nseq // sb,),
        in_specs=[pl.BlockSpec((sb, nrow, MLA_KV_LORA), lambda i: (i, 0, 0)),
                  pl.BlockSpec((sb * ROWS_S, BLK), lambda i: (i, B_MLAZ)),
                  pl.BlockSpec((MLA_HEADS, MLA_KV_LORA, MLA_V), lambda i: (0, 0, 0))],
        out_specs=pl.BlockSpec((sb * ROWS_S, BLK), lambda i: (i, 0)),
        out_shape=jax.ShapeDtypeStruct((nseq * ROWS_S, BLK), BF16),
        compiler_params=_cparams(("arbitrary",)),
        name="mla_post",
    )(o_lat, proj, wuv)


def _outproj_kernel(b0_ref, b1_ref, b2_ref, b3_ref, x_ref, gate_ref, w_ref, y_ref):
    mixed = _bdot(b0_ref[...], w_ref[0:BLK, :])
    for k, br in enumerate((b1_ref, b2_ref, b3_ref), start=1):
        mixed = mixed + _bdot(br[...], w_ref[k * BLK:(k + 1) * BLK, :])
    gate = gate_ref[...].reshape(-1, D_MODEL)
    y_ref[...] = x_ref[...] + gate * mixed


def _out_proj(branches, x, mod, w, *, tm, rows_per_mod):
    t, d = x.shape
    if rows_per_mod == 1:
        gate_spec = pl.BlockSpec((tm, d), lambda i: (i, 2))
    else:
        per = rows_per_mod // tm
        gate_spec = pl.BlockSpec((1, 1, d), lambda i: (i // per, 0, 2))
    br = pl.BlockSpec((tm, BLK), lambda i: (i, 0))
    return pl.pallas_call(
        _outproj_kernel,
        grid=(t // tm,),
        in_specs=[br, br, br, br, pl.BlockSpec((tm, d), lambda i: (i, 0)), gate_spec,
                  pl.BlockSpec((d, d), lambda i: (0, 0))],
        out_specs=pl.BlockSpec((tm, d), lambda i: (i, 0)),
        out_shape=jax.ShapeDtypeStruct((t, d), F32),
        compiler_params=_cparams(("arbitrary",)),
        name="out_proj",
    )(*branches, x, mod, w)


def _small_arrays(proj, nbatch, nch, L):
    small = proj[:, B_MLAQ * BLK + SMALL_OFF:B_MLAQ * BLK + SMALL_OFF + 16]
    small_t = small.reshape(nbatch, nch, L, 16).transpose(0, 1, 3, 2)
    return small, small_t


def _pad_rows(a):
    return jnp.pad(a, ((0, 0), (8 - (CONV_K - 1), 0), (0, 0)))


def _layer_prompt(x, mod3, p, w_in, w_out, tabs, nbatch, seq):
    t = x.shape[0]
    tm = min(512, seq)
    proj = _in_proj(x, mod3, p['norm_g'], w_in, tm=tm, rows_per_mod=seq)
    nch5 = seq // S5_CHUNK
    u_g = proj[:, :BLK].reshape(nbatch * nch5, S5_CHUNK, S5_GROUPS, S5_CH).transpose(2, 0, 1, 3)
    u_g = u_g.reshape(S5_GROUPS, nbatch * nch5, S5_CHUNK * S5_CH).astype(BF16)
    h0 = jnp.zeros((S5_GROUPS, 8, 2 * S5_STATE), F32)
    y5, hl5 = _s5_core(u_g, h0, tabs['s5_prompt'], nbatch=nbatch, nch=nch5, scan=True)
    y5 = y5.reshape(S5_GROUPS, nbatch * nch5, S5_CHUNK, S5_CH).transpose(1, 2, 0, 3).reshape(t, BLK)
    s5_y = _s5_post(y5, proj, p, tm=tm)
    s5_h = jnp.stack([hl5[..., :S5_STATE], hl5[..., S5_STATE:]], axis=-1).transpose(1, 0, 2, 3)
    nch = seq // SSD_CHUNK
    small, small_t = _small_arrays(proj, nbatch, nch, SSD_CHUNK)
    ssd_y, ssd_h = _ssd(proj, small, small_t, jnp.zeros((nbatch, 8, SSD_CONV_DIM), F32),
                        jnp.zeros((nbatch, SSD_HEADS, SSD_HEADDIM, SSD_STATE), F32), p,
                        nbatch=nbatch, nch=nch, L=SSD_CHUNK, nvalid=SSD_CHUNK)
    p3 = proj.reshape(nbatch, seq, NCOL)
    ssd_buf = p3[:, seq - (CONV_K - 1):, B_SSDX * BLK:B_SSDX * BLK + SSD_CONV_DIM]
    q, lat, kpe, k, v = _mla_pre(proj, tabs['cos_p'], tabs['sin_p'], p, tm=tm, pos_blocks=seq // tm, expand_kv=True)
    mla_y = _flash(q, k, v, proj, nbatch=nbatch, seq=seq, tq=tm)
    nchd = seq // DN_CHUNK
    small, small_t = _small_arrays(proj, nbatch, nchd, DN_CHUNK)
    dn_y, dn_s = _dn(proj, small, small_t, jnp.zeros((nbatch, 8, DN_CONV_DIM), F32),
                     jnp.zeros((nbatch, DN_HEADS, DN_DK, DN_DV), F32), p,
                     nbatch=nbatch, nch=nchd, L=DN_CHUNK, nvalid=DN_CHUNK)
    dn_buf = p3[:, seq - (CONV_K - 1):, B_DNQ * BLK:B_DNQ * BLK + DN_CONV_DIM]
    y = _out_proj((s5_y, ssd_y, mla_y, dn_y), x, mod3, w_out, tm=tm, rows_per_mod=seq)
    states = (lat.reshape(nbatch, seq, -1), kpe.reshape(nbatch, seq, -1), s5_h, ssd_h, ssd_buf, dn_s, dn_buf)
    return y, states


def _layer_sample(x, mod, p, w_in, w_out, tabs, li, st, caches, page_table, td):
    t = x.shape[0]
    nseq = t // ROWS_S
    tm = min(256, t)
    s5_h0, ssd_h0, ssd_buf, dn_s0, dn_buf = st
    proj = _in_proj(x, mod, p['norm_g'], w_in, tm=tm, rows_per_mod=1)
    p3 = proj.reshape(nseq, ROWS_S, NCOL)
    u_g = p3[:, :td, :BLK].reshape(nseq, td, S5_GROUPS, S5_CH).transpose(2, 0, 1, 3)
    u_g = u_g.reshape(S5_GROUPS, nseq, td * S5_CH).astype(BF16)
    h0 = jnp.concatenate([s5_h0[..., 0], s5_h0[..., 1]], axis=-1).transpose(1, 0, 2)
    y5, hl5 = _s5_core(u_g, h0, tabs['s5_sample'], nbatch=nseq, nch=1, scan=False)
    y5 = y5.reshape(S5_GROUPS, nseq, td, S5_CH).transpose(1, 2, 0, 3).reshape(nseq, td, BLK)
    y5 = jnp.pad(y5, ((0, 0), (0, ROWS_S - td), (0, 0))).reshape(t, BLK)
    s5_y = _s5_post(y5, proj, p, tm=tm)
    s5_h = jnp.stack([hl5[..., :S5_STATE], hl5[..., S5_STATE:]], axis=-1).transpose(1, 0, 2, 3)
    small, small_t = _small_arrays(proj, nseq, 1, ROWS_S)
    ssd_y, ssd_h = _ssd(proj, small, small_t, _pad_rows(ssd_buf), ssd_h0, p,
                        nbatch=nseq, nch=1, L=ROWS_S, nvalid=td)
    ssd_buf_new = p3[:, td - (CONV_K - 1):td, B_SSDX * BLK:B_SSDX * BLK + SSD_CONV_DIM]
    dn_y, dn_s = _dn(proj, small, small_t, _pad_rows(dn_buf), dn_s0, p,
                     nbatch=nseq, nch=1, L=ROWS_S, nvalid=td)
    dn_buf_new = p3[:, td - (CONV_K - 1):td, B_DNQ * BLK:B_DNQ * BLK + DN_CONV_DIM]
    q, lat, kpe = _mla_pre(proj, tabs['cos_s'], tabs['sin_s'], p, tm=tm, pos_blocks=1, expand_kv=False)
    n_pages = page_table.shape[1]
    pps = min(16, n_pages // 2)
    o_lat = _decode(page_table, q, lat, kpe, caches[0], caches[1], p, li=li, pps=pps, ts=min(512, pps * PAGE_SIZE))
    mla_y = _mla_post(o_lat, proj, p, sb=min(32, nseq))
    y = _out_proj((s5_y, ssd_y, mla_y, dn_y), x, mod, w_out, tm=tm, rows_per_mod=1)
    lat3 = lat.reshape(nseq, ROWS_S, -1)[:, :td]
    kpe3 = kpe.reshape(nseq, ROWS_S, -1)[:, :td]
    return y, (lat3, kpe3, s5_h, ssd_h, ssd_buf_new, dn_s, dn_buf_new)


def kernel(x_prompt, x_sample, c_prompt, c_sample, cache_kv_latent, cache_k_rope, state_s5, state_ssd, state_ssd_conv, state_dn, state_dn_conv, page_table, norm_g, ada_w, ada_b, w_in, w_out, s5_lam_re, s5_lam_im, s5_log_dt, s5_b_re, s5_b_im, s5_c_re, s5_c_im, s5_d, s5_glu_w, s5_glu_b, ssd_conv_w, ssd_conv_b, ssd_dt_bias, ssd_a_log, ssd_d, ssd_norm_g, mla_q_norm_g, mla_kv_norm_g, mla_w_uq, mla_w_ukv, mla_q_g, mla_k_g, dn_conv_w, dn_a_log, dn_dt_bias, dn_norm_g):
    weights = dict(
        norm_g=norm_g, s5_lam_re=s5_lam_re, s5_lam_im=s5_lam_im, s5_log_dt=s5_log_dt,
        s5_b_re=s5_b_re, s5_b_im=s5_b_im, s5_c_re=s5_c_re, s5_c_im=s5_c_im,
        s5_d=s5_d, s5_glu_w=s5_glu_w, s5_glu_b=s5_glu_b,
        ssd_conv_w=ssd_conv_w, ssd_conv_b=ssd_conv_b, ssd_dt_bias=ssd_dt_bias,
        ssd_a_log=ssd_a_log, ssd_d=ssd_d, ssd_norm_g=ssd_norm_g,
        mla_q_norm_g=mla_q_norm_g, mla_kv_norm_g=mla_kv_norm_g, mla_w_uq=mla_w_uq,
        mla_w_ukv=mla_w_ukv, mla_q_g=mla_q_g, mla_k_g=mla_k_g,
        dn_conv_w=dn_conv_w, dn_a_log=dn_a_log, dn_dt_bias=dn_dt_bias, dn_norm_g=dn_norm_g)
    depth = w_in.shape[0]
    bp, tp, d = x_prompt.shape
    nseq, td, _ = x_sample.shape
    past_len = page_table.shape[1] * PAGE_SIZE

    n_s = nseq * ROWS_S
    c_all = jnp.concatenate([jnp.repeat(c_sample, ROWS_S, axis=0), c_prompt,
                             jnp.zeros((8 - bp, d), F32)], axis=0)
    mod_all = _ada(c_all, ada_w, ada_b)

    tm_p = min(512, tp)
    cos_p, sin_p = _rope_tables(jnp.arange(tp, dtype=F32))
    cos_s, sin_s = _rope_tables(jnp.arange(ROWS_S, dtype=F32) + past_len)
    reps = min(256, n_s) // ROWS_S
    tabs_pos = dict(cos_p=cos_p, sin_p=sin_p, cos_s=jnp.tile(cos_s, (reps, 1)), sin_s=jnp.tile(sin_s, (reps, 1)))

    y_p = x_prompt.reshape(bp * tp, d)
    y_s = jnp.pad(x_sample, ((0, 0), (0, ROWS_S - td), (0, 0))).reshape(n_s, d)
    p_states, s_states = [], []
    nsteps = max(1, int(math.log2(tp // S5_CHUNK)))
    for li in range(depth):
        p = {name: w[li] for name, w in weights.items()}
        w_in_l = _prep_w_in(w_in[li])
        w_out_l = w_out[li].astype(BF16)
        tabs = dict(tabs_pos, s5_prompt=_s5_tables(p, S5_CHUNK, nsteps), s5_sample=_s5_tables(p, td, 1))
        mod_s = mod_all[li]
        mod_p = mod_all[li, n_s:n_s + bp].reshape(bp, 1, 3 * d)
        y_p, st_p = _layer_prompt(y_p, mod_p, p, w_in_l, w_out_l, tabs, bp, tp)
        p_states.append(st_p)
        st = (state_s5[li], state_ssd[li], state_ssd_conv[li], state_dn[li], state_dn_conv[li])
        y_s, st_s = _layer_sample(y_s, mod_s, p, w_in_l, w_out_l, tabs, li, st,
                                  (cache_kv_latent, cache_k_rope), page_table, td)
        s_states.append(st_s)

    stack = lambda states, i: jnp.stack([s[i] for s in states], axis=0)
    return ((y_p.reshape(bp, tp, d), y_s.reshape(nseq, ROWS_S, d)[:, :td])
            + tuple(stack(p_states, i) for i in range(7))
            + tuple(stack(s_states, i) for i in range(7)))
```

```python
import functools
import math

import numpy as np
import jax
import jax.numpy as jnp
from jax import lax
from jax.experimental import pallas as pl
from jax.experimental.pallas import tpu as pltpu

F32 = jnp.float32
BF16 = jnp.bfloat16
EPS = 1e-6

D_MODEL = 2048
BRANCH = 512
CONV_K = 4
S5_CH = 16
S5_GROUPS = 32
S5_STATE = 64
S5_CHUNK = 16
SSD_HEADDIM = 64
SSD_HEADS = 8
SSD_GROUPS = 2
SSD_STATE = 128
SSD_CHUNK = 128
SSD_CONV_DIM = BRANCH + 2 * SSD_GROUPS * SSD_STATE
MLA_NOPE = 128
MLA_ROPE = 64
MLA_QK = MLA_NOPE + MLA_ROPE
MLA_V = 128
MLA_HEADS = 4
MLA_Q_LORA = 384
MLA_KV_LORA = 256
ROPE_THETA = 10000.0
DN_DK = 128
DN_DV = 128
DN_HEADS = 4
DN_CHUNK = 64
DN_CONV_DIM = 2 * DN_HEADS * DN_DK + DN_HEADS * DN_DV
PAGE_SIZE = 128

IN_SIZES = (BRANCH, BRANCH, BRANCH, SSD_CONV_DIM, SSD_HEADS, MLA_Q_LORA, MLA_KV_LORA, MLA_ROPE, BRANCH,
            DN_CONV_DIM, BRANCH, DN_HEADS, DN_HEADS)

ROWS_S = 8
BLK = 512
NBLK = 12
NCOL = NBLK * BLK
B_S5U, B_S5Z, B_SSDZ, B_SSDX, B_SSDBC, B_MLAZ, B_DNQ, B_DNK, B_DNV, B_DNZ, B_MLAQ, B_MLAKV = range(NBLK)
SMALL_OFF = MLA_Q_LORA + MLA_ROPE
NEG = -1e30
VMEM_LIMIT = 52 * 1024 * 1024


def _cparams(sem, vmem=VMEM_LIMIT):
    return pltpu.CompilerParams(dimension_semantics=sem, vmem_limit_bytes=vmem)


def _bdot(a, b):
    return jnp.dot(a.astype(BF16), b.astype(BF16), preferred_element_type=F32)


def _nt(a, b):
    return lax.dot_general(a.astype(BF16), b.astype(BF16), (((1,), (1,)), ((), ())), preferred_element_type=F32)


def _tn(a, b):
    return lax.dot_general(a.astype(BF16), b.astype(BF16), (((0,), (0,)), ((), ())), preferred_element_type=F32)


def _split(x, n):
    parts = []
    r = x
    for _ in range(n):
        p = r.astype(BF16)
        parts.append(p)
        r = r - p.astype(F32)
    return parts


def _dot_exact_l(t01, x):
    t = t01.astype(BF16)
    return sum(jnp.dot(t, p, preferred_element_type=F32) for p in _split(x, 3))


def _dot_exact_r(x, t01):
    t = t01.astype(BF16)
    return sum(jnp.dot(p, t, preferred_element_type=F32) for p in _split(x, 3))


def _dot3(a, b):
    a1, a2 = _split(a, 2)
    b1, b2 = _split(b, 2)
    d = functools.partial(jnp.dot, preferred_element_type=F32)
    return d(a1, b1) + (d(a1, b2) + d(a2, b1))


def _silu(x):
    return x * jax.nn.sigmoid(x)


def _iota(shape, dim):
    return lax.broadcasted_iota(jnp.int32, shape, dim)


def _ada_kernel(c_ref, w_ref, b_ref, o_ref):
    o_ref[0] = _bdot(_silu(c_ref[...]), w_ref[0]) + b_ref[0]


def _ada(c_all, ada_w, ada_b):
    depth, d, n3 = ada_w.shape
    rows = c_all.shape[0]
    tn = 1024
    return pl.pallas_call(
        _ada_kernel,
        grid=(depth, n3 // tn),
        in_specs=[pl.BlockSpec((rows, d), lambda l, j: (0, 0)),
                  pl.BlockSpec((1, d, tn), lambda l, j: (l, 0, j)),
                  pl.BlockSpec((1, 1, tn), lambda l, j: (l, 0, j))],
        out_specs=pl.BlockSpec((1, rows, tn), lambda l, j: (l, 0, j)),
        out_shape=jax.ShapeDtypeStruct((depth, rows, n3), F32),
        compiler_params=_cparams(("arbitrary", "arbitrary")),
        name="ada",
    )(c_all, ada_w, ada_b.reshape(depth, 1, n3))


def _inproj_kernel(x_ref, mod_ref, g_ref, w_ref, o_ref, h_ref):
    @pl.when(pl.program_id(1) == 0)
    def _():
        x = x_ref[...]
        xn = x * lax.rsqrt(jnp.mean(x * x, axis=-1, keepdims=True) + EPS) * g_ref[...]
        mod = mod_ref[0]
        h = xn * (1.0 + mod[..., D_MODEL:]) + mod[..., :D_MODEL]
        h_ref[...] = h.reshape(-1, D_MODEL).astype(BF16)

    o_ref[...] = jnp.dot(h_ref[...], w_ref[...], preferred_element_type=F32)


def _row_maps(x_shape, nb, rows, mod_row0):
    per = x_shape[1] // rows
    if nb == 1:
        x_map = lambda i: (i // per, i % per, 0)
        m_row = lambda i: mod_row0 + i // per
    else:
        x_map = lambda i: (i, 0, 0)
        m_row = lambda i: i
    return x_map, m_row, (x_shape[0] // nb) * per


def _in_proj(x, mod4, norm_g, w, *, li, nb, rows, mod_row0):
    d = x.shape[-1]
    tn = 1536
    tm = nb * rows
    x_map, m_row, grid_m = _row_maps(x.shape, nb, rows, mod_row0)
    return pl.pallas_call(
        _inproj_kernel,
        grid=(grid_m, NCOL // tn),
        in_specs=[pl.BlockSpec((nb, rows, d), lambda i, j: x_map(i)),
                  pl.BlockSpec((1, nb, 1, 2 * d), lambda i, j: (li, m_row(i), 0, 0)),
                  pl.BlockSpec((1, d), lambda i, j: (0, 0)),
                  pl.BlockSpec((d, tn), lambda i, j: (0, j))],
        out_specs=pl.BlockSpec((tm, tn), lambda i, j: (i, j)),
        out_shape=jax.ShapeDtypeStruct((x.shape[0] * x.shape[1], NCOL), F32),
        scratch_shapes=[pltpu.VMEM((tm, d), BF16)],
        compiler_params=_cparams(("arbitrary", "arbitrary")),
        name="in_proj",
    )(x, mod4, norm_g.reshape(1, d), w)


def _prep_w_in(w):
    o = np.concatenate([[0], np.cumsum(IN_SIZES)])
    s5_u, s5_z, ssd_z, xbc, dt, cq, ckv, kpe, mla_z, qkv, dn_z, dn_a, dn_b = [
        w[:, int(o[i]):int(o[i + 1])] for i in range(len(IN_SIZES))]
    half = MLA_ROPE // 2
    partner = jnp.concatenate([-kpe[:, half:], kpe[:, :half]], axis=1)
    zeros = lambda n: jnp.zeros((w.shape[0], n), w.dtype)
    cols = [s5_u, s5_z, ssd_z, xbc, mla_z, qkv, dn_z,
            cq, kpe, dt, dn_a, dn_b, zeros(BLK - SMALL_OFF - 16),
            ckv, partner, zeros(BLK - MLA_KV_LORA - MLA_ROPE)]
    return jnp.concatenate(cols, axis=1).astype(BF16)


def _cmul(x, tr, ti):
    return x * tr + pltpu.roll(x, S5_STATE, axis=1) * ti


def _s5_kernel(u_ref, toep_ref, w_ref, v_ref, h0_ref, tr_ref, ti_ref, y_ref, hl_ref, *, nbatch, nch, scan):
    u = u_ref[0]
    s_in = jnp.dot(u, w_ref[0], preferred_element_type=F32)
    tr = tr_ref[0]
    ti = ti_ref[0]
    if scan:
        rows = _iota((nch, 2 * S5_STATE), 0)
        h_in, h_last = [], []
        for b in range(nbatch):
            x = s_in[b * nch:(b + 1) * nch]
            for k in range(int(math.log2(nch))):
                d = 1 << k
                xs = jnp.where(rows >= d, pltpu.roll(x, d, axis=0), 0.0)
                x = x + _cmul(xs, tr[k:k + 1], ti[k:k + 1])
            h_in.append(jnp.where(rows >= 1, pltpu.roll(x, 1, axis=0), 0.0))
            h_last.append(x[nch - 1:nch])
        h_in = jnp.concatenate(h_in, axis=0)
        hl_ref[0] = jnp.concatenate(h_last, axis=0)
    else:
        h_in = h0_ref[0]
        hl_ref[0] = _cmul(h_in, tr[0:1], ti[0:1]) + s_in
    y_ref[0] = jnp.dot(u, toep_ref[0], preferred_element_type=F32) + _bdot(h_in, v_ref[0])


def _s5_tables(p, L, nsteps):
    lam_r, lam_i = p['s5_lam_re'], p['s5_lam_im']
    delta = jnp.exp(p['s5_log_dt'])[:, None]
    lr, li = lam_r * delta, lam_i * delta

    def apow(k):
        k = jnp.asarray(k, F32)
        e = jnp.exp(lr[:, None, :] * k[None, :, None])
        return e * jnp.cos(li[:, None, :] * k[None, :, None]), e * jnp.sin(li[:, None, :] * k[None, :, None])

    a_r, a_i = apow(jnp.ones((1,), F32))
    a_r, a_i = a_r[:, 0], a_i[:, 0]
    den = lam_r * lam_r + lam_i * lam_i
    q_r = ((a_r - 1.0) * lam_r + a_i * lam_i) / den
    q_i = (a_i * lam_r - (a_r - 1.0) * lam_i) / den
    b_r = q_r[..., None] * p['s5_b_re'] - q_i[..., None] * p['s5_b_im']
    b_i = q_r[..., None] * p['s5_b_im'] + q_i[..., None] * p['s5_b_re']
    c_r, c_i = p['s5_c_re'], p['s5_c_im']

    pk_r, pk_i = apow(np.arange(L + 1))
    d_r = pk_r[:, :L, :, None] * b_r[:, None] - pk_i[:, :L, :, None] * b_i[:, None]
    d_i = pk_r[:, :L, :, None] * b_i[:, None] + pk_i[:, :L, :, None] * b_r[:, None]
    kern = jnp.einsum('gcp,gtpd->gtcd', c_r, d_r) - jnp.einsum('gcp,gtpd->gtcd', c_i, d_i)
    s_i = np.arange(L)[:, None]
    t_i = np.arange(L)[None, :]
    tau = np.clip(t_i - s_i, 0, L - 1)
    toep = jnp.where((t_i >= s_i)[None, :, :, None, None], kern[:, tau], 0.0)
    toep = toep.transpose(0, 1, 4, 2, 3).reshape(S5_GROUPS, L * S5_CH, L * S5_CH)
    rev = L - 1 - np.arange(L)
    bt_r, bt_i = jnp.swapaxes(b_r, 1, 2)[:, None], jnp.swapaxes(b_i, 1, 2)[:, None]
    w_r = pk_r[:, rev][:, :, None, :] * bt_r - pk_i[:, rev][:, :, None, :] * bt_i
    w_i = pk_r[:, rev][:, :, None, :] * bt_i + pk_i[:, rev][:, :, None, :] * bt_r
    w_mat = jnp.concatenate([w_r, w_i], axis=-1).reshape(S5_GROUPS, L * S5_CH, 2 * S5_STATE)
    v_r = c_r[:, None] * pk_r[:, 1:L + 1][:, :, None, :] - c_i[:, None] * pk_i[:, 1:L + 1][:, :, None, :]
    v_i = c_r[:, None] * pk_i[:, 1:L + 1][:, :, None, :] + c_i[:, None] * pk_r[:, 1:L + 1][:, :, None, :]
    v_mat = jnp.concatenate([v_r, -v_i], axis=-1).reshape(S5_GROUPS, L * S5_CH, 2 * S5_STATE)
    v_mat = jnp.swapaxes(v_mat, 1, 2)
    s_r, s_i = apow((2.0 ** np.arange(nsteps)) * L)
    tr = jnp.concatenate([s_r, s_r], axis=-1)
    ti = jnp.concatenate([-s_i, s_i], axis=-1)
    return toep.astype(BF16), w_mat.astype(BF16), v_mat.astype(BF16), tr, ti


def _s5_core(u_g, h0_g, tables, *, nbatch, nch, scan):
    toep, w_mat, v_mat, tr, ti = tables
    g, r, lc = u_g.shape
    nk = tr.shape[1]
    nb_out = nbatch if scan else r
    blk = lambda shape: pl.BlockSpec((1,) + shape, lambda i: (i, 0, 0))
    kern = functools.partial(_s5_kernel, nbatch=nbatch, nch=nch, scan=scan)
    return pl.pallas_call(
        kern,
        grid=(g,),
        in_specs=[blk((r, lc)), blk((lc, lc)), blk((lc, 2 * S5_STATE)), blk((2 * S5_STATE, lc)),
                  blk(h0_g.shape[1:]), blk((nk, 2 * S5_STATE)), blk((nk, 2 * S5_STATE))],
        out_specs=[blk((r, lc)), blk((nb_out, 2 * S5_STATE))],
        out_shape=[jax.ShapeDtypeStruct((g, r, lc), F32), jax.ShapeDtypeStruct((g, nb_out, 2 * S5_STATE), F32)],
        compiler_params=_cparams(("arbitrary",)),
        name="s5_core",
    )(u_g, toep, w_mat, v_mat, h0_g, tr, ti)


def _s5_post_kernel(yc_ref, u_ref, z_ref, d_ref, w_ref, b_ref, o_ref):
    y = jax.nn.gelu(yc_ref[...] + d_ref[...] * u_ref[...])
    y = y * jax.nn.sigmoid(_bdot(y, w_ref[...]) + b_ref[...])
    o_ref[...] = (y * _silu(z_ref[...])).astype(BF16)


def _s5_post(ycore, proj, p, *, tm):
    t = ycore.shape[0]
    row = lambda blk: pl.BlockSpec((tm, BLK), lambda i, blk=blk: (i, blk))
    full = lambda shape: pl.BlockSpec(shape, lambda i: (0, 0))
    return pl.pallas_call(
        _s5_post_kernel,
        grid=(t // tm,),
        in_specs=[row(0), row(B_S5U), row(B_S5Z), full((1, BLK)), full((BLK, BLK)), full((1, BLK))],
        out_specs=row(0),
        out_shape=jax.ShapeDtypeStruct((t, BLK), BF16),
        compiler_params=_cparams(("arbitrary",)),
        name="s5_post",
    )(ycore, proj, proj, p['s5_d'].reshape(1, BLK), p['s5_glu_w'].astype(BF16), p['s5_glu_b'].reshape(1, BLK))


def _causal_conv(tail, cur, w_ref, L):
    ext = jnp.concatenate([tail, cur], axis=0)
    base = 8 - (CONV_K - 1)
    out = ext[base:base + L] * w_ref[0:1, :]
    for j in range(1, CONV_K):
        out = out + ext[base + j:base + j + L] * w_ref[j:j + 1, :]
    return out


def _tri(L, strict=False):
    r = _iota((L, L), 0)
    c = _iota((L, L), 1)
    return (r > c) if strict else (r >= c)


def _pair_select(L, a, b):
    return jnp.where(_iota((L, 128), 1) < 64, a, b)


def _seq_specs(nb, L, nch, li, hd, conv_dim, blocks):
    row = lambda blk: pl.BlockSpec((nb, L, BLK), lambda i, c, blk=blk: (i, c, blk))
    specs = [row(b) for b in blocks]
    specs += [pl.BlockSpec((nb, L, 16), lambda i, c: (i, c, 0)),
              pl.BlockSpec((nb, 1, 16, L), lambda i, c: (i, c, 0, 0)),
              pl.BlockSpec((nb, 8, conv_dim), lambda i, c: (i, 0, 0)),
              pl.BlockSpec((1, nb) + hd, lambda i, c: (li, i, 0, 0, 0))]
    out_specs = [pl.BlockSpec((nb, L, BLK), lambda i, c: (i, c, 0)),
                 pl.BlockSpec((nb,) + hd, lambda i, c: (i, 0, 0, 0))]
    return specs, out_specs


def _ssd_kernel(z_ref, x_ref, bc_ref, sm_ref, smt_ref, buf_ref, h0_ref,
                cw_ref, cb_ref, dtb_ref, dtbt_ref, al_ref, alt_ref, drow_ref, ng_ref,
                y_ref, hl_ref, tail_ref, st_ref, *, nb, L, nvalid):
    c = pl.program_id(1)

    @pl.when(c == 0)
    def _():
        tail_ref[...] = buf_ref[...]
        st_ref[...] = h0_ref[0]

    tri = _tri(L)
    triu = (_iota((L, L), 0) <= _iota((L, L), 1)).astype(F32)
    rows128 = _iota((128, SSD_STATE), 0)
    gn = SSD_GROUPS * SSD_STATE
    half = BRANCH // SSD_GROUPS
    lane_lo = _iota((L, 128), 1) < 64
    xs_, dtc_, cumc_, cumr_, bm_, cm_ = {}, {}, {}, {}, {}, {}
    for j in range(nb):
        xbc = jnp.concatenate([x_ref[j], bc_ref[j]], axis=1)
        conv = _silu(_causal_conv(tail_ref[j], xbc, cw_ref, L) + cb_ref[...])
        tail_ref[j] = xbc[L - 8:L]
        xs_[j] = conv[:, :BRANCH]
        dtc = jax.nn.softplus(sm_ref[j, :, 0:SSD_HEADS] + dtb_ref[...])
        dtr = jax.nn.softplus(smt_ref[j, 0, 0:SSD_HEADS, :] + dtbt_ref[...])
        if nvalid < L:
            dtc = jnp.where(_iota(dtc.shape, 0) < nvalid, dtc, 0.0)
            dtr = jnp.where(_iota(dtr.shape, 1) < nvalid, dtr, 0.0)
        dtc_[j] = dtc
        cumc_[j] = _dot_exact_l(tri.astype(F32), dtc * (-jnp.exp(al_ref[...])))
        cumr_[j] = _dot_exact_r(dtr * (-jnp.exp(alt_ref[...])), triu)
        for g in range(SSD_GROUPS):
            bm_[j, g] = conv[:, BRANCH + g * SSD_STATE:BRANCH + (g + 1) * SSD_STATE]
            cm_[j, g] = conv[:, BRANCH + gn + g * SSD_STATE:BRANCH + gn + (g + 1) * SSD_STATE]
    groups = [(j, g) for j in range(nb) for g in range(SSD_GROUPS)]
    pairs = [(j, g, pr) for (j, g) in groups for pr in range(2)]
    heads = lambda c_: (c_[1] * 4 + c_[2] * 2, c_[1] * 4 + c_[2] * 2 + 1)
    col = lambda a, h: a[:, h:h + 1]
    cb = {c_: _nt(cm_[c_], bm_[c_]) for c_ in groups}
    st = {c_: st_ref[c_[0], heads(c_)[0]:heads(c_)[0] + 2].reshape(2 * SSD_HEADDIM, SSD_STATE) for c_ in pairs}
    y_off = {c_: _nt(cm_[c_[:2]], st[c_]) for c_ in pairs}
    xdt, upd, y_diag = {}, {}, {}
    for c_ in pairs:
        j, (h0, h1) = c_[0], heads(c_)
        xdt[c_] = xs_[j][:, h0 * SSD_HEADDIM:(h0 + 2) * SSD_HEADDIM] * jnp.where(lane_lo, col(dtc_[j], h0), col(dtc_[j], h1))
    for c_ in pairs:
        j, (h0, h1) = c_[0], heads(c_)
        edec = jnp.exp(cumc_[j][L - 1:L] - cumc_[j])
        upd[c_] = _tn(xdt[c_] * jnp.where(lane_lo, col(edec, h0), col(edec, h1)), bm_[c_[:2]])
    for c_ in pairs:
        j, (h0, h1) = c_[0], heads(c_)
        yd = [_bdot(cb[c_[:2]] * jnp.exp(jnp.where(tri, col(cumc_[j], h) - cumr_[j][h:h + 1, :], NEG)), xdt[c_])
              for h in (h0, h1)]
        y_diag[c_] = jnp.where(lane_lo, yd[0], yd[1])
    for c_ in pairs:
        j, (h0, h1) = c_[0], heads(c_)
        etot = jnp.exp(cumc_[j][L - 1:L])
        st_new = st[c_] * jnp.where(rows128 < 64, col(etot, h0), col(etot, h1)) + upd[c_]
        st_ref[j, h0:h0 + 2] = st_new.reshape(2, SSD_HEADDIM, SSD_STATE)
    for j in range(nb):
        ecum = jnp.exp(cumc_[j])
        y_parts = []
        for g in range(SSD_GROUPS):
            for pr in range(2):
                h0, h1 = heads((j, g, pr))
                y_parts.append(y_diag[j, g, pr] + y_off[j, g, pr] * jnp.where(lane_lo, col(ecum, h0), col(ecum, h1)))
        xs = xs_[j]
        y = jnp.concatenate(y_parts, axis=1) + drow_ref[...] * xs
        y = y * _silu(z_ref[j])
        outs = []
        for g in range(SSD_GROUPS):
            yg = y[:, g * half:(g + 1) * half]
            outs.append(yg * lax.rsqrt(jnp.mean(yg * yg, axis=-1, keepdims=True) + EPS))
        y_ref[j] = (jnp.concatenate(outs, axis=1) * ng_ref[...]).astype(y_ref.dtype)

    @pl.when(c == pl.num_programs(1) - 1)
    def _():
        hl_ref[...] = st_ref[...]


def _ssd(proj3, small, small_t, buf8, h0, p, *, li, nb, nch, L, nvalid):
    nbatch, r, _ = proj3.shape
    full = lambda shape: pl.BlockSpec(shape, lambda i, c: (0,) * len(shape))
    hd = (SSD_HEADS, SSD_HEADDIM, SSD_STATE)
    specs, out_specs = _seq_specs(nb, L, nch, li, hd, SSD_CONV_DIM, (B_SSDZ, B_SSDX, B_SSDBC))
    kern = functools.partial(_ssd_kernel, nb=nb, L=L, nvalid=nvalid)
    return pl.pallas_call(
        kern,
        grid=(nbatch // nb, nch),
        in_specs=specs + [full((CONV_K, SSD_CONV_DIM)), full((1, SSD_CONV_DIM)),
                          full((1, SSD_HEADS)), full((SSD_HEADS, 1)), full((1, SSD_HEADS)), full((SSD_HEADS, 1)),
                          full((1, BRANCH)), full((1, BRANCH))],
        out_specs=out_specs,
        out_shape=[jax.ShapeDtypeStruct((nbatch, r, BLK), BF16 if L % 16 == 0 else F32),
                   jax.ShapeDtypeStruct((nbatch,) + hd, F32)],
        scratch_shapes=[pltpu.VMEM((nb, 8, SSD_CONV_DIM), F32), pltpu.VMEM((nb,) + hd, F32)],
        compiler_params=_cparams(("arbitrary", "arbitrary")),
        name="ssd",
    )(proj3, proj3, proj3, small, small_t, buf8, h0,
      p['ssd_conv_w'], p['ssd_conv_b'].reshape(1, -1),
      p['ssd_dt_bias'].reshape(1, -1), p['ssd_dt_bias'].reshape(-1, 1),
      p['ssd_a_log'].reshape(1, -1), p['ssd_a_log'].reshape(-1, 1),
      jnp.repeat(p['ssd_d'], SSD_HEADDIM).reshape(1, BRANCH), p['ssd_norm_g'].reshape(1, BRANCH))


def _dn_kernel(q_ref, k_ref, v_ref, z_ref, sm_ref, smt_ref, buf_ref, s0_ref,
               cw_ref, al_ref, alt_ref, dtb_ref, dtbt_ref, ng_ref,
               y_ref, sl_ref, tail_ref, st_ref, *, nb, L, nvalid):
    c = pl.program_id(1)

    @pl.when(c == 0)
    def _():
        tail_ref[...] = buf_ref[...]
        st_ref[...] = s0_ref[0]

    a_off = SSD_HEADS
    b_off = SSD_HEADS + DN_HEADS
    tri = _tri(L)
    stri = _tri(L, strict=True)
    triu = (_iota((L, L), 0) <= _iota((L, L), 1)).astype(F32)
    eye = (_iota((L, L), 0) == _iota((L, L), 1)).astype(F32)
    kd = DN_HEADS * DN_DK
    chains = [(j, h) for j in range(nb) for h in range(DN_HEADS)]
    qs, ks, vs, gcis, bcols, gammas = {}, {}, {}, {}, {}, {}
    for j in range(nb):
        qkv = jnp.concatenate([q_ref[j], k_ref[j], v_ref[j]], axis=1)
        conv = _silu(_causal_conv(tail_ref[j], qkv, cw_ref, L))
        tail_ref[j] = qkv[L - 8:L]

        gc = -jnp.exp(al_ref[...]) * jax.nn.softplus(sm_ref[j, :, a_off:a_off + DN_HEADS] + dtb_ref[...])
        gr = -jnp.exp(alt_ref[...]) * jax.nn.softplus(smt_ref[j, 0, a_off:a_off + DN_HEADS, :] + dtbt_ref[...])
        beta = jax.nn.sigmoid(sm_ref[j, :, b_off:b_off + DN_HEADS])
        if nvalid < L:
            vc = _iota(gc.shape, 0) < nvalid
            gc = jnp.where(vc, gc, 0.0)
            beta = jnp.where(vc, beta, 0.0)
            gr = jnp.where(_iota(gr.shape, 1) < nvalid, gr, 0.0)
        gcc = _dot_exact_l(tri.astype(F32), gc)
        gcr = _dot_exact_r(gr, triu)
        for h in range(DN_HEADS):
            qh = conv[:, h * DN_DK:(h + 1) * DN_DK]
            kh = conv[:, kd + h * DN_DK:kd + (h + 1) * DN_DK]
            qs[j, h] = qh * lax.rsqrt(jnp.sum(qh * qh, axis=-1, keepdims=True) + EPS) * (DN_DK ** -0.5)
            ks[j, h] = kh * lax.rsqrt(jnp.sum(kh * kh, axis=-1, keepdims=True) + EPS)
            vs[j, h] = conv[:, 2 * kd + h * DN_DV:2 * kd + (h + 1) * DN_DV]
            gcis[j, h] = gcc[:, h:h + 1]
            bcols[j, h] = beta[:, h:h + 1]
            gammas[j, h] = jnp.exp(jnp.where(tri, gcc[:, h:h + 1] - gcr[h:h + 1, :], NEG))

    kk = {c_: _nt(ks[c_], ks[c_]) for c_ in chains}
    qk = {c_: _nt(qs[c_], ks[c_]) for c_ in chains}
    pw = {c_: jnp.where(stri, -(bcols[c_] * kk[c_] * gammas[c_]), 0.0) for c_ in chains}
    inv = {c_: eye + pw[c_] for c_ in chains}
    for _ in range(int(math.log2(L)) - 1):
        pw = {c_: _dot3(pw[c_], pw[c_]) for c_ in chains}
        inv = {c_: inv[c_] + _dot3(inv[c_], pw[c_]) for c_ in chains}
    sol = {c_: _dot3(inv[c_], jnp.concatenate([vs[c_] * bcols[c_], ks[c_] * (bcols[c_] * jnp.exp(gcis[c_]))], axis=1))
           for c_ in chains}
    st = {c_: st_ref[c_[0], c_[1]] for c_ in chains}
    v_new = {c_: sol[c_][:, :DN_DV] - _bdot(sol[c_][:, DN_DV:], st[c_]) for c_ in chains}
    o_s = {c_: _bdot(qs[c_] * jnp.exp(gcis[c_]), st[c_]) for c_ in chains}
    o_a = {c_: _bdot(jnp.where(tri, qk[c_] * gammas[c_], 0.0), v_new[c_]) for c_ in chains}
    upd = {c_: _tn(ks[c_] * jnp.exp(gcis[c_][L - 1:L] - gcis[c_]), v_new[c_]) for c_ in chains}
    for c_ in chains:
        st_ref[c_[0], c_[1]] = st[c_] * jnp.exp(gcis[c_][L - 1:L]) + upd[c_]
    for j in range(nb):
        outs = []
        for h in range(DN_HEADS):
            o = o_s[j, h] + o_a[j, h]
            o = o * lax.rsqrt(jnp.mean(o * o, axis=-1, keepdims=True) + EPS) * ng_ref[...]
            outs.append(o * _silu(z_ref[j, :, h * DN_DV:(h + 1) * DN_DV]))
        y_ref[j] = jnp.concatenate(outs, axis=1).astype(y_ref.dtype)

    @pl.when(c == pl.num_programs(1) - 1)
    def _():
        sl_ref[...] = st_ref[...]


def _dn(proj3, small, small_t, buf8, s0, p, *, li, nb, nch, L, nvalid):
    nbatch, r, _ = proj3.shape
    full = lambda shape: pl.BlockSpec(shape, lambda i, c: (0,) * len(shape))
    hd = (DN_HEADS, DN_DK, DN_DV)
    specs, out_specs = _seq_specs(nb, L, nch, li, hd, DN_CONV_DIM, (B_DNQ, B_DNK, B_DNV, B_DNZ))
    kern = functools.partial(_dn_kernel, nb=nb, L=L, nvalid=nvalid)
    return pl.pallas_call(
        kern,
        grid=(nbatch // nb, nch),
        in_specs=specs + [full((CONV_K, DN_CONV_DIM)),
                          full((1, DN_HEADS)), full((DN_HEADS, 1)), full((1, DN_HEADS)), full((DN_HEADS, 1)),
                          full((1, DN_DV))],
        out_specs=out_specs,
        out_shape=[jax.ShapeDtypeStruct((nbatch, r, BLK), BF16 if L % 16 == 0 else F32),
                   jax.ShapeDtypeStruct((nbatch,) + hd, F32)],
        scratch_shapes=[pltpu.VMEM((nb, 8, DN_CONV_DIM), F32), pltpu.VMEM((nb,) + hd, F32)],
        compiler_params=_cparams(("arbitrary", "arbitrary")),
        name="dn",
    )(proj3, proj3, proj3, proj3, small, small_t, buf8, s0,
      p['dn_conv_w'],
      p['dn_a_log'].reshape(1, -1), p['dn_a_log'].reshape(-1, 1),
      p['dn_dt_bias'].reshape(1, -1), p['dn_dt_bias'].reshape(-1, 1),
      p['dn_norm_g'].reshape(1, -1))


def _mla_pre_kernel(a_ref, b_ref, cos_ref, sin_ref, qng_ref, kvng_ref, wuq_ref, gq_ref, *rest, expand_kv):
    if expand_kv:
        wukv_ref, gk_ref, q_ref, lat_ref, kpe_ref, k_ref, v_ref = rest
    else:
        q_ref, lat_ref, kpe_ref = rest
    cos = cos_ref[...]
    sin = sin_ref[...]
    cq = a_ref[:, :MLA_Q_LORA]
    cqn = cq * lax.rsqrt(jnp.mean(cq * cq, axis=-1, keepdims=True) + EPS) * qng_ref[...]
    qp = _bdot(cqn, wuq_ref[...])
    for h in range(MLA_HEADS):
        nope = qp[:, h * 384:h * 384 + 128]
        ro = qp[:, h * 384 + 128:h * 384 + 256] * cos + qp[:, h * 384 + 256:h * 384 + 384] * sin
        ms = (jnp.sum(nope * nope, axis=-1, keepdims=True) + jnp.sum(ro * ro, axis=-1, keepdims=True)) / MLA_QK
        rinv = lax.rsqrt(ms + EPS)
        q_ref[:, h * 256:h * 256 + 128] = (nope * rinv * gq_ref[:, :128]).astype(q_ref.dtype)
        q_ref[:, h * 256 + 128:(h + 1) * 256] = (ro * rinv * gq_ref[:, 128:]).astype(q_ref.dtype)
    ckv = b_ref[:, :MLA_KV_LORA]
    lat = ckv * lax.rsqrt(jnp.mean(ckv * ckv, axis=-1, keepdims=True) + EPS) * kvng_ref[...]
    lat_ref[...] = lat
    kpe = a_ref[:, MLA_Q_LORA:] * cos + b_ref[:, MLA_KV_LORA:MLA_KV_LORA + 128] * sin
    kpe_ref[...] = kpe[:, :MLA_ROPE]
    if expand_kv:
        kv = _bdot(lat, wukv_ref[...])
        pe_sq = jnp.sum(kpe * kpe, axis=-1, keepdims=True)
        for h in range(MLA_HEADS):
            kn = kv[:, h * 256:h * 256 + 128]
            rinv = lax.rsqrt((jnp.sum(kn * kn, axis=-1, keepdims=True) + pe_sq) / MLA_QK + EPS)
            k_ref[:, h * 256:h * 256 + 128] = (kn * rinv * gk_ref[:, :128]).astype(BF16)
            k_ref[:, h * 256 + 128:(h + 1) * 256] = (kpe * rinv * gk_ref[:, 128:]).astype(BF16)
            v_ref[:, h * 256:h * 256 + 128] = kv[:, h * 256 + 128:(h + 1) * 256].astype(BF16)
            v_ref[:, h * 256 + 128:(h + 1) * 256] = jnp.ones((kn.shape[0], 128), BF16)


def _pad_gain(g, scale=1.0):
    return jnp.concatenate([g * scale, jnp.zeros((256 - MLA_QK,), F32)]).reshape(1, 256)


def _mla_pre(proj, cos, sin, p, *, tm, pos_blocks, expand_kv):
    t = proj.shape[0]
    half = MLA_ROPE // 2
    wq = p['mla_w_uq'].reshape(MLA_Q_LORA, MLA_HEADS, MLA_QK)
    nope, ro = wq[..., :MLA_NOPE], wq[..., MLA_NOPE:]
    zeros = jnp.zeros((MLA_Q_LORA, MLA_HEADS, 128 - MLA_ROPE), F32)
    partner = jnp.concatenate([-ro[..., half:], ro[..., :half]], axis=-1)
    wuq = jnp.concatenate([nope, ro, zeros, partner, zeros], axis=-1).reshape(MLA_Q_LORA, MLA_HEADS * 384).astype(BF16)
    row = lambda blk: pl.BlockSpec((tm, BLK), lambda i, blk=blk: (i, blk))
    tab = pl.BlockSpec((tm, 128), lambda i: (i % pos_blocks, 0))
    full = lambda shape: pl.BlockSpec(shape, lambda i: (0, 0))
    in_specs = [row(B_MLAQ), row(B_MLAKV), tab, tab, full((1, MLA_Q_LORA)), full((1, MLA_KV_LORA)),
                full((MLA_Q_LORA, MLA_HEADS * 384)), full((1, 256))]
    args = [proj, proj, cos, sin, p['mla_q_norm_g'].reshape(1, -1), p['mla_kv_norm_g'].reshape(1, -1),
            wuq, _pad_gain(p['mla_q_g'], MLA_QK ** -0.5)]
    out_specs = [pl.BlockSpec((tm, 1024), lambda i: (i, 0)), pl.BlockSpec((tm, MLA_KV_LORA), lambda i: (i, 0)),
                 pl.BlockSpec((tm, MLA_ROPE), lambda i: (i, 0))]
    out_shape = [jax.ShapeDtypeStruct((t, 1024), BF16 if expand_kv else F32),
                 jax.ShapeDtypeStruct((t, MLA_KV_LORA), F32), jax.ShapeDtypeStruct((t, MLA_ROPE), F32)]
    if expand_kv:
        in_specs += [full((MLA_KV_LORA, MLA_HEADS * 256)), full((1, 256))]
        args += [p['mla_w_ukv'].astype(BF16), _pad_gain(p['mla_k_g'])]
        out_specs += [pl.BlockSpec((tm, 1024), lambda i: (i, 0)), pl.BlockSpec((tm, 1024), lambda i: (i, 0))]
        out_shape += [jax.ShapeDtypeStruct((t, 1024), BF16), jax.ShapeDtypeStruct((t, 1024), BF16)]
    return pl.pallas_call(
        functools.partial(_mla_pre_kernel, expand_kv=expand_kv),
        grid=(t // tm,),
        in_specs=in_specs, out_specs=out_specs, out_shape=out_shape,
        compiler_params=_cparams(("arbitrary",)),
        name="mla_pre",
    )(*args)


def _rope_tables(pos):
    half = MLA_ROPE // 2
    inv = ROPE_THETA ** (-jnp.arange(half, dtype=F32) / half)
    ang = pos[:, None] * inv[None, :]
    z = jnp.zeros((pos.shape[0], 128 - MLA_ROPE), F32)
    cos, sin = jnp.cos(ang), jnp.sin(ang)
    return jnp.concatenate([cos, cos, z], axis=1), jnp.concatenate([sin, sin, z], axis=1)


def _flash_kernel(q_ref, k_ref, v_ref, z_ref, o_ref, m_ref, acc_ref, *, tq, tk):
    i = pl.program_id(2)
    m_ref[...] = jnp.full(m_ref.shape, NEG, F32)
    acc_ref[...] = jnp.zeros(acc_ref.shape, F32)
    q = q_ref[...]
    row = _iota((tq, tk), 0)
    col = _iota((tq, tk), 1)

    def step(j, masked):
        k = k_ref[pl.ds(pl.multiple_of(j * tk, tk), tk), :]
        v = v_ref[pl.ds(pl.multiple_of(j * tk, tk), tk), :]
        s = lax.dot_general(q, k, (((1,), (1,)), ((), ())), preferred_element_type=F32)
        if masked:
            s = jnp.where(col <= row, s, NEG)
        m_prev = m_ref[...]
        m_new = jnp.maximum(m_prev, jnp.max(s, axis=-1, keepdims=True))
        pr = jnp.exp(s - m_new)
        corr = jnp.exp(m_prev - m_new)
        acc_ref[...] = acc_ref[...] * corr + jnp.dot(pr.astype(BF16), v, preferred_element_type=F32)
        m_ref[...] = m_new

    def body(j, carry):
        step(j, False)
        return carry

    lax.fori_loop(0, i, body, 0)
    step(i, True)
    o_ref[...] = (acc_ref[:, :MLA_V] / acc_ref[:, MLA_V:] * _silu(z_ref[...])).astype(BF16)


def _flash(q, k, v, proj, *, nbatch, seq, tq):
    t = q.shape[0]
    nq = seq // tq
    kern = functools.partial(_flash_kernel, tq=tq, tk=tq)
    return pl.pallas_call(
        kern,
        grid=(nbatch, MLA_HEADS, nq),
        in_specs=[pl.BlockSpec((tq, 256), lambda b, h, i: (b * nq + i, h)),
                  pl.BlockSpec((seq, 256), lambda b, h, i: (b, h)),
                  pl.BlockSpec((seq, 2 * MLA_V), lambda b, h, i: (b, h)),
                  pl.BlockSpec((tq, MLA_V), lambda b, h, i: (b * nq + i, B_MLAZ * (BLK // MLA_V) + h))],
        out_specs=pl.BlockSpec((tq, MLA_V), lambda b, h, i: (b * nq + i, h)),
        out_shape=jax.ShapeDtypeStruct((t, BRANCH), BF16),
        scratch_shapes=[pltpu.VMEM((tq, 1), F32), pltpu.VMEM((tq, 2 * MLA_V), F32)],
        compiler_params=_cparams(("arbitrary", "arbitrary", "arbitrary")),
        name="flash",
    )(q, k, v, proj)


def _decode_kernel(pt_ref, q_ref, latn_ref, pen_ref, wukt_ref, gk_ref, lat_hbm, pet_hbm, o_ref,
                   latbuf, pebuf, latb, s_scr, sem, *, li, pps, ts, n_pages):
    b = pl.program_id(0)
    nseq = pl.num_programs(0)
    ng = n_pages // pps
    nrow = MLA_HEADS * ROWS_S
    nk = MLA_HEADS * MLA_NOPE
    per = ts // PAGE_SIZE

    def copies(seq, g, slot, k):
        ph = pt_ref[seq, g * pps + k]
        return (pltpu.make_async_copy(lat_hbm.at[li, ph], latbuf.at[slot, k], sem.at[slot]),
                pltpu.make_async_copy(pet_hbm.at[li, ph], pebuf.at[slot, k], sem.at[slot]))

    def fetch(seq, g, slot):
        for k in range(pps):
            for cp in copies(seq, g, slot, k):
                cp.start()

    def wait(g, slot):
        for k in range(pps):
            for cp in copies(b, g, slot, k):
                cp.wait()

    @pl.when(b == 0)
    def _():
        fetch(0, 0, 0)

    q = q_ref[...]
    gk = gk_ref[...]
    qa, qpe = [], []
    for h in range(MLA_HEADS):
        qn = q[:, h * 256:h * 256 + 128] * gk[:, :128]
        qa.append(_bdot(qn, wukt_ref[h * 128:(h + 1) * 128, :]))
        qpe.append(q[:, h * 256 + 128:h * 256 + 128 + MLA_ROPE] * gk[:, 128:128 + MLA_ROPE])
    amats = [jnp.concatenate([wukt_ref[hp * 2 * MLA_NOPE:(hp + 1) * 2 * MLA_NOPE, :],
                              qa[2 * hp].astype(BF16), qa[2 * hp + 1].astype(BF16)], axis=0)
             for hp in range(MLA_HEADS // 2)]
    qpe = jnp.concatenate(qpe, axis=0).astype(BF16)

    def project(lat_b):
        return [lax.dot_general(a, lat_b, (((1,), (1,)), ((), ())), preferred_element_type=F32) for a in amats]

    def scores(rs, s_pe, pe_sq):
        s = []
        for h in range(MLA_HEADS):
            r = rs[h // 2]
            o = (h % 2) * MLA_NOPE
            kn = r[o:o + MLA_NOPE]
            rinv = lax.rsqrt((jnp.sum(kn * kn, axis=0, keepdims=True) + pe_sq) / MLA_QK + EPS)
            sq = r[2 * MLA_NOPE + (h % 2) * ROWS_S:2 * MLA_NOPE + (h % 2 + 1) * ROWS_S]
            s.append((sq + s_pe[h * ROWS_S:(h + 1) * ROWS_S]) * rinv)
        return jnp.concatenate(s, axis=0)

    def softmax_update(carry, s, lat_b):
        m, l, acc = carry
        m_new = jnp.maximum(m, jnp.max(s, axis=-1, keepdims=True))
        pr = jnp.exp(s - m_new)
        corr = jnp.exp(m - m_new)
        l = l * corr + jnp.sum(pr, axis=-1, keepdims=True)
        acc = acc * corr + jnp.dot(pr.astype(BF16), lat_b, preferred_element_type=F32)
        return m_new, l, acc

    lat_n = latn_ref[...].astype(BF16)
    pe_n = pen_ref[...]
    r_n = project(lat_n)
    s_pe_n = _nt(qpe, pe_n)
    pe_sq_n = _nt(jnp.ones((8, MLA_ROPE), F32), pe_n * pe_n)[0:1]
    qrow = _iota((nrow, ROWS_S), 0) % ROWS_S
    s_n = jnp.where(_iota((nrow, ROWS_S), 1) <= qrow, scores(r_n, s_pe_n, pe_sq_n), NEG)
    carry = (jnp.full((nrow, 1), NEG, F32), jnp.zeros((nrow, 1), F32), jnp.zeros((nrow, MLA_KV_LORA), F32))
    carry = softmax_update(carry, s_n, lat_n)

    def group(g, slot, carry):
        wait(g, slot)
        for k in range(pps // per):
            lat_b = latbuf[slot, k * per:(k + 1) * per].reshape(ts, MLA_KV_LORA).astype(BF16)
            latb[k * ts:(k + 1) * ts, :] = lat_b
            pet = jnp.concatenate([pebuf[slot, k * per + i] for i in range(per)], axis=1)
            s_pe = jnp.dot(qpe, pet.astype(BF16), preferred_element_type=F32)
            pe_sq = jnp.sum(pet * pet, axis=0, keepdims=True)
            s_scr[:, k * ts:(k + 1) * ts] = scores(project(lat_b), s_pe, pe_sq)
        return softmax_update(carry, s_scr[...], latb[...])

    def two_groups(gg, carry):
        fetch(b, 2 * gg + 1, 1)
        carry = group(2 * gg, 0, carry)

        @pl.when(2 * gg + 2 < ng)
        def _():
            fetch(b, 2 * gg + 2, 0)

        @pl.when(jnp.logical_and(2 * gg + 2 >= ng, b + 1 < nseq))
        def _():
            fetch(b + 1, 0, 0)

        return group(2 * gg + 1, 1, carry)

    m, l, acc = lax.fori_loop(0, ng // 2, two_groups, carry)
    o_ref[0] = acc / l


def _decode(page_table, q, lat_new, pe_new, cache_lat, cache_pet, p, *, li, pps, ts):
    nseq, n_pages = page_table.shape
    nrow = MLA_HEADS * ROWS_S
    wukv = p['mla_w_ukv'].reshape(MLA_KV_LORA, MLA_HEADS, MLA_NOPE + MLA_V)
    wukt = wukv[..., :MLA_NOPE].transpose(1, 2, 0).reshape(MLA_HEADS * MLA_NOPE, MLA_KV_LORA).astype(BF16)
    kern = functools.partial(_decode_kernel, li=li, pps=pps, ts=ts, n_pages=n_pages)
    grid_spec = pltpu.PrefetchScalarGridSpec(
        num_scalar_prefetch=1,
        grid=(nseq,),
        in_specs=[pl.BlockSpec((ROWS_S, 1024), lambda b, pt: (b, 0)),
                  pl.BlockSpec((ROWS_S, MLA_KV_LORA), lambda b, pt: (b, 0)),
                  pl.BlockSpec((ROWS_S, MLA_ROPE), lambda b, pt: (b, 0)),
                  pl.BlockSpec((MLA_HEADS * MLA_NOPE, MLA_KV_LORA), lambda b, pt: (0, 0)),
                  pl.BlockSpec((1, 256), lambda b, pt: (0, 0)),
                  pl.BlockSpec(memory_space=pl.ANY),
                  pl.BlockSpec(memory_space=pl.ANY)],
        out_specs=pl.BlockSpec((1, nrow, MLA_KV_LORA), lambda b, pt: (b, 0, 0)),
        scratch_shapes=[pltpu.VMEM((2, pps, PAGE_SIZE, MLA_KV_LORA), F32),
                        pltpu.VMEM((2, pps, MLA_ROPE, PAGE_SIZE), F32),
                        pltpu.VMEM((pps * PAGE_SIZE, MLA_KV_LORA), BF16),
                        pltpu.VMEM((nrow, pps * PAGE_SIZE), F32),
                        pltpu.SemaphoreType.DMA((2,))],
    )
    return pl.pallas_call(
        kern,
        grid_spec=grid_spec,
        out_shape=jax.ShapeDtypeStruct((nseq, nrow, MLA_KV_LORA), F32),
        compiler_params=_cparams(("arbitrary",)),
        name="decode",
    )(page_table, q, lat_new, pe_new, wukt, _pad_gain(p['mla_k_g']), cache_lat, cache_pet)


def _mla_post_kernel(o_ref, z_ref, wuv_ref, y_ref, *, sb):
    for h in range(MLA_HEADS):
        x = o_ref[:, h * ROWS_S:(h + 1) * ROWS_S, :].reshape(sb * ROWS_S, MLA_KV_LORA)
        y = _bdot(x, wuv_ref[h]) * _silu(z_ref[:, h * MLA_V:(h + 1) * MLA_V])
        y_ref[:, h * MLA_V:(h + 1) * MLA_V] = y.astype(BF16)


def _mla_post(o_lat, proj, p, *, sb):
    nseq = o_lat.shape[0]
    nrow = MLA_HEADS * ROWS_S
    wukv = p['mla_w_ukv'].reshape(MLA_KV_LORA, MLA_HEADS, MLA_NOPE + MLA_V)
    wuv = wukv[..., MLA_NOPE:].transpose(1, 0, 2).astype(BF16)
    return pl.pallas_call(
        functools.partial(_mla_post_kernel, sb=sb),
        grid=(nseq // sb,),
        in_specs=[pl.BlockSpec((sb, nrow, MLA_KV_LORA), lambda i: (i, 0, 0)),
                  pl.BlockSpec((sb * ROWS_S, BLK), lambda i: (i, B_MLAZ)),
                  pl.BlockSpec((MLA_HEADS, MLA_KV_LORA, MLA_V), lambda i: (0, 0, 0))],
        out_specs=pl.BlockSpec((sb * ROWS_S, BLK), lambda i: (i, 0)),
        out_shape=jax.ShapeDtypeStruct((nseq * ROWS_S, BLK), BF16),
        compiler_params=_cparams(("arbitrary",)),
        name="mla_post",
    )(o_lat, proj, wuv)


def _outproj_kernel(b0_ref, b1_ref, b2_ref, b3_ref, x_ref, gate_ref, w_ref, y_ref):
    mixed = _bdot(b0_ref[...], w_ref[0:BLK, :])
    for k, br in enumerate((b1_ref, b2_ref, b3_ref), start=1):
        mixed = mixed + _bdot(br[...], w_ref[k * BLK:(k + 1) * BLK, :])
    y_ref[...] = x_ref[...] + gate_ref[0] * mixed.reshape(x_ref.shape)


def _out_proj(branches, x, mod4, w, *, li, nb, rows, mod_row0):
    d = x.shape[-1]
    tm = nb * rows
    x_map, m_row, grid_m = _row_maps(x.shape, nb, rows, mod_row0)
    br = pl.BlockSpec((tm, BLK), lambda i: (i, 0))
    return pl.pallas_call(
        _outproj_kernel,
        grid=(grid_m,),
        in_specs=[br, br, br, br, pl.BlockSpec((nb, rows, d), x_map),
                  pl.BlockSpec((1, nb, 1, d), lambda i: (li, m_row(i), 0, 2)),
                  pl.BlockSpec((d, d), lambda i: (0, 0))],
        out_specs=pl.BlockSpec((nb, rows, d), x_map),
        out_shape=jax.ShapeDtypeStruct(x.shape, F32),
        compiler_params=_cparams(("arbitrary",)),
        name="out_proj",
    )(*[b.reshape(-1, BLK) for b in branches], x, mod4, w)


def _small_arrays(proj3, nch, L):
    nbatch = proj3.shape[0]
    small = proj3[:, :, B_MLAQ * BLK + SMALL_OFF:B_MLAQ * BLK + SMALL_OFF + 16]
    small_t = small.reshape(nbatch, nch, L, 16).transpose(0, 1, 3, 2)
    return small, small_t


def _pad_rows(a):
    return jnp.pad(a, ((0, 0), (8 - (CONV_K - 1), 0), (0, 0)))


def _layer_prompt(x, mod4, p, w_in, w_out, tabs, li, mod_row0):
    nbatch, seq, _ = x.shape
    t = nbatch * seq
    tm = min(512, seq)
    proj = _in_proj(x, mod4, p['norm_g'], w_in, li=li, nb=1, rows=tm, mod_row0=mod_row0)
    p3 = proj.reshape(nbatch, seq, NCOL)
    nch5 = seq // S5_CHUNK
    u_g = proj[:, :BLK].reshape(nbatch * nch5, S5_CHUNK, S5_GROUPS, S5_CH).transpose(2, 0, 1, 3)
    u_g = u_g.reshape(S5_GROUPS, nbatch * nch5, S5_CHUNK * S5_CH).astype(BF16)
    h0 = jnp.zeros((S5_GROUPS, 8, 2 * S5_STATE), F32)
    y5, hl5 = _s5_core(u_g, h0, tabs['s5_prompt'], nbatch=nbatch, nch=nch5, scan=True)
    y5 = y5.reshape(S5_GROUPS, nbatch * nch5, S5_CHUNK, S5_CH).transpose(1, 2, 0, 3).reshape(t, BLK)
    s5_y = _s5_post(y5, proj, p, tm=tm)
    s5_h = jnp.stack([hl5[..., :S5_STATE], hl5[..., S5_STATE:]], axis=-1).transpose(1, 0, 2, 3)
    nch = seq // SSD_CHUNK
    small, small_t = _small_arrays(p3, nch, SSD_CHUNK)
    ssd_y, ssd_h = _ssd(p3, small, small_t, jnp.zeros((nbatch, 8, SSD_CONV_DIM), F32),
                        jnp.zeros((1, nbatch, SSD_HEADS, SSD_HEADDIM, SSD_STATE), F32), p,
                        li=0, nb=nbatch, nch=nch, L=SSD_CHUNK, nvalid=SSD_CHUNK)
    ssd_buf = p3[:, seq - (CONV_K - 1):, B_SSDX * BLK:B_SSDX * BLK + SSD_CONV_DIM]
    q, lat, kpe, k, v = _mla_pre(proj, tabs['cos_p'], tabs['sin_p'], p, tm=tm, pos_blocks=seq // tm, expand_kv=True)
    mla_y = _flash(q, k, v, proj, nbatch=nbatch, seq=seq, tq=tm)
    nchd = seq // DN_CHUNK
    small, small_t = _small_arrays(p3, nchd, DN_CHUNK)
    dn_y, dn_s = _dn(p3, small, small_t, jnp.zeros((nbatch, 8, DN_CONV_DIM), F32),
                     jnp.zeros((1, nbatch, DN_HEADS, DN_DK, DN_DV), F32), p,
                     li=0, nb=nbatch, nch=nchd, L=DN_CHUNK, nvalid=DN_CHUNK)
    dn_buf = p3[:, seq - (CONV_K - 1):, B_DNQ * BLK:B_DNQ * BLK + DN_CONV_DIM]
    y = _out_proj((s5_y, ssd_y, mla_y, dn_y), x, mod4, w_out, li=li, nb=1, rows=tm, mod_row0=mod_row0)
    states = (lat.reshape(nbatch, seq, -1), kpe.reshape(nbatch, seq, -1), s5_h, ssd_h, ssd_buf, dn_s, dn_buf)
    return y, states


def _layer_sample(x, mod4, p, w_in, w_out, tabs, li, st, caches, page_table, td):
    nseq = x.shape[0]
    t = nseq * ROWS_S
    sb = min(32, nseq)
    tm = sb * ROWS_S
    nb = min(8, nseq)
    s5_h0, ssd_h0, ssd_buf, dn_s0, dn_buf = st
    proj = _in_proj(x, mod4, p['norm_g'], w_in, li=li, nb=sb, rows=ROWS_S, mod_row0=0)
    p3 = proj.reshape(nseq, ROWS_S, NCOL)
    u_g = p3[:, :td, :BLK].reshape(nseq, td, S5_GROUPS, S5_CH).transpose(2, 0, 1, 3)
    u_g = u_g.reshape(S5_GROUPS, nseq, td * S5_CH).astype(BF16)
    h0 = jnp.concatenate([s5_h0[li, ..., 0], s5_h0[li, ..., 1]], axis=-1).transpose(1, 0, 2)
    y5, hl5 = _s5_core(u_g, h0, tabs['s5_sample'], nbatch=nseq, nch=1, scan=False)
    y5 = y5.reshape(S5_GROUPS, nseq, td, S5_CH).transpose(1, 2, 0, 3).reshape(nseq, td, BLK)
    y5 = jnp.pad(y5, ((0, 0), (0, ROWS_S - td), (0, 0))).reshape(t, BLK)
    s5_y = _s5_post(y5, proj, p, tm=tm)
    s5_h = jnp.stack([hl5[..., :S5_STATE], hl5[..., S5_STATE:]], axis=-1).transpose(1, 0, 2, 3)
    small, small_t = _small_arrays(p3, 1, ROWS_S)
    ssd_y, ssd_h = _ssd(p3, small, small_t, _pad_rows(ssd_buf[li]), ssd_h0, p,
                        li=li, nb=nb, nch=1, L=ROWS_S, nvalid=td)
    ssd_buf_new = p3[:, td - (CONV_K - 1):td, B_SSDX * BLK:B_SSDX * BLK + SSD_CONV_DIM]
    dn_y, dn_s = _dn(p3, small, small_t, _pad_rows(dn_buf[li]), dn_s0, p,
                     li=li, nb=nb, nch=1, L=ROWS_S, nvalid=td)
    dn_buf_new = p3[:, td - (CONV_K - 1):td, B_DNQ * BLK:B_DNQ * BLK + DN_CONV_DIM]
    q, lat, kpe = _mla_pre(proj, tabs['cos_s'], tabs['sin_s'], p, tm=tm, pos_blocks=1, expand_kv=False)
    n_pages = page_table.shape[1]
    pps = min(32, n_pages // 2)
    o_lat = _decode(page_table, q, lat, kpe, caches[0], caches[1], p, li=li, pps=pps, ts=min(1024, pps * PAGE_SIZE))
    mla_y = _mla_post(o_lat, proj, p, sb=sb)
    y = _out_proj((s5_y, ssd_y, mla_y, dn_y), x, mod4, w_out, li=li, nb=sb, rows=ROWS_S, mod_row0=0)
    lat3 = lat.reshape(nseq, ROWS_S, -1)[:, :td]
    kpe3 = kpe.reshape(nseq, ROWS_S, -1)[:, :td]
    return y, (lat3, kpe3, s5_h, ssd_h, ssd_buf_new, dn_s, dn_buf_new)


def kernel(x_prompt, x_sample, c_prompt, c_sample, cache_kv_latent, cache_k_rope, state_s5, state_ssd, state_ssd_conv, state_dn, state_dn_conv, page_table, norm_g, ada_w, ada_b, w_in, w_out, s5_lam_re, s5_lam_im, s5_log_dt, s5_b_re, s5_b_im, s5_c_re, s5_c_im, s5_d, s5_glu_w, s5_glu_b, ssd_conv_w, ssd_conv_b, ssd_dt_bias, ssd_a_log, ssd_d, ssd_norm_g, mla_q_norm_g, mla_kv_norm_g, mla_w_uq, mla_w_ukv, mla_q_g, mla_k_g, dn_conv_w, dn_a_log, dn_dt_bias, dn_norm_g):
    weights = dict(
        norm_g=norm_g, s5_lam_re=s5_lam_re, s5_lam_im=s5_lam_im, s5_log_dt=s5_log_dt,
        s5_b_re=s5_b_re, s5_b_im=s5_b_im, s5_c_re=s5_c_re, s5_c_im=s5_c_im,
        s5_d=s5_d, s5_glu_w=s5_glu_w, s5_glu_b=s5_glu_b,
        ssd_conv_w=ssd_conv_w, ssd_conv_b=ssd_conv_b, ssd_dt_bias=ssd_dt_bias,
        ssd_a_log=ssd_a_log, ssd_d=ssd_d, ssd_norm_g=ssd_norm_g,
        mla_q_norm_g=mla_q_norm_g, mla_kv_norm_g=mla_kv_norm_g, mla_w_uq=mla_w_uq,
        mla_w_ukv=mla_w_ukv, mla_q_g=mla_q_g, mla_k_g=mla_k_g,
        dn_conv_w=dn_conv_w, dn_a_log=dn_a_log, dn_dt_bias=dn_dt_bias, dn_norm_g=dn_norm_g)
    depth = w_in.shape[0]
    bp, tp, d = x_prompt.shape
    nseq, td, _ = x_sample.shape
    past_len = page_table.shape[1] * PAGE_SIZE

    c_all = jnp.concatenate([c_sample, c_prompt, jnp.zeros((8 - bp, d), F32)], axis=0)
    mod4 = _ada(c_all, ada_w, ada_b).reshape(depth, nseq + 8, 1, 3 * d)

    cos_p, sin_p = _rope_tables(jnp.arange(tp, dtype=F32))
    cos_s, sin_s = _rope_tables(jnp.arange(ROWS_S, dtype=F32) + past_len)
    reps = min(32, nseq)
    tabs_pos = dict(cos_p=cos_p, sin_p=sin_p, cos_s=jnp.tile(cos_s, (reps, 1)), sin_s=jnp.tile(sin_s, (reps, 1)))
    caches = (cache_kv_latent, jnp.swapaxes(cache_k_rope, 2, 3))
    st = (state_s5, state_ssd, state_ssd_conv, state_dn, state_dn_conv)

    y_p = x_prompt
    y_s = jnp.pad(x_sample, ((0, 0), (0, ROWS_S - td), (0, 0)))
    p_states, s_states = [], []
    nsteps = max(1, int(math.log2(tp // S5_CHUNK)))
    for li in range(depth):
        p = {name: w[li] for name, w in weights.items()}
        w_in_l = _prep_w_in(w_in[li])
        w_out_l = w_out[li].astype(BF16)
        tabs = dict(tabs_pos, s5_prompt=_s5_tables(p, S5_CHUNK, nsteps), s5_sample=_s5_tables(p, td, 1))
        y_p, st_p = _layer_prompt(y_p, mod4, p, w_in_l, w_out_l, tabs, li, nseq)
        p_states.append(st_p)
        y_s, st_s = _layer_sample(y_s, mod4, p, w_in_l, w_out_l, tabs, li, st, caches, page_table, td)
        s_states.append(st_s)

    stack = lambda states, i: jnp.stack([s[i] for s in states], axis=0)
    return ((y_p, y_s[:, :td])
            + tuple(stack(p_states, i) for i in range(7))
            + tuple(stack(s_states, i) for i in range(7)))
```

```python
import functools
import math

import numpy as np
import jax
import jax.numpy as jnp
from jax import lax
from jax.experimental import pallas as pl
from jax.experimental.pallas import tpu as pltpu

F32 = jnp.float32
BF16 = jnp.bfloat16
EPS = 1e-6

D_MODEL = 2048
BRANCH = 512
CONV_K = 4
S5_CH = 16
S5_GROUPS = 32
S5_STATE = 64
S5_CHUNK = 16
SSD_HEADDIM = 64
SSD_HEADS = 8
SSD_GROUPS = 2
SSD_STATE = 128
SSD_CHUNK = 128
SSD_CONV_DIM = BRANCH + 2 * SSD_GROUPS * SSD_STATE
MLA_NOPE = 128
MLA_ROPE = 64
MLA_QK = MLA_NOPE + MLA_ROPE
MLA_V = 128
MLA_HEADS = 4
MLA_Q_LORA = 384
MLA_KV_LORA = 256
ROPE_THETA = 10000.0
DN_DK = 128
DN_DV = 128
DN_HEADS = 4
DN_CHUNK = 64
DN_CONV_DIM = 2 * DN_HEADS * DN_DK + DN_HEADS * DN_DV
PAGE_SIZE = 128

IN_SIZES = (BRANCH, BRANCH, BRANCH, SSD_CONV_DIM, SSD_HEADS, MLA_Q_LORA, MLA_KV_LORA, MLA_ROPE, BRANCH,
            DN_CONV_DIM, BRANCH, DN_HEADS, DN_HEADS)

ROWS_S = 8
BLK = 512
NBLK = 12
NCOL = NBLK * BLK
B_S5U, B_S5Z, B_SSDZ, B_SSDX, B_SSDBC, B_MLAZ, B_DNQ, B_DNK, B_DNV, B_DNZ, B_MLAQ, B_MLAKV = range(NBLK)
SMALL_OFF = MLA_Q_LORA + MLA_ROPE
NEG = -1e30
VMEM_LIMIT = 52 * 1024 * 1024


def _cparams(sem, vmem=VMEM_LIMIT):
    return pltpu.CompilerParams(dimension_semantics=sem, vmem_limit_bytes=vmem)


def _bdot(a, b):
    return jnp.dot(a.astype(BF16), b.astype(BF16), preferred_element_type=F32)


def _nt(a, b):
    return lax.dot_general(a.astype(BF16), b.astype(BF16), (((1,), (1,)), ((), ())), preferred_element_type=F32)


def _tn(a, b):
    return lax.dot_general(a.astype(BF16), b.astype(BF16), (((0,), (0,)), ((), ())), preferred_element_type=F32)


def _split(x, n):
    parts = []
    r = x
    for _ in range(n):
        p = r.astype(BF16)
        parts.append(p)
        r = r - p.astype(F32)
    return parts


def _dot_exact_l(t01, x):
    t = t01.astype(BF16)
    return sum(jnp.dot(t, p, preferred_element_type=F32) for p in _split(x, 3))


def _dot_exact_r(x, t01):
    t = t01.astype(BF16)
    return sum(jnp.dot(p, t, preferred_element_type=F32) for p in _split(x, 3))


def _dot3(a, b):
    a1, a2 = _split(a, 2)
    b1, b2 = _split(b, 2)
    d = functools.partial(jnp.dot, preferred_element_type=F32)
    return d(a1, b1) + (d(a1, b2) + d(a2, b1))


def _silu(x):
    return x * jax.nn.sigmoid(x)


def _iota(shape, dim):
    return lax.broadcasted_iota(jnp.int32, shape, dim)


def _ada_kernel(c_ref, w_ref, b_ref, o_ref):
    o_ref[0] = _bdot(_silu(c_ref[...]), w_ref[0]) + b_ref[0]


def _ada(c_all, ada_w, ada_b):
    depth, d, n3 = ada_w.shape
    rows = c_all.shape[0]
    tn = 1024
    return pl.pallas_call(
        _ada_kernel,
        grid=(depth, n3 // tn),
        in_specs=[pl.BlockSpec((rows, d), lambda l, j: (0, 0)),
                  pl.BlockSpec((1, d, tn), lambda l, j: (l, 0, j)),
                  pl.BlockSpec((1, 1, tn), lambda l, j: (l, 0, j))],
        out_specs=pl.BlockSpec((1, rows, tn), lambda l, j: (l, 0, j)),
        out_shape=jax.ShapeDtypeStruct((depth, rows, n3), F32),
        compiler_params=_cparams(("arbitrary", "arbitrary")),
        name="ada",
    )(c_all, ada_w, ada_b.reshape(depth, 1, n3))


def _inproj_kernel(x_ref, mod_ref, g_ref, w_ref, o_ref, h_ref):
    @pl.when(pl.program_id(1) == 0)
    def _():
        x = x_ref[...]
        xn = x * lax.rsqrt(jnp.mean(x * x, axis=-1, keepdims=True) + EPS) * g_ref[...]
        mod = mod_ref[0]
        h = xn * (1.0 + mod[..., D_MODEL:]) + mod[..., :D_MODEL]
        h_ref[...] = h.reshape(-1, D_MODEL).astype(BF16)

    o_ref[...] = jnp.dot(h_ref[...], w_ref[...], preferred_element_type=F32)


def _row_maps(x_shape, nb, rows, mod_row0):
    per = x_shape[1] // rows
    if nb == 1:
        x_map = lambda i: (i // per, i % per, 0)
        m_row = lambda i: mod_row0 + i // per
    else:
        x_map = lambda i: (i, 0, 0)
        m_row = lambda i: i
    return x_map, m_row, (x_shape[0] // nb) * per


def _in_proj(x, mod4, norm_g, w, *, li, nb, rows, mod_row0):
    d = x.shape[-1]
    tn = 1536
    tm = nb * rows
    x_map, m_row, grid_m = _row_maps(x.shape, nb, rows, mod_row0)
    return pl.pallas_call(
        _inproj_kernel,
        grid=(grid_m, NCOL // tn),
        in_specs=[pl.BlockSpec((nb, rows, d), lambda i, j: x_map(i)),
                  pl.BlockSpec((1, nb, 1, 2 * d), lambda i, j: (li, m_row(i), 0, 0)),
                  pl.BlockSpec((1, d), lambda i, j: (0, 0)),
                  pl.BlockSpec((d, tn), lambda i, j: (0, j))],
        out_specs=pl.BlockSpec((tm, tn), lambda i, j: (i, j)),
        out_shape=jax.ShapeDtypeStruct((x.shape[0] * x.shape[1], NCOL), F32),
        scratch_shapes=[pltpu.VMEM((tm, d), BF16)],
        compiler_params=_cparams(("arbitrary", "arbitrary")),
        name="in_proj",
    )(x, mod4, norm_g.reshape(1, d), w)


def _prep_w_in(w):
    o = np.concatenate([[0], np.cumsum(IN_SIZES)])
    s5_u, s5_z, ssd_z, xbc, dt, cq, ckv, kpe, mla_z, qkv, dn_z, dn_a, dn_b = [
        w[:, int(o[i]):int(o[i + 1])] for i in range(len(IN_SIZES))]
    half = MLA_ROPE // 2
    partner = jnp.concatenate([-kpe[:, half:], kpe[:, :half]], axis=1)
    zeros = lambda n: jnp.zeros((w.shape[0], n), w.dtype)
    cols = [s5_u, s5_z, ssd_z, xbc, mla_z, qkv, dn_z,
            cq, kpe, dt, dn_a, dn_b, zeros(BLK - SMALL_OFF - 16),
            ckv, partner, zeros(BLK - MLA_KV_LORA - MLA_ROPE)]
    return jnp.concatenate(cols, axis=1).astype(BF16)


def _cmul(x, tr, ti):
    return x * tr + pltpu.roll(x, S5_STATE, axis=1) * ti


def _s5_kernel(u_ref, toep_ref, w_ref, v_ref, h0_ref, tr_ref, ti_ref, y_ref, hl_ref, *, nbatch, nch, scan):
    u = u_ref[0]
    s_in = jnp.dot(u, w_ref[0], preferred_element_type=F32)
    tr = tr_ref[0]
    ti = ti_ref[0]
    if scan:
        rows = _iota((nch, 2 * S5_STATE), 0)
        h_in, h_last = [], []
        for b in range(nbatch):
            x = s_in[b * nch:(b + 1) * nch]
            for k in range(int(math.log2(nch))):
                d = 1 << k
                xs = jnp.where(rows >= d, pltpu.roll(x, d, axis=0), 0.0)
                x = x + _cmul(xs, tr[k:k + 1], ti[k:k + 1])
            h_in.append(jnp.where(rows >= 1, pltpu.roll(x, 1, axis=0), 0.0))
            h_last.append(x[nch - 1:nch])
        h_in = jnp.concatenate(h_in, axis=0)
        hl_ref[0] = jnp.concatenate(h_last, axis=0)
    else:
        h_in = h0_ref[0]
        hl_ref[0] = _cmul(h_in, tr[0:1], ti[0:1]) + s_in
    y_ref[0] = jnp.dot(u, toep_ref[0], preferred_element_type=F32) + _bdot(h_in, v_ref[0])


def _s5_tables(p, L, nsteps):
    lam_r, lam_i = p['s5_lam_re'], p['s5_lam_im']
    delta = jnp.exp(p['s5_log_dt'])[:, None]
    lr, li = lam_r * delta, lam_i * delta

    def apow(k):
        k = jnp.asarray(k, F32)
        e = jnp.exp(lr[:, None, :] * k[None, :, None])
        return e * jnp.cos(li[:, None, :] * k[None, :, None]), e * jnp.sin(li[:, None, :] * k[None, :, None])

    a_r, a_i = apow(jnp.ones((1,), F32))
    a_r, a_i = a_r[:, 0], a_i[:, 0]
    den = lam_r * lam_r + lam_i * lam_i
    q_r = ((a_r - 1.0) * lam_r + a_i * lam_i) / den
    q_i = (a_i * lam_r - (a_r - 1.0) * lam_i) / den
    b_r = q_r[..., None] * p['s5_b_re'] - q_i[..., None] * p['s5_b_im']
    b_i = q_r[..., None] * p['s5_b_im'] + q_i[..., None] * p['s5_b_re']
    c_r, c_i = p['s5_c_re'], p['s5_c_im']

    pk_r, pk_i = apow(np.arange(L + 1))
    d_r = pk_r[:, :L, :, None] * b_r[:, None] - pk_i[:, :L, :, None] * b_i[:, None]
    d_i = pk_r[:, :L, :, None] * b_i[:, None] + pk_i[:, :L, :, None] * b_r[:, None]
    kern = jnp.einsum('gcp,gtpd->gdtc', c_r, d_r) - jnp.einsum('gcp,gtpd->gdtc', c_i, d_i)
    toep = jnp.stack([jnp.pad(kern[:, :, :L - s, :], ((0, 0), (0, 0), (s, 0), (0, 0))) for s in range(L)], axis=1)
    toep = toep.reshape(S5_GROUPS, L * S5_CH, L * S5_CH)
    rev = L - 1 - np.arange(L)
    bt_r, bt_i = jnp.swapaxes(b_r, 1, 2)[:, None], jnp.swapaxes(b_i, 1, 2)[:, None]
    w_r = pk_r[:, rev][:, :, None, :] * bt_r - pk_i[:, rev][:, :, None, :] * bt_i
    w_i = pk_r[:, rev][:, :, None, :] * bt_i + pk_i[:, rev][:, :, None, :] * bt_r
    w_mat = jnp.concatenate([w_r, w_i], axis=-1).reshape(S5_GROUPS, L * S5_CH, 2 * S5_STATE)
    ct_r, ct_i = jnp.swapaxes(c_r, 1, 2)[:, :, None, :], jnp.swapaxes(c_i, 1, 2)[:, :, None, :]
    pt_r, pt_i = jnp.swapaxes(pk_r, 1, 2)[:, :, 1:, None], jnp.swapaxes(pk_i, 1, 2)[:, :, 1:, None]
    v_r = ct_r * pt_r - ct_i * pt_i
    v_i = ct_r * pt_i + ct_i * pt_r
    v_mat = jnp.concatenate([v_r, -v_i], axis=1).reshape(S5_GROUPS, 2 * S5_STATE, L * S5_CH)
    s_r, s_i = apow((2.0 ** np.arange(nsteps)) * L)
    tr = jnp.concatenate([s_r, s_r], axis=-1)
    ti = jnp.concatenate([-s_i, s_i], axis=-1)
    return toep.astype(BF16), w_mat.astype(BF16), v_mat.astype(BF16), tr, ti


def _s5_core(u_g, h0_g, tables, *, nbatch, nch, scan):
    toep, w_mat, v_mat, tr, ti = tables
    g, r, lc = u_g.shape
    nk = tr.shape[1]
    nb_out = nbatch if scan else r
    blk = lambda shape: pl.BlockSpec((1,) + shape, lambda i: (i, 0, 0))
    kern = functools.partial(_s5_kernel, nbatch=nbatch, nch=nch, scan=scan)
    return pl.pallas_call(
        kern,
        grid=(g,),
        in_specs=[blk((r, lc)), blk((lc, lc)), blk((lc, 2 * S5_STATE)), blk((2 * S5_STATE, lc)),
                  blk(h0_g.shape[1:]), blk((nk, 2 * S5_STATE)), blk((nk, 2 * S5_STATE))],
        out_specs=[blk((r, lc)), blk((nb_out, 2 * S5_STATE))],
        out_shape=[jax.ShapeDtypeStruct((g, r, lc), F32), jax.ShapeDtypeStruct((g, nb_out, 2 * S5_STATE), F32)],
        compiler_params=_cparams(("arbitrary",)),
        name="s5_core",
    )(u_g, toep, w_mat, v_mat, h0_g, tr, ti)


def _s5_post_kernel(yc_ref, u_ref, z_ref, d_ref, w_ref, b_ref, o_ref):
    y = jax.nn.gelu(yc_ref[...] + d_ref[...] * u_ref[...])
    y = y * jax.nn.sigmoid(_bdot(y, w_ref[...]) + b_ref[...])
    o_ref[...] = (y * _silu(z_ref[...])).astype(BF16)


def _s5_post(ycore, proj, p, *, tm):
    t = ycore.shape[0]
    row = lambda blk: pl.BlockSpec((tm, BLK), lambda i, blk=blk: (i, blk))
    full = lambda shape: pl.BlockSpec(shape, lambda i: (0, 0))
    return pl.pallas_call(
        _s5_post_kernel,
        grid=(t // tm,),
        in_specs=[row(0), row(B_S5U), row(B_S5Z), full((1, BLK)), full((BLK, BLK)), full((1, BLK))],
        out_specs=row(0),
        out_shape=jax.ShapeDtypeStruct((t, BLK), BF16),
        compiler_params=_cparams(("arbitrary",)),
        name="s5_post",
    )(ycore, proj, proj, p['s5_d'].reshape(1, BLK), p['s5_glu_w'].astype(BF16), p['s5_glu_b'].reshape(1, BLK))


def _causal_conv(tail, cur, w_ref, L):
    ext = jnp.concatenate([tail, cur], axis=0)
    base = 8 - (CONV_K - 1)
    out = ext[base:base + L] * w_ref[0:1, :]
    for j in range(1, CONV_K):
        out = out + ext[base + j:base + j + L] * w_ref[j:j + 1, :]
    return out


def _tri(L, strict=False):
    r = _iota((L, L), 0)
    c = _iota((L, L), 1)
    return (r > c) if strict else (r >= c)


def _pair_select(L, a, b):
    return jnp.where(_iota((L, 128), 1) < 64, a, b)


def _seq_specs(nb, L, nch, li, hd, conv_dim, blocks):
    row = lambda blk: pl.BlockSpec((nb, L, BLK), lambda i, c, blk=blk: (i, c, blk))
    specs = [row(b) for b in blocks]
    specs += [pl.BlockSpec((nb, L, 16), lambda i, c: (i, c, 0)),
              pl.BlockSpec((nb, 1, 16, L), lambda i, c: (i, c, 0, 0)),
              pl.BlockSpec((nb, 8, conv_dim), lambda i, c: (i, 0, 0)),
              pl.BlockSpec((1, nb) + hd, lambda i, c: (li, i, 0, 0, 0))]
    out_specs = [pl.BlockSpec((nb, L, BLK), lambda i, c: (i, c, 0)),
                 pl.BlockSpec((nb,) + hd, lambda i, c: (i, 0, 0, 0))]
    return specs, out_specs


def _ssd_kernel(z_ref, x_ref, bc_ref, sm_ref, smt_ref, buf_ref, h0_ref,
                cw_ref, cb_ref, dtb_ref, dtbt_ref, al_ref, alt_ref, drow_ref, ng_ref,
                y_ref, hl_ref, tail_ref, st_ref, *, nb, L, nvalid):
    c = pl.program_id(1)

    @pl.when(c == 0)
    def _():
        tail_ref[...] = buf_ref[...]
        st_ref[...] = h0_ref[0]

    tri = _tri(L)
    triu = (_iota((L, L), 0) <= _iota((L, L), 1)).astype(F32)
    rows128 = _iota((128, SSD_STATE), 0)
    gn = SSD_GROUPS * SSD_STATE
    half = BRANCH // SSD_GROUPS
    lane_lo = _iota((L, 128), 1) < 64
    xs_, dtc_, cumc_, cumr_, bm_, cm_ = {}, {}, {}, {}, {}, {}
    for j in range(nb):
        xbc = jnp.concatenate([x_ref[j], bc_ref[j]], axis=1)
        conv = _silu(_causal_conv(tail_ref[j], xbc, cw_ref, L) + cb_ref[...])
        tail_ref[j] = xbc[L - 8:L]
        xs_[j] = conv[:, :BRANCH]
        dtc = jax.nn.softplus(sm_ref[j, :, 0:SSD_HEADS] + dtb_ref[...])
        dtr = jax.nn.softplus(smt_ref[j, 0, 0:SSD_HEADS, :] + dtbt_ref[...])
        if nvalid < L:
            dtc = jnp.where(_iota(dtc.shape, 0) < nvalid, dtc, 0.0)
            dtr = jnp.where(_iota(dtr.shape, 1) < nvalid, dtr, 0.0)
        dtc_[j] = dtc
        cumc_[j] = _dot_exact_l(tri.astype(F32), dtc * (-jnp.exp(al_ref[...])))
        cumr_[j] = _dot_exact_r(dtr * (-jnp.exp(alt_ref[...])), triu)
        for g in range(SSD_GROUPS):
            bm_[j, g] = conv[:, BRANCH + g * SSD_STATE:BRANCH + (g + 1) * SSD_STATE]
            cm_[j, g] = conv[:, BRANCH + gn + g * SSD_STATE:BRANCH + gn + (g + 1) * SSD_STATE]
    groups = [(j, g) for j in range(nb) for g in range(SSD_GROUPS)]
    pairs = [(j, g, pr) for (j, g) in groups for pr in range(2)]
    heads = lambda c_: (c_[1] * 4 + c_[2] * 2, c_[1] * 4 + c_[2] * 2 + 1)
    col = lambda a, h: a[:, h:h + 1]
    cb = {c_: _nt(cm_[c_], bm_[c_]) for c_ in groups}
    st = {c_: st_ref[c_[0], heads(c_)[0]:heads(c_)[0] + 2].reshape(2 * SSD_HEADDIM, SSD_STATE) for c_ in pairs}
    y_off = {c_: _nt(cm_[c_[:2]], st[c_]) for c_ in pairs}
    xdt, upd, y_diag = {}, {}, {}
    for c_ in pairs:
        j, (h0, h1) = c_[0], heads(c_)
        xdt[c_] = xs_[j][:, h0 * SSD_HEADDIM:(h0 + 2) * SSD_HEADDIM] * jnp.where(lane_lo, col(dtc_[j], h0), col(dtc_[j], h1))
    for c_ in pairs:
        j, (h0, h1) = c_[0], heads(c_)
        edec = jnp.exp(cumc_[j][L - 1:L] - cumc_[j])
        upd[c_] = _tn(xdt[c_] * jnp.where(lane_lo, col(edec, h0), col(edec, h1)), bm_[c_[:2]])
    for c_ in pairs:
        j, (h0, h1) = c_[0], heads(c_)
        yd = [_bdot(cb[c_[:2]] * jnp.exp(jnp.where(tri, col(cumc_[j], h) - cumr_[j][h:h + 1, :], NEG)), xdt[c_])
              for h in (h0, h1)]
        y_diag[c_] = jnp.where(lane_lo, yd[0], yd[1])
    for c_ in pairs:
        j, (h0, h1) = c_[0], heads(c_)
        etot = jnp.exp(cumc_[j][L - 1:L])
        st_new = st[c_] * jnp.where(rows128 < 64, col(etot, h0), col(etot, h1)) + upd[c_]
        st_ref[j, h0:h0 + 2] = st_new.reshape(2, SSD_HEADDIM, SSD_STATE)
    for j in range(nb):
        ecum = jnp.exp(cumc_[j])
        y_parts = []
        for g in range(SSD_GROUPS):
            for pr in range(2):
                h0, h1 = heads((j, g, pr))
                y_parts.append(y_diag[j, g, pr] + y_off[j, g, pr] * jnp.where(lane_lo, col(ecum, h0), col(ecum, h1)))
        xs = xs_[j]
        y = jnp.concatenate(y_parts, axis=1) + drow_ref[...] * xs
        y = y * _silu(z_ref[j])
        outs = []
        for g in range(SSD_GROUPS):
            yg = y[:, g * half:(g + 1) * half]
            outs.append(yg * lax.rsqrt(jnp.mean(yg * yg, axis=-1, keepdims=True) + EPS))
        y_ref[j] = (jnp.concatenate(outs, axis=1) * ng_ref[...]).astype(y_ref.dtype)

    @pl.when(c == pl.num_programs(1) - 1)
    def _():
        hl_ref[...] = st_ref[...]


def _ssd(proj3, small, small_t, buf8, h0, p, *, li, nb, nch, L, nvalid):
    nbatch, r, _ = proj3.shape
    full = lambda shape: pl.BlockSpec(shape, lambda i, c: (0,) * len(shape))
    hd = (SSD_HEADS, SSD_HEADDIM, SSD_STATE)
    specs, out_specs = _seq_specs(nb, L, nch, li, hd, SSD_CONV_DIM, (B_SSDZ, B_SSDX, B_SSDBC))
    kern = functools.partial(_ssd_kernel, nb=nb, L=L, nvalid=nvalid)
    return pl.pallas_call(
        kern,
        grid=(nbatch // nb, nch),
        in_specs=specs + [full((CONV_K, SSD_CONV_DIM)), full((1, SSD_CONV_DIM)),
                          full((1, SSD_HEADS)), full((SSD_HEADS, 1)), full((1, SSD_HEADS)), full((SSD_HEADS, 1)),
                          full((1, BRANCH)), full((1, BRANCH))],
        out_specs=out_specs,
        out_shape=[jax.ShapeDtypeStruct((nbatch, r, BLK), BF16 if L % 16 == 0 else F32),
                   jax.ShapeDtypeStruct((nbatch,) + hd, F32)],
        scratch_shapes=[pltpu.VMEM((nb, 8, SSD_CONV_DIM), F32), pltpu.VMEM((nb,) + hd, F32)],
        compiler_params=_cparams(("arbitrary", "arbitrary")),
        name="ssd",
    )(proj3, proj3, proj3, small, small_t, buf8, h0,
      p['ssd_conv_w'], p['ssd_conv_b'].reshape(1, -1),
      p['ssd_dt_bias'].reshape(1, -1), p['ssd_dt_bias'].reshape(-1, 1),
      p['ssd_a_log'].reshape(1, -1), p['ssd_a_log'].reshape(-1, 1),
      jnp.repeat(p['ssd_d'], SSD_HEADDIM).reshape(1, BRANCH), p['ssd_norm_g'].reshape(1, BRANCH))


def _dn_kernel(q_ref, k_ref, v_ref, z_ref, sm_ref, smt_ref, buf_ref, s0_ref,
               cw_ref, al_ref, alt_ref, dtb_ref, dtbt_ref, ng_ref,
               y_ref, sl_ref, tail_ref, st_ref, *, nb, L, nvalid):
    c = pl.program_id(1)

    @pl.when(c == 0)
    def _():
        tail_ref[...] = buf_ref[...]
        st_ref[...] = s0_ref[0]

    a_off = SSD_HEADS
    b_off = SSD_HEADS + DN_HEADS
    tri = _tri(L)
    stri = _tri(L, strict=True)
    triu = (_iota((L, L), 0) <= _iota((L, L), 1)).astype(F32)
    eye = (_iota((L, L), 0) == _iota((L, L), 1)).astype(F32)
    kd = DN_HEADS * DN_DK
    chains = [(j, h) for j in range(nb) for h in range(DN_HEADS)]
    qs, ks, vs, gcis, bcols, gammas = {}, {}, {}, {}, {}, {}
    for j in range(nb):
        qkv = jnp.concatenate([q_ref[j], k_ref[j], v_ref[j]], axis=1)
        conv = _silu(_causal_conv(tail_ref[j], qkv, cw_ref, L))
        tail_ref[j] = qkv[L - 8:L]

        gc = -jnp.exp(al_ref[...]) * jax.nn.softplus(sm_ref[j, :, a_off:a_off + DN_HEADS] + dtb_ref[...])
        gr = -jnp.exp(alt_ref[...]) * jax.nn.softplus(smt_ref[j, 0, a_off:a_off + DN_HEADS, :] + dtbt_ref[...])
        beta = jax.nn.sigmoid(sm_ref[j, :, b_off:b_off + DN_HEADS])
        if nvalid < L:
            vc = _iota(gc.shape, 0) < nvalid
            gc = jnp.where(vc, gc, 0.0)
            beta = jnp.where(vc, beta, 0.0)
            gr = jnp.where(_iota(gr.shape, 1) < nvalid, gr, 0.0)
        gcc = _dot_exact_l(tri.astype(F32), gc)
        gcr = _dot_exact_r(gr, triu)
        for h in range(DN_HEADS):
            qh = conv[:, h * DN_DK:(h + 1) * DN_DK]
            kh = conv[:, kd + h * DN_DK:kd + (h + 1) * DN_DK]
            qs[j, h] = qh * lax.rsqrt(jnp.sum(qh * qh, axis=-1, keepdims=True) + EPS) * (DN_DK ** -0.5)
            ks[j, h] = kh * lax.rsqrt(jnp.sum(kh * kh, axis=-1, keepdims=True) + EPS)
            vs[j, h] = conv[:, 2 * kd + h * DN_DV:2 * kd + (h + 1) * DN_DV]
            gcis[j, h] = gcc[:, h:h + 1]
            bcols[j, h] = beta[:, h:h + 1]
            gammas[j, h] = jnp.exp(jnp.where(tri, gcc[:, h:h + 1] - gcr[h:h + 1, :], NEG))

    kk = {c_: _nt(ks[c_], ks[c_]) for c_ in chains}
    qk = {c_: _nt(qs[c_], ks[c_]) for c_ in chains}
    pw = {c_: jnp.where(stri, -(bcols[c_] * kk[c_] * gammas[c_]), 0.0) for c_ in chains}
    inv = {c_: eye + pw[c_] for c_ in chains}
    for _ in range(int(math.log2(L)) - 1):
        pw = {c_: _dot3(pw[c_], pw[c_]) for c_ in chains}
        inv = {c_: inv[c_] + _dot3(inv[c_], pw[c_]) for c_ in chains}
    sol = {c_: _dot3(inv[c_], jnp.concatenate([vs[c_] * bcols[c_], ks[c_] * (bcols[c_] * jnp.exp(gcis[c_]))], axis=1))
           for c_ in chains}
    st = {c_: st_ref[c_[0], c_[1]] for c_ in chains}
    v_new = {c_: sol[c_][:, :DN_DV] - _bdot(sol[c_][:, DN_DV:], st[c_]) for c_ in chains}
    o_s = {c_: _bdot(qs[c_] * jnp.exp(gcis[c_]), st[c_]) for c_ in chains}
    o_a = {c_: _bdot(jnp.where(tri, qk[c_] * gammas[c_], 0.0), v_new[c_]) for c_ in chains}
    upd = {c_: _tn(ks[c_] * jnp.exp(gcis[c_][L - 1:L] - gcis[c_]), v_new[c_]) for c_ in chains}
    for c_ in chains:
        st_ref[c_[0], c_[1]] = st[c_] * jnp.exp(gcis[c_][L - 1:L]) + upd[c_]
    for j in range(nb):
        outs = []
        for h in range(DN_HEADS):
            o = o_s[j, h] + o_a[j, h]
            o = o * lax.rsqrt(jnp.mean(o * o, axis=-1, keepdims=True) + EPS) * ng_ref[...]
            outs.append(o * _silu(z_ref[j, :, h * DN_DV:(h + 1) * DN_DV]))
        y_ref[j] = jnp.concatenate(outs, axis=1).astype(y_ref.dtype)

    @pl.when(c == pl.num_programs(1) - 1)
    def _():
        sl_ref[...] = st_ref[...]


def _dn(proj3, small, small_t, buf8, s0, p, *, li, nb, nch, L, nvalid):
    nbatch, r, _ = proj3.shape
    full = lambda shape: pl.BlockSpec(shape, lambda i, c: (0,) * len(shape))
    hd = (DN_HEADS, DN_DK, DN_DV)
    specs, out_specs = _seq_specs(nb, L, nch, li, hd, DN_CONV_DIM, (B_DNQ, B_DNK, B_DNV, B_DNZ))
    kern = functools.partial(_dn_kernel, nb=nb, L=L, nvalid=nvalid)
    return pl.pallas_call(
        kern,
        grid=(nbatch // nb, nch),
        in_specs=specs + [full((CONV_K, DN_CONV_DIM)),
                          full((1, DN_HEADS)), full((DN_HEADS, 1)), full((1, DN_HEADS)), full((DN_HEADS, 1)),
                          full((1, DN_DV))],
        out_specs=out_specs,
        out_shape=[jax.ShapeDtypeStruct((nbatch, r, BLK), BF16 if L % 16 == 0 else F32),
                   jax.ShapeDtypeStruct((nbatch,) + hd, F32)],
        scratch_shapes=[pltpu.VMEM((nb, 8, DN_CONV_DIM), F32), pltpu.VMEM((nb,) + hd, F32)],
        compiler_params=_cparams(("arbitrary", "arbitrary")),
        name="dn",
    )(proj3, proj3, proj3, proj3, small, small_t, buf8, s0,
      p['dn_conv_w'],
      p['dn_a_log'].reshape(1, -1), p['dn_a_log'].reshape(-1, 1),
      p['dn_dt_bias'].reshape(1, -1), p['dn_dt_bias'].reshape(-1, 1),
      p['dn_norm_g'].reshape(1, -1))


def _mla_pre_kernel(a_ref, b_ref, cos_ref, sin_ref, qng_ref, kvng_ref, wuq_ref, gq_ref, *rest, expand_kv):
    if expand_kv:
        wukv_ref, gk_ref, q_ref, lat_ref, kpe_ref, k_ref, v_ref = rest
    else:
        q_ref, lat_ref, kpe_ref = rest
    cos = cos_ref[...]
    sin = sin_ref[...]
    cq = a_ref[:, :MLA_Q_LORA]
    cqn = cq * lax.rsqrt(jnp.mean(cq * cq, axis=-1, keepdims=True) + EPS) * qng_ref[...]
    qp = _bdot(cqn, wuq_ref[...])
    for h in range(MLA_HEADS):
        nope = qp[:, h * 384:h * 384 + 128]
        ro = qp[:, h * 384 + 128:h * 384 + 256] * cos + qp[:, h * 384 + 256:h * 384 + 384] * sin
        ms = (jnp.sum(nope * nope, axis=-1, keepdims=True) + jnp.sum(ro * ro, axis=-1, keepdims=True)) / MLA_QK
        rinv = lax.rsqrt(ms + EPS)
        q_ref[:, h * 256:h * 256 + 128] = (nope * rinv * gq_ref[:, :128]).astype(q_ref.dtype)
        q_ref[:, h * 256 + 128:(h + 1) * 256] = (ro * rinv * gq_ref[:, 128:]).astype(q_ref.dtype)
    ckv = b_ref[:, :MLA_KV_LORA]
    lat = ckv * lax.rsqrt(jnp.mean(ckv * ckv, axis=-1, keepdims=True) + EPS) * kvng_ref[...]
    lat_ref[...] = lat
    kpe = a_ref[:, MLA_Q_LORA:] * cos + b_ref[:, MLA_KV_LORA:MLA_KV_LORA + 128] * sin
    kpe_ref[...] = kpe[:, :MLA_ROPE]
    if expand_kv:
        kv = _bdot(lat, wukv_ref[...])
        pe_sq = jnp.sum(kpe * kpe, axis=-1, keepdims=True)
        for h in range(MLA_HEADS):
            kn = kv[:, h * 256:h * 256 + 128]
            rinv = lax.rsqrt((jnp.sum(kn * kn, axis=-1, keepdims=True) + pe_sq) / MLA_QK + EPS)
            k_ref[:, h * 256:h * 256 + 128] = (kn * rinv * gk_ref[:, :128]).astype(BF16)
            k_ref[:, h * 256 + 128:(h + 1) * 256] = (kpe * rinv * gk_ref[:, 128:]).astype(BF16)
            v_ref[:, h * 256:h * 256 + 128] = kv[:, h * 256 + 128:(h + 1) * 256].astype(BF16)
            v_ref[:, h * 256 + 128:(h + 1) * 256] = jnp.ones((kn.shape[0], 128), BF16)


def _pad_gain(g, scale=1.0):
    return jnp.concatenate([g * scale, jnp.zeros((256 - MLA_QK,), F32)]).reshape(1, 256)


def _mla_pre(proj, cos, sin, p, *, tm, pos_blocks, expand_kv):
    t = proj.shape[0]
    half = MLA_ROPE // 2
    wq = p['mla_w_uq'].reshape(MLA_Q_LORA, MLA_HEADS, MLA_QK)
    nope, ro = wq[..., :MLA_NOPE], wq[..., MLA_NOPE:]
    zeros = jnp.zeros((MLA_Q_LORA, MLA_HEADS, 128 - MLA_ROPE), F32)
    partner = jnp.concatenate([-ro[..., half:], ro[..., :half]], axis=-1)
    wuq = jnp.concatenate([nope, ro, zeros, partner, zeros], axis=-1).reshape(MLA_Q_LORA, MLA_HEADS * 384).astype(BF16)
    row = lambda blk: pl.BlockSpec((tm, BLK), lambda i, blk=blk: (i, blk))
    tab = pl.BlockSpec((tm, 128), lambda i: (i % pos_blocks, 0))
    full = lambda shape: pl.BlockSpec(shape, lambda i: (0, 0))
    in_specs = [row(B_MLAQ), row(B_MLAKV), tab, tab, full((1, MLA_Q_LORA)), full((1, MLA_KV_LORA)),
                full((MLA_Q_LORA, MLA_HEADS * 384)), full((1, 256))]
    args = [proj, proj, cos, sin, p['mla_q_norm_g'].reshape(1, -1), p['mla_kv_norm_g'].reshape(1, -1),
            wuq, _pad_gain(p['mla_q_g'], MLA_QK ** -0.5)]
    out_specs = [pl.BlockSpec((tm, 1024), lambda i: (i, 0)), pl.BlockSpec((tm, MLA_KV_LORA), lambda i: (i, 0)),
                 pl.BlockSpec((tm, MLA_ROPE), lambda i: (i, 0))]
    out_shape = [jax.ShapeDtypeStruct((t, 1024), BF16 if expand_kv else F32),
                 jax.ShapeDtypeStruct((t, MLA_KV_LORA), F32), jax.ShapeDtypeStruct((t, MLA_ROPE), F32)]
    if expand_kv:
        in_specs += [full((MLA_KV_LORA, MLA_HEADS * 256)), full((1, 256))]
        args += [p['mla_w_ukv'].astype(BF16), _pad_gain(p['mla_k_g'])]
        out_specs += [pl.BlockSpec((tm, 1024), lambda i: (i, 0)), pl.BlockSpec((tm, 1024), lambda i: (i, 0))]
        out_shape += [jax.ShapeDtypeStruct((t, 1024), BF16), jax.ShapeDtypeStruct((t, 1024), BF16)]
    return pl.pallas_call(
        functools.partial(_mla_pre_kernel, expand_kv=expand_kv),
        grid=(t // tm,),
        in_specs=in_specs, out_specs=out_specs, out_shape=out_shape,
        compiler_params=_cparams(("arbitrary",)),
        name="mla_pre",
    )(*args)


def _rope_tables(pos):
    half = MLA_ROPE // 2
    inv = ROPE_THETA ** (-jnp.arange(half, dtype=F32) / half)
    ang = pos[:, None] * inv[None, :]
    z = jnp.zeros((pos.shape[0], 128 - MLA_ROPE), F32)
    cos, sin = jnp.cos(ang), jnp.sin(ang)
    return jnp.concatenate([cos, cos, z], axis=1), jnp.concatenate([sin, sin, z], axis=1)


def _flash_kernel(q_ref, k_ref, v_ref, z_ref, o_ref, m_ref, acc_ref, *, tq, tk):
    i = pl.program_id(2)
    m_ref[...] = jnp.full(m_ref.shape, NEG, F32)
    acc_ref[...] = jnp.zeros(acc_ref.shape, F32)
    q = q_ref[...]
    jd = (i * tq) // tk
    row = _iota((tq, tk), 0) + (i * tq - jd * tk)
    col = _iota((tq, tk), 1)

    def step(j, masked):
        k = k_ref[pl.ds(pl.multiple_of(j * tk, tk), tk), :]
        v = v_ref[pl.ds(pl.multiple_of(j * tk, tk), tk), :]
        s = lax.dot_general(q, k, (((1,), (1,)), ((), ())), preferred_element_type=F32)
        if masked:
            s = jnp.where(col <= row, s, NEG)
        m_prev = m_ref[...]
        m_new = jnp.maximum(m_prev, jnp.max(s, axis=-1, keepdims=True))
        pr = jnp.exp(s - m_new)
        corr = jnp.exp(m_prev - m_new)
        acc_ref[...] = acc_ref[...] * corr + jnp.dot(pr.astype(BF16), v, preferred_element_type=F32)
        m_ref[...] = m_new

    def body(j, carry):
        step(j, False)
        return carry

    lax.fori_loop(0, jd, body, 0)
    step(jd, True)
    o_ref[...] = (acc_ref[:, :MLA_V] / acc_ref[:, MLA_V:] * _silu(z_ref[...])).astype(BF16)


def _flash(q, k, v, proj, *, nbatch, seq, tq, tk):
    t = q.shape[0]
    nq = seq // tq
    kern = functools.partial(_flash_kernel, tq=tq, tk=tk)
    return pl.pallas_call(
        kern,
        grid=(nbatch, MLA_HEADS, nq),
        in_specs=[pl.BlockSpec((tq, 256), lambda b, h, i: (b * nq + i, h)),
                  pl.BlockSpec((seq, 256), lambda b, h, i: (b, h)),
                  pl.BlockSpec((seq, 2 * MLA_V), lambda b, h, i: (b, h)),
                  pl.BlockSpec((tq, MLA_V), lambda b, h, i: (b * nq + i, B_MLAZ * (BLK // MLA_V) + h))],
        out_specs=pl.BlockSpec((tq, MLA_V), lambda b, h, i: (b * nq + i, h)),
        out_shape=jax.ShapeDtypeStruct((t, BRANCH), BF16),
        scratch_shapes=[pltpu.VMEM((tq, 1), F32), pltpu.VMEM((tq, 2 * MLA_V), F32)],
        compiler_params=_cparams(("arbitrary", "arbitrary", "arbitrary")),
        name="flash",
    )(q, k, v, proj)


def _decode_kernel(pt_ref, q_ref, latn_ref, pen_ref, wukt_ref, gk_ref, lat_hbm, pet_hbm, o_ref,
                   latbuf, pebuf, sem, *, li, pps, ts, n_pages):
    b = pl.program_id(0)
    nseq = pl.num_programs(0)
    ng = n_pages // pps
    nrow = MLA_HEADS * ROWS_S
    nk = MLA_HEADS * MLA_NOPE
    per = ts // PAGE_SIZE

    def copies(seq, g, slot, k):
        ph = pt_ref[seq, g * pps + k]
        return (pltpu.make_async_copy(lat_hbm.at[li, ph], latbuf.at[slot, k], sem.at[slot]),
                pltpu.make_async_copy(pet_hbm.at[li, ph], pebuf.at[slot, k], sem.at[slot]))

    def fetch(seq, g, slot, pages=None):
        for k in (range(pps) if pages is None else pages):
            for cp in copies(seq, g, slot, k):
                cp.start()

    def wait(g, slot):
        for k in range(pps):
            for cp in copies(b, g, slot, k):
                cp.wait()

    @pl.when(b == 0)
    def _():
        fetch(0, 0, 0)

    q = q_ref[...]
    gk = gk_ref[...]
    qa, qpe = [], []
    for h in range(MLA_HEADS):
        qn = q[:, h * 256:h * 256 + 128] * gk[:, :128]
        qa.append(_bdot(qn, wukt_ref[h * 128:(h + 1) * 128, :]))
        qpe.append(q[:, h * 256 + 128:h * 256 + 128 + MLA_ROPE] * gk[:, 128:128 + MLA_ROPE])
    amats = [jnp.concatenate([wukt_ref[hp * 2 * MLA_NOPE:(hp + 1) * 2 * MLA_NOPE, :],
                              qa[2 * hp].astype(BF16), qa[2 * hp + 1].astype(BF16)], axis=0)
             for hp in range(MLA_HEADS // 2)]
    qpe = jnp.concatenate(qpe, axis=0).astype(BF16)

    amat = jnp.concatenate(amats, axis=0)
    npair = 2 * MLA_NOPE + 2 * ROWS_S

    def project(lat_b):
        n = lat_b.shape[0]
        if n < 512:
            return [lax.dot_general(a, lat_b, (((1,), (1,)), ((), ())), preferred_element_type=F32) for a in amats]
        r = jnp.concatenate([lax.dot_general(amat, lat_b[i * (n // 2):(i + 1) * (n // 2)], (((1,), (1,)), ((), ())),
                                             preferred_element_type=F32) for i in range(2)], axis=1)
        return [r[hp * npair:(hp + 1) * npair] for hp in range(MLA_HEADS // 2)]

    def scores(rs, s_pe, pe_sq):
        s = []
        for h in range(MLA_HEADS):
            r = rs[h // 2]
            o = (h % 2) * MLA_NOPE
            kn = r[o:o + MLA_NOPE]
            rinv = lax.rsqrt((jnp.sum(kn * kn, axis=0, keepdims=True) + pe_sq) / MLA_QK + EPS)
            sq = r[2 * MLA_NOPE + (h % 2) * ROWS_S:2 * MLA_NOPE + (h % 2 + 1) * ROWS_S]
            s.append((sq + s_pe[h * ROWS_S:(h + 1) * ROWS_S]) * rinv)
        return jnp.concatenate(s, axis=0)

    def partial_softmax(s, lat_b):
        m = jnp.max(s, axis=-1, keepdims=True)
        pr = jnp.exp(s - m)
        return m, jnp.sum(pr, axis=-1, keepdims=True), jnp.dot(pr.astype(BF16), lat_b, preferred_element_type=F32)

    def merge(carry, parts):
        m, l, acc = carry
        m_new = m
        for pm, _, _ in parts:
            m_new = jnp.maximum(m_new, pm)
        corr = jnp.exp(m - m_new)
        l, acc = l * corr, acc * corr
        for pm, pl_, pv in parts:
            w = jnp.exp(pm - m_new)
            l, acc = l + pl_ * w, acc + pv * w
        return m_new, l, acc

    lat_n = latn_ref[...].astype(BF16)
    pe_n = pen_ref[...]
    s_pe_n = _nt(qpe, pe_n)
    pe_sq_n = _nt(jnp.ones((8, MLA_ROPE), F32), pe_n * pe_n)[0:1]
    qrow = _iota((nrow, ROWS_S), 0) % ROWS_S
    s_n = jnp.where(_iota((nrow, ROWS_S), 1) <= qrow, scores(project(lat_n), s_pe_n, pe_sq_n), NEG)
    carry = (jnp.full((nrow, 1), NEG, F32), jnp.zeros((nrow, 1), F32), jnp.zeros((nrow, MLA_KV_LORA), F32))
    carry = merge(carry, [partial_softmax(s_n, lat_n)])

    def group(g, slot, carry, nxt):
        wait(g, slot)
        tiles = range(pps // per)
        lat_b = [latbuf[slot, k * per:(k + 1) * per].reshape(ts, MLA_KV_LORA).astype(BF16) for k in tiles]
        rs = []
        for k in tiles:
            fetch(*nxt, pages=range(k * per, (k + 1) * per))
            rs.append(project(lat_b[k]))
        pet = [jnp.concatenate([pebuf[slot, k * per + i] for i in range(per)], axis=1) for k in tiles]
        s_pe = [jnp.dot(qpe, x.astype(BF16), preferred_element_type=F32) for x in pet]
        pe_sq = [jnp.sum(x * x, axis=0, keepdims=True) for x in pet]
        ss = [scores(rs[k], s_pe[k], pe_sq[k]) for k in tiles]
        return merge(carry, [partial_softmax(ss[k], lat_b[k]) for k in tiles])

    def two_groups(gg, carry):
        carry = group(2 * gg, 0, carry, (b, 2 * gg + 1, 1))
        last = 2 * gg + 2 >= ng
        nxt = (jnp.where(last, jnp.minimum(b + 1, nseq - 1), b), jnp.where(last, 0, 2 * gg + 2), 0)
        return group(2 * gg + 1, 1, carry, nxt)

    m, l, acc = lax.fori_loop(0, ng // 2, two_groups, carry)
    o_ref[0] = acc / l

    @pl.when(b == nseq - 1)
    def _():
        wait(0, 0)


def _decode(page_table, q, lat_new, pe_new, cache_lat, cache_pet, p, *, li, pps, ts):
    nseq, n_pages = page_table.shape
    nrow = MLA_HEADS * ROWS_S
    wukv = p['mla_w_ukv'].reshape(MLA_KV_LORA, MLA_HEADS, MLA_NOPE + MLA_V)
    wukt = wukv[..., :MLA_NOPE].transpose(1, 2, 0).reshape(MLA_HEADS * MLA_NOPE, MLA_KV_LORA).astype(BF16)
    kern = functools.partial(_decode_kernel, li=li, pps=pps, ts=ts, n_pages=n_pages)
    grid_spec = pltpu.PrefetchScalarGridSpec(
        num_scalar_prefetch=1,
        grid=(nseq,),
        in_specs=[pl.BlockSpec((ROWS_S, 1024), lambda b, pt: (b, 0)),
                  pl.BlockSpec((ROWS_S, MLA_KV_LORA), lambda b, pt: (b, 0)),
                  pl.BlockSpec((ROWS_S, MLA_ROPE), lambda b, pt: (b, 0)),
                  pl.BlockSpec((MLA_HEADS * MLA_NOPE, MLA_KV_LORA), lambda b, pt: (0, 0)),
                  pl.BlockSpec((1, 256), lambda b, pt: (0, 0)),
                  pl.BlockSpec(memory_space=pl.ANY),
                  pl.BlockSpec(memory_space=pl.ANY)],
        out_specs=pl.BlockSpec((1, nrow, MLA_KV_LORA), lambda b, pt: (b, 0, 0)),
        scratch_shapes=[pltpu.VMEM((2, pps, PAGE_SIZE, MLA_KV_LORA), F32),
                        pltpu.VMEM((2, pps, MLA_ROPE, PAGE_SIZE), F32),
                        pltpu.SemaphoreType.DMA((2,))],
    )
    return pl.pallas_call(
        kern,
        grid_spec=grid_spec,
        out_shape=jax.ShapeDtypeStruct((nseq, nrow, MLA_KV_LORA), F32),
        compiler_params=_cparams(("arbitrary",)),
        name="decode",
    )(page_table, q, lat_new, pe_new, wukt, _pad_gain(p['mla_k_g']), cache_lat, cache_pet)


def _mla_post_kernel(o_ref, z_ref, wuv_ref, y_ref, *, sb):
    for h in range(MLA_HEADS):
        x = o_ref[:, h * ROWS_S:(h + 1) * ROWS_S, :].reshape(sb * ROWS_S, MLA_KV_LORA)
        y = _bdot(x, wuv_ref[h]) * _silu(z_ref[:, h * MLA_V:(h + 1) * MLA_V])
        y_ref[:, h * MLA_V:(h + 1) * MLA_V] = y.astype(BF16)


def _mla_post(o_lat, proj, p, *, sb):
    nseq = o_lat.shape[0]
    nrow = MLA_HEADS * ROWS_S
    wukv = p['mla_w_ukv'].reshape(MLA_KV_LORA, MLA_HEADS, MLA_NOPE + MLA_V)
    wuv = wukv[..., MLA_NOPE:].transpose(1, 0, 2).astype(BF16)
    return pl.pallas_call(
        functools.partial(_mla_post_kernel, sb=sb),
        grid=(nseq // sb,),
        in_specs=[pl.BlockSpec((sb, nrow, MLA_KV_LORA), lambda i: (i, 0, 0)),
                  pl.BlockSpec((sb * ROWS_S, BLK), lambda i: (i, B_MLAZ)),
                  pl.BlockSpec((MLA_HEADS, MLA_KV_LORA, MLA_V), lambda i: (0, 0, 0))],
        out_specs=pl.BlockSpec((sb * ROWS_S, BLK), lambda i: (i, 0)),
        out_shape=jax.ShapeDtypeStruct((nseq * ROWS_S, BLK), BF16),
        compiler_params=_cparams(("arbitrary",)),
        name="mla_post",
    )(o_lat, proj, wuv)


def _outproj_kernel(b0_ref, b1_ref, b2_ref, b3_ref, x_ref, gate_ref, w_ref, y_ref):
    mixed = _bdot(b0_ref[...], w_ref[0:BLK, :])
    for k, br in enumerate((b1_ref, b2_ref, b3_ref), start=1):
        mixed = mixed + _bdot(br[...], w_ref[k * BLK:(k + 1) * BLK, :])
    y_ref[...] = x_ref[...] + gate_ref[0] * mixed.reshape(x_ref.shape)


def _out_proj(branches, x, mod4, w, *, li, nb, rows, mod_row0):
    d = x.shape[-1]
    tm = nb * rows
    x_map, m_row, grid_m = _row_maps(x.shape, nb, rows, mod_row0)
    br = pl.BlockSpec((tm, BLK), lambda i: (i, 0))
    return pl.pallas_call(
        _outproj_kernel,
        grid=(grid_m,),
        in_specs=[br, br, br, br, pl.BlockSpec((nb, rows, d), x_map),
                  pl.BlockSpec((1, nb, 1, d), lambda i: (li, m_row(i), 0, 2)),
                  pl.BlockSpec((d, d), lambda i: (0, 0))],
        out_specs=pl.BlockSpec((nb, rows, d), x_map),
        out_shape=jax.ShapeDtypeStruct(x.shape, F32),
        compiler_params=_cparams(("arbitrary",)),
        name="out_proj",
    )(*[b.reshape(-1, BLK) for b in branches], x, mod4, w)


def _small_arrays(proj3, nch, L):
    nbatch = proj3.shape[0]
    small = proj3[:, :, B_MLAQ * BLK + SMALL_OFF:B_MLAQ * BLK + SMALL_OFF + 16]
    small_t = small.reshape(nbatch, nch, L, 16).transpose(0, 1, 3, 2)
    return small, small_t


def _pad_rows(a):
    return jnp.pad(a, ((0, 0), (8 - (CONV_K - 1), 0), (0, 0)))


def _layer_prompt(x, mod4, p, w_in, w_out, tabs, li, mod_row0):
    nbatch, seq, _ = x.shape
    t = nbatch * seq
    tm = min(512, seq)
    proj = _in_proj(x, mod4, p['norm_g'], w_in, li=li, nb=1, rows=tm, mod_row0=mod_row0)
    p3 = proj.reshape(nbatch, seq, NCOL)
    nch5 = seq // S5_CHUNK
    u_g = proj[:, :BLK].reshape(nbatch * nch5, S5_CHUNK, S5_GROUPS, S5_CH).transpose(2, 0, 1, 3)
    u_g = u_g.reshape(S5_GROUPS, nbatch * nch5, S5_CHUNK * S5_CH).astype(BF16)
    h0 = jnp.zeros((S5_GROUPS, 8, 2 * S5_STATE), F32)
    y5, hl5 = _s5_core(u_g, h0, tabs['s5_prompt'], nbatch=nbatch, nch=nch5, scan=True)
    y5 = y5.reshape(S5_GROUPS, nbatch * nch5, S5_CHUNK, S5_CH).transpose(1, 2, 0, 3).reshape(t, BLK)
    s5_y = _s5_post(y5, proj, p, tm=tm)
    s5_h = jnp.stack([hl5[..., :S5_STATE], hl5[..., S5_STATE:]], axis=-1).transpose(1, 0, 2, 3)
    nch = seq // SSD_CHUNK
    small, small_t = _small_arrays(p3, nch, SSD_CHUNK)
    ssd_y, ssd_h = _ssd(p3, small, small_t, jnp.zeros((nbatch, 8, SSD_CONV_DIM), F32),
                        jnp.zeros((1, nbatch, SSD_HEADS, SSD_HEADDIM, SSD_STATE), F32), p,
                        li=0, nb=nbatch, nch=nch, L=SSD_CHUNK, nvalid=SSD_CHUNK)
    ssd_buf = p3[:, seq - (CONV_K - 1):, B_SSDX * BLK:B_SSDX * BLK + SSD_CONV_DIM]
    q, lat, kpe, k, v = _mla_pre(proj, tabs['cos_p'], tabs['sin_p'], p, tm=tm, pos_blocks=seq // tm, expand_kv=True)
    mla_y = _flash(q, k, v, proj, nbatch=nbatch, seq=seq, tq=tm, tk=min(1024, seq))
    nchd = seq // DN_CHUNK
    small, small_t = _small_arrays(p3, nchd, DN_CHUNK)
    dn_y, dn_s = _dn(p3, small, small_t, jnp.zeros((nbatch, 8, DN_CONV_DIM), F32),
                     jnp.zeros((1, nbatch, DN_HEADS, DN_DK, DN_DV), F32), p,
                     li=0, nb=nbatch, nch=nchd, L=DN_CHUNK, nvalid=DN_CHUNK)
    dn_buf = p3[:, seq - (CONV_K - 1):, B_DNQ * BLK:B_DNQ * BLK + DN_CONV_DIM]
    y = _out_proj((s5_y, ssd_y, mla_y, dn_y), x, mod4, w_out, li=li, nb=1, rows=tm, mod_row0=mod_row0)
    states = (lat.reshape(nbatch, seq, -1), kpe.reshape(nbatch, seq, -1), s5_h, ssd_h, ssd_buf, dn_s, dn_buf)
    return y, states


def _layer_sample(x, mod4, p, w_in, w_out, tabs, li, st, caches, page_table, td):
    nseq = x.shape[0]
    t = nseq * ROWS_S
    sb = min(32, nseq)
    tm = sb * ROWS_S
    nb = min(8, nseq)
    s5_h0, ssd_h0, ssd_buf, dn_s0, dn_buf = st
    proj = _in_proj(x, mod4, p['norm_g'], w_in, li=li, nb=sb, rows=ROWS_S, mod_row0=0)
    p3 = proj.reshape(nseq, ROWS_S, NCOL)
    u_g = p3[:, :td, :BLK].reshape(nseq, td, S5_GROUPS, S5_CH).transpose(2, 0, 1, 3)
    u_g = u_g.reshape(S5_GROUPS, nseq, td * S5_CH).astype(BF16)
    h0 = jnp.concatenate([s5_h0[li, ..., 0], s5_h0[li, ..., 1]], axis=-1).transpose(1, 0, 2)
    y5, hl5 = _s5_core(u_g, h0, tabs['s5_sample'], nbatch=nseq, nch=1, scan=False)
    y5 = y5.reshape(S5_GROUPS, nseq, td, S5_CH).transpose(1, 2, 0, 3).reshape(nseq, td, BLK)
    y5 = jnp.pad(y5, ((0, 0), (0, ROWS_S - td), (0, 0))).reshape(t, BLK)
    s5_y = _s5_post(y5, proj, p, tm=tm)
    s5_h = jnp.stack([hl5[..., :S5_STATE], hl5[..., S5_STATE:]], axis=-1).transpose(1, 0, 2, 3)
    small, small_t = _small_arrays(p3, 1, ROWS_S)
    ssd_y, ssd_h = _ssd(p3, small, small_t, _pad_rows(ssd_buf[li]), ssd_h0, p,
                        li=li, nb=nb, nch=1, L=ROWS_S, nvalid=td)
    ssd_buf_new = p3[:, td - (CONV_K - 1):td, B_SSDX * BLK:B_SSDX * BLK + SSD_CONV_DIM]
    dn_y, dn_s = _dn(p3, small, small_t, _pad_rows(dn_buf[li]), dn_s0, p,
                     li=li, nb=nb, nch=1, L=ROWS_S, nvalid=td)
    dn_buf_new = p3[:, td - (CONV_K - 1):td, B_DNQ * BLK:B_DNQ * BLK + DN_CONV_DIM]
    q, lat, kpe = _mla_pre(proj, tabs['cos_s'], tabs['sin_s'], p, tm=tm, pos_blocks=1, expand_kv=False)
    n_pages = page_table.shape[1]
    pps = min(32, n_pages // 2)
    o_lat = _decode(page_table, q, lat, kpe, caches[0], caches[1], p, li=li, pps=pps, ts=min(1024, pps * PAGE_SIZE))
    mla_y = _mla_post(o_lat, proj, p, sb=sb)
    y = _out_proj((s5_y, ssd_y, mla_y, dn_y), x, mod4, w_out, li=li, nb=sb, rows=ROWS_S, mod_row0=0)
    lat3 = lat.reshape(nseq, ROWS_S, -1)[:, :td]
    kpe3 = kpe.reshape(nseq, ROWS_S, -1)[:, :td]
    return y, (lat3, kpe3, s5_h, ssd_h, ssd_buf_new, dn_s, dn_buf_new)


def kernel(x_prompt, x_sample, c_prompt, c_sample, cache_kv_latent, cache_k_rope, state_s5, state_ssd, state_ssd_conv, state_dn, state_dn_conv, page_table, norm_g, ada_w, ada_b, w_in, w_out, s5_lam_re, s5_lam_im, s5_log_dt, s5_b_re, s5_b_im, s5_c_re, s5_c_im, s5_d, s5_glu_w, s5_glu_b, ssd_conv_w, ssd_conv_b, ssd_dt_bias, ssd_a_log, ssd_d, ssd_norm_g, mla_q_norm_g, mla_kv_norm_g, mla_w_uq, mla_w_ukv, mla_q_g, mla_k_g, dn_conv_w, dn_a_log, dn_dt_bias, dn_norm_g):
    weights = dict(
        norm_g=norm_g, s5_lam_re=s5_lam_re, s5_lam_im=s5_lam_im, s5_log_dt=s5_log_dt,
        s5_b_re=s5_b_re, s5_b_im=s5_b_im, s5_c_re=s5_c_re, s5_c_im=s5_c_im,
        s5_d=s5_d, s5_glu_w=s5_glu_w, s5_glu_b=s5_glu_b,
        ssd_conv_w=ssd_conv_w, ssd_conv_b=ssd_conv_b, ssd_dt_bias=ssd_dt_bias,
        ssd_a_log=ssd_a_log, ssd_d=ssd_d, ssd_norm_g=ssd_norm_g,
        mla_q_norm_g=mla_q_norm_g, mla_kv_norm_g=mla_kv_norm_g, mla_w_uq=mla_w_uq,
        mla_w_ukv=mla_w_ukv, mla_q_g=mla_q_g, mla_k_g=mla_k_g,
        dn_conv_w=dn_conv_w, dn_a_log=dn_a_log, dn_dt_bias=dn_dt_bias, dn_norm_g=dn_norm_g)
    depth = w_in.shape[0]
    bp, tp, d = x_prompt.shape
    nseq, td, _ = x_sample.shape
    past_len = page_table.shape[1] * PAGE_SIZE

    c_all = jnp.concatenate([c_sample, c_prompt, jnp.zeros((8 - bp, d), F32)], axis=0)
    mod4 = _ada(c_all, ada_w, ada_b).reshape(depth, nseq + 8, 1, 3 * d)

    cos_p, sin_p = _rope_tables(jnp.arange(tp, dtype=F32))
    cos_s, sin_s = _rope_tables(jnp.arange(ROWS_S, dtype=F32) + past_len)
    reps = min(32, nseq)
    tabs_pos = dict(cos_p=cos_p, sin_p=sin_p, cos_s=jnp.tile(cos_s, (reps, 1)), sin_s=jnp.tile(sin_s, (reps, 1)))
    caches = (cache_kv_latent, jnp.swapaxes(cache_k_rope, 2, 3))
    st = (state_s5, state_ssd, state_ssd_conv, state_dn, state_dn_conv)

    y_p = x_prompt
    y_s = jnp.pad(x_sample, ((0, 0), (0, ROWS_S - td), (0, 0)))
    p_states, s_states = [], []
    nsteps = max(1, int(math.log2(tp // S5_CHUNK)))
    for li in range(depth):
        p = {name: w[li] for name, w in weights.items()}
        w_in_l = _prep_w_in(w_in[li])
        w_out_l = w_out[li].astype(BF16)
        tabs = dict(tabs_pos, s5_prompt=_s5_tables(p, S5_CHUNK, nsteps), s5_sample=_s5_tables(p, td, 1))
        y_p, st_p = _layer_prompt(y_p, mod4, p, w_in_l, w_out_l, tabs, li, nseq)
        p_states.append(st_p)
        y_s, st_s = _layer_sample(y_s, mod4, p, w_in_l, w_out_l, tabs, li, st, caches, page_table, td)
        s_states.append(st_s)

    stack = lambda states, i: jnp.stack([s[i] for s in states], axis=0)
    return ((y_p, y_s[:, :td])
            + tuple(stack(p_states, i) for i in range(7))
            + tuple(stack(s_states, i) for i in range(7)))
```

```python
import functools
import math

import numpy as np
import jax
import jax.numpy as jnp
from jax import lax
from jax.experimental import pallas as pl
from jax.experimental.pallas import tpu as pltpu

F32 = jnp.float32
BF16 = jnp.bfloat16
EPS = 1e-6

D_MODEL = 2048
BRANCH = 512
CONV_K = 4
S5_CH = 16
S5_GROUPS = 32
S5_STATE = 64
S5_CHUNK = 16
SSD_HEADDIM = 64
SSD_HEADS = 8
SSD_GROUPS = 2
SSD_STATE = 128
SSD_CHUNK = 128
SSD_CONV_DIM = BRANCH + 2 * SSD_GROUPS * SSD_STATE
MLA_NOPE = 128
MLA_ROPE = 64
MLA_QK = MLA_NOPE + MLA_ROPE
MLA_V = 128
MLA_HEADS = 4
MLA_Q_LORA = 384
MLA_KV_LORA = 256
ROPE_THETA = 10000.0
DN_DK = 128
DN_DV = 128
DN_HEADS = 4
DN_CHUNK = 64
DN_CONV_DIM = 2 * DN_HEADS * DN_DK + DN_HEADS * DN_DV
PAGE_SIZE = 128

IN_SIZES = (BRANCH, BRANCH, BRANCH, SSD_CONV_DIM, SSD_HEADS, MLA_Q_LORA, MLA_KV_LORA, MLA_ROPE, BRANCH,
            DN_CONV_DIM, BRANCH, DN_HEADS, DN_HEADS)

ROWS_S = 8
BLK = 512
NBLK = 12
NCOL = NBLK * BLK
B_S5U, B_S5Z, B_SSDZ, B_SSDX, B_SSDBC, B_MLAZ, B_DNQ, B_DNK, B_DNV, B_DNZ, B_MLAQ, B_MLAKV = range(NBLK)
SMALL_OFF = MLA_Q_LORA + MLA_ROPE
NEG = -1e30
VMEM_LIMIT = 52 * 1024 * 1024


def _cparams(sem, vmem=VMEM_LIMIT):
    return pltpu.CompilerParams(dimension_semantics=sem, vmem_limit_bytes=vmem)


def _bdot(a, b):
    return jnp.dot(a.astype(BF16), b.astype(BF16), preferred_element_type=F32)


def _nt(a, b):
    return lax.dot_general(a.astype(BF16), b.astype(BF16), (((1,), (1,)), ((), ())), preferred_element_type=F32)


def _tn(a, b):
    return lax.dot_general(a.astype(BF16), b.astype(BF16), (((0,), (0,)), ((), ())), preferred_element_type=F32)


def _split(x, n):
    parts = []
    r = x
    for _ in range(n):
        p = r.astype(BF16)
        parts.append(p)
        r = r - p.astype(F32)
    return parts


def _dot_exact_l(t01, x):
    t = t01.astype(BF16)
    return sum(jnp.dot(t, p, preferred_element_type=F32) for p in _split(x, 3))


def _dot_exact_r(x, t01):
    t = t01.astype(BF16)
    return sum(jnp.dot(p, t, preferred_element_type=F32) for p in _split(x, 3))


def _dot3(a, b):
    a1, a2 = _split(a, 2)
    b1, b2 = _split(b, 2)
    d = functools.partial(jnp.dot, preferred_element_type=F32)
    return d(a1, b1) + (d(a1, b2) + d(a2, b1))


def _silu(x):
    return x * jax.nn.sigmoid(x)


def _iota(shape, dim):
    return lax.broadcasted_iota(jnp.int32, shape, dim)


def _ada_kernel(c_ref, w_ref, b_ref, o_ref):
    o_ref[0] = _bdot(_silu(c_ref[...]), w_ref[0]) + b_ref[0]


def _ada(c_all, ada_w, ada_b):
    depth, d, n3 = ada_w.shape
    rows = c_all.shape[0]
    tn = 1024
    return pl.pallas_call(
        _ada_kernel,
        grid=(depth, n3 // tn),
        in_specs=[pl.BlockSpec((rows, d), lambda l, j: (0, 0)),
                  pl.BlockSpec((1, d, tn), lambda l, j: (l, 0, j)),
                  pl.BlockSpec((1, 1, tn), lambda l, j: (l, 0, j))],
        out_specs=pl.BlockSpec((1, rows, tn), lambda l, j: (l, 0, j)),
        out_shape=jax.ShapeDtypeStruct((depth, rows, n3), F32),
        compiler_params=_cparams(("arbitrary", "arbitrary")),
        name="ada",
    )(c_all, ada_w, ada_b.reshape(depth, 1, n3))


def _inproj_kernel(x_ref, mod_ref, g_ref, w_ref, o_ref, h_ref):
    @pl.when(pl.program_id(1) == 0)
    def _():
        x = x_ref[...]
        xn = x * lax.rsqrt(jnp.mean(x * x, axis=-1, keepdims=True) + EPS) * g_ref[...]
        mod = mod_ref[0]
        h = xn * (1.0 + mod[..., D_MODEL:]) + mod[..., :D_MODEL]
        h_ref[...] = h.reshape(-1, D_MODEL).astype(BF16)

    o_ref[...] = jnp.dot(h_ref[...], w_ref[...], preferred_element_type=F32)


def _row_maps(x_shape, nb, rows, mod_row0):
    per = x_shape[1] // rows
    if nb == 1:
        x_map = lambda i: (i // per, i % per, 0)
        m_row = lambda i: mod_row0 + i // per
    else:
        x_map = lambda i: (i, 0, 0)
        m_row = lambda i: i
    return x_map, m_row, (x_shape[0] // nb) * per


def _in_proj(x, mod4, norm_g, w, *, li, nb, rows, mod_row0):
    d = x.shape[-1]
    tm = nb * rows
    tn = 1536 if tm <= 512 else 768
    x_map, m_row, grid_m = _row_maps(x.shape, nb, rows, mod_row0)
    return pl.pallas_call(
        _inproj_kernel,
        grid=(grid_m, NCOL // tn),
        in_specs=[pl.BlockSpec((nb, rows, d), lambda i, j: x_map(i)),
                  pl.BlockSpec((1, nb, 1, 2 * d), lambda i, j: (li, m_row(i), 0, 0)),
                  pl.BlockSpec((1, d), lambda i, j: (0, 0)),
                  pl.BlockSpec((d, tn), lambda i, j: (0, j))],
        out_specs=pl.BlockSpec((tm, tn), lambda i, j: (i, j)),
        out_shape=jax.ShapeDtypeStruct((x.shape[0] * x.shape[1], NCOL), F32),
        scratch_shapes=[pltpu.VMEM((tm, d), BF16)],
        compiler_params=_cparams(("arbitrary", "arbitrary")),
        name="in_proj",
    )(x, mod4, norm_g.reshape(1, d), w)


def _prep_w_in(w):
    o = np.concatenate([[0], np.cumsum(IN_SIZES)])
    s5_u, s5_z, ssd_z, xbc, dt, cq, ckv, kpe, mla_z, qkv, dn_z, dn_a, dn_b = [
        w[:, int(o[i]):int(o[i + 1])] for i in range(len(IN_SIZES))]
    half = MLA_ROPE // 2
    partner = jnp.concatenate([-kpe[:, half:], kpe[:, :half]], axis=1)
    zeros = lambda n: jnp.zeros((w.shape[0], n), w.dtype)
    cols = [s5_u, s5_z, ssd_z, xbc, mla_z, qkv, dn_z,
            cq, kpe, dt, dn_a, dn_b, zeros(BLK - SMALL_OFF - 16),
            ckv, partner, zeros(BLK - MLA_KV_LORA - MLA_ROPE)]
    return jnp.concatenate(cols, axis=1).astype(BF16)


def _cmul(x, tr, ti):
    return x * tr + pltpu.roll(x, S5_STATE, axis=1) * ti


def _s5_kernel(u_ref, toep_ref, w_ref, v_ref, h0_ref, tr_ref, ti_ref, y_ref, hl_ref, *, nbatch, nch, scan):
    u = u_ref[0]
    s_in = jnp.dot(u, w_ref[0], preferred_element_type=F32)
    tr = tr_ref[0]
    ti = ti_ref[0]
    if scan:
        rows = _iota((nch, 2 * S5_STATE), 0)
        h_in, h_last = [], []
        for b in range(nbatch):
            x = s_in[b * nch:(b + 1) * nch]
            for k in range(int(math.log2(nch))):
                d = 1 << k
                xs = jnp.where(rows >= d, pltpu.roll(x, d, axis=0), 0.0)
                x = x + _cmul(xs, tr[k:k + 1], ti[k:k + 1])
            h_in.append(jnp.where(rows >= 1, pltpu.roll(x, 1, axis=0), 0.0))
            h_last.append(x[nch - 1:nch])
        h_in = jnp.concatenate(h_in, axis=0)
        hl_ref[0] = jnp.concatenate(h_last, axis=0)
    else:
        h_in = h0_ref[0]
        hl_ref[0] = _cmul(h_in, tr[0:1], ti[0:1]) + s_in
    y_ref[0] = jnp.dot(u, toep_ref[0], preferred_element_type=F32) + _bdot(h_in, v_ref[0])


def _s5_tables(p, L, nsteps):
    lam_r, lam_i = p['s5_lam_re'], p['s5_lam_im']
    delta = jnp.exp(p['s5_log_dt'])[:, None]
    lr, li = lam_r * delta, lam_i * delta

    def apow(k):
        k = jnp.asarray(k, F32)
        e = jnp.exp(lr[:, None, :] * k[None, :, None])
        return e * jnp.cos(li[:, None, :] * k[None, :, None]), e * jnp.sin(li[:, None, :] * k[None, :, None])

    a_r, a_i = apow(jnp.ones((1,), F32))
    a_r, a_i = a_r[:, 0], a_i[:, 0]
    den = lam_r * lam_r + lam_i * lam_i
    q_r = ((a_r - 1.0) * lam_r + a_i * lam_i) / den
    q_i = (a_i * lam_r - (a_r - 1.0) * lam_i) / den
    b_r = q_r[..., None] * p['s5_b_re'] - q_i[..., None] * p['s5_b_im']
    b_i = q_r[..., None] * p['s5_b_im'] + q_i[..., None] * p['s5_b_re']
    c_r, c_i = p['s5_c_re'], p['s5_c_im']

    pk_r, pk_i = apow(np.arange(L + 1))
    d_r = pk_r[:, :L, :, None] * b_r[:, None] - pk_i[:, :L, :, None] * b_i[:, None]
    d_i = pk_r[:, :L, :, None] * b_i[:, None] + pk_i[:, :L, :, None] * b_r[:, None]
    kern = jnp.einsum('gcp,gtpd->gdtc', c_r, d_r) - jnp.einsum('gcp,gtpd->gdtc', c_i, d_i)
    toep = jnp.stack([jnp.pad(kern[:, :, :L - s, :], ((0, 0), (0, 0), (s, 0), (0, 0))) for s in range(L)], axis=1)
    toep = toep.reshape(S5_GROUPS, L * S5_CH, L * S5_CH)
    rev = L - 1 - np.arange(L)
    bt_r, bt_i = jnp.swapaxes(b_r, 1, 2)[:, None], jnp.swapaxes(b_i, 1, 2)[:, None]
    w_r = pk_r[:, rev][:, :, None, :] * bt_r - pk_i[:, rev][:, :, None, :] * bt_i
    w_i = pk_r[:, rev][:, :, None, :] * bt_i + pk_i[:, rev][:, :, None, :] * bt_r
    w_mat = jnp.concatenate([w_r, w_i], axis=-1).reshape(S5_GROUPS, L * S5_CH, 2 * S5_STATE)
    ct_r, ct_i = jnp.swapaxes(c_r, 1, 2)[:, :, None, :], jnp.swapaxes(c_i, 1, 2)[:, :, None, :]
    pt_r, pt_i = jnp.swapaxes(pk_r, 1, 2)[:, :, 1:, None], jnp.swapaxes(pk_i, 1, 2)[:, :, 1:, None]
    v_r = ct_r * pt_r - ct_i * pt_i
    v_i = ct_r * pt_i + ct_i * pt_r
    v_mat = jnp.concatenate([v_r, -v_i], axis=1).reshape(S5_GROUPS, 2 * S5_STATE, L * S5_CH)
    s_r, s_i = apow((2.0 ** np.arange(nsteps)) * L)
    tr = jnp.concatenate([s_r, s_r], axis=-1)
    ti = jnp.concatenate([-s_i, s_i], axis=-1)
    return toep.astype(BF16), w_mat.astype(BF16), v_mat.astype(BF16), tr, ti


def _s5_core(u_g, h0_g, tables, *, nbatch, nch, scan):
    toep, w_mat, v_mat, tr, ti = tables
    g, r, lc = u_g.shape
    nk = tr.shape[1]
    nb_out = nbatch if scan else r
    blk = lambda shape: pl.BlockSpec((1,) + shape, lambda i: (i, 0, 0))
    kern = functools.partial(_s5_kernel, nbatch=nbatch, nch=nch, scan=scan)
    return pl.pallas_call(
        kern,
        grid=(g,),
        in_specs=[blk((r, lc)), blk((lc, lc)), blk((lc, 2 * S5_STATE)), blk((2 * S5_STATE, lc)),
                  blk(h0_g.shape[1:]), blk((nk, 2 * S5_STATE)), blk((nk, 2 * S5_STATE))],
        out_specs=[blk((r, lc)), blk((nb_out, 2 * S5_STATE))],
        out_shape=[jax.ShapeDtypeStruct((g, r, lc), F32), jax.ShapeDtypeStruct((g, nb_out, 2 * S5_STATE), F32)],
        compiler_params=_cparams(("arbitrary",)),
        name="s5_core",
    )(u_g, toep, w_mat, v_mat, h0_g, tr, ti)


def _s5_post_kernel(yc_ref, u_ref, z_ref, d_ref, w_ref, b_ref, o_ref):
    y = jax.nn.gelu(yc_ref[...] + d_ref[...] * u_ref[...])
    y = y * jax.nn.sigmoid(_bdot(y, w_ref[...]) + b_ref[...])
    o_ref[...] = (y * _silu(z_ref[...])).astype(BF16)


def _s5_post(ycore, proj, p, *, tm):
    t = ycore.shape[0]
    row = lambda blk: pl.BlockSpec((tm, BLK), lambda i, blk=blk: (i, blk))
    full = lambda shape: pl.BlockSpec(shape, lambda i: (0, 0))
    return pl.pallas_call(
        _s5_post_kernel,
        grid=(t // tm,),
        in_specs=[row(0), row(B_S5U), row(B_S5Z), full((1, BLK)), full((BLK, BLK)), full((1, BLK))],
        out_specs=row(0),
        out_shape=jax.ShapeDtypeStruct((t, BLK), BF16),
        compiler_params=_cparams(("arbitrary",)),
        name="s5_post",
    )(ycore, proj, proj, p['s5_d'].reshape(1, BLK), p['s5_glu_w'].astype(BF16), p['s5_glu_b'].reshape(1, BLK))


def _causal_conv(tail, cur, w_ref, L):
    ext = jnp.concatenate([tail, cur], axis=0)
    base = 8 - (CONV_K - 1)
    out = ext[base:base + L] * w_ref[0:1, :]
    for j in range(1, CONV_K):
        out = out + ext[base + j:base + j + L] * w_ref[j:j + 1, :]
    return out


def _tri(L, strict=False):
    r = _iota((L, L), 0)
    c = _iota((L, L), 1)
    return (r > c) if strict else (r >= c)


def _pair_select(L, a, b):
    return jnp.where(_iota((L, 128), 1) < 64, a, b)


def _seq_specs(nb, L, nch, li, hd, conv_dim, blocks):
    row = lambda blk: pl.BlockSpec((nb, L, BLK), lambda i, c, blk=blk: (i, c, blk))
    specs = [row(b) for b in blocks]
    specs += [pl.BlockSpec((nb, L, 16), lambda i, c: (i, c, 0)),
              pl.BlockSpec((nb, 1, 16, L), lambda i, c: (i, c, 0, 0)),
              pl.BlockSpec((nb, 8, conv_dim), lambda i, c: (i, 0, 0)),
              pl.BlockSpec((1, nb) + hd, lambda i, c: (li, i, 0, 0, 0))]
    out_specs = [pl.BlockSpec((nb, L, BLK), lambda i, c: (i, c, 0)),
                 pl.BlockSpec((nb,) + hd, lambda i, c: (i, 0, 0, 0))]
    return specs, out_specs


def _ssd_kernel(z_ref, x_ref, bc_ref, sm_ref, smt_ref, buf_ref, h0_ref,
                cw_ref, cb_ref, dtb_ref, dtbt_ref, al_ref, alt_ref, drow_ref, ng_ref,
                y_ref, hl_ref, tail_ref, st_ref, *, nb, L, nvalid):
    c = pl.program_id(1)

    @pl.when(c == 0)
    def _():
        tail_ref[...] = buf_ref[...]
        st_ref[...] = h0_ref[0]

    tri = _tri(L)
    triu = (_iota((L, L), 0) <= _iota((L, L), 1)).astype(F32)
    rows128 = _iota((128, SSD_STATE), 0)
    gn = SSD_GROUPS * SSD_STATE
    half = BRANCH // SSD_GROUPS
    lane_lo = _iota((L, 128), 1) < 64
    xs_, dtc_, cumc_, cumr_, bm_, cm_ = {}, {}, {}, {}, {}, {}
    for j in range(nb):
        xbc = jnp.concatenate([x_ref[j], bc_ref[j]], axis=1)
        conv = _silu(_causal_conv(tail_ref[j], xbc, cw_ref, L) + cb_ref[...])
        tail_ref[j] = xbc[L - 8:L]
        xs_[j] = conv[:, :BRANCH]
        dtc = jax.nn.softplus(sm_ref[j, :, 0:SSD_HEADS] + dtb_ref[...])
        dtr = jax.nn.softplus(smt_ref[j, 0, 0:SSD_HEADS, :] + dtbt_ref[...])
        if nvalid < L:
            dtc = jnp.where(_iota(dtc.shape, 0) < nvalid, dtc, 0.0)
            dtr = jnp.where(_iota(dtr.shape, 1) < nvalid, dtr, 0.0)
        dtc_[j] = dtc
        cumc_[j] = _dot_exact_l(tri.astype(F32), dtc * (-jnp.exp(al_ref[...])))
        cumr_[j] = _dot_exact_r(dtr * (-jnp.exp(alt_ref[...])), triu)
        for g in range(SSD_GROUPS):
            bm_[j, g] = conv[:, BRANCH + g * SSD_STATE:BRANCH + (g + 1) * SSD_STATE]
            cm_[j, g] = conv[:, BRANCH + gn + g * SSD_STATE:BRANCH + gn + (g + 1) * SSD_STATE]
    groups = [(j, g) for j in range(nb) for g in range(SSD_GROUPS)]
    pairs = [(j, g, pr) for (j, g) in groups for pr in range(2)]
    heads = lambda c_: (c_[1] * 4 + c_[2] * 2, c_[1] * 4 + c_[2] * 2 + 1)
    col = lambda a, h: a[:, h:h + 1]
    cb = {c_: _nt(cm_[c_], bm_[c_]) for c_ in groups}
    st = {c_: st_ref[c_[0], heads(c_)[0]:heads(c_)[0] + 2].reshape(2 * SSD_HEADDIM, SSD_STATE) for c_ in pairs}
    y_off = {c_: _nt(cm_[c_[:2]], st[c_]) for c_ in pairs}
    xdt, upd, y_diag = {}, {}, {}
    for c_ in pairs:
        j, (h0, h1) = c_[0], heads(c_)
        xdt[c_] = xs_[j][:, h0 * SSD_HEADDIM:(h0 + 2) * SSD_HEADDIM] * jnp.where(lane_lo, col(dtc_[j], h0), col(dtc_[j], h1))
    for c_ in pairs:
        j, (h0, h1) = c_[0], heads(c_)
        edec = jnp.exp(cumc_[j][L - 1:L] - cumc_[j])
        upd[c_] = _tn(xdt[c_] * jnp.where(lane_lo, col(edec, h0), col(edec, h1)), bm_[c_[:2]])
    for c_ in pairs:
        j, (h0, h1) = c_[0], heads(c_)
        yd = [_bdot(cb[c_[:2]] * jnp.exp(jnp.where(tri, col(cumc_[j], h) - cumr_[j][h:h + 1, :], NEG)), xdt[c_])
              for h in (h0, h1)]
        y_diag[c_] = jnp.where(lane_lo, yd[0], yd[1])
    for c_ in pairs:
        j, (h0, h1) = c_[0], heads(c_)
        etot = jnp.exp(cumc_[j][L - 1:L])
        st_new = st[c_] * jnp.where(rows128 < 64, col(etot, h0), col(etot, h1)) + upd[c_]
        st_ref[j, h0:h0 + 2] = st_new.reshape(2, SSD_HEADDIM, SSD_STATE)
    for j in range(nb):
        ecum = jnp.exp(cumc_[j])
        y_parts = []
        for g in range(SSD_GROUPS):
            for pr in range(2):
                h0, h1 = heads((j, g, pr))
                y_parts.append(y_diag[j, g, pr] + y_off[j, g, pr] * jnp.where(lane_lo, col(ecum, h0), col(ecum, h1)))
        xs = xs_[j]
        y = jnp.concatenate(y_parts, axis=1) + drow_ref[...] * xs
        y = y * _silu(z_ref[j])
        outs = []
        for g in range(SSD_GROUPS):
            yg = y[:, g * half:(g + 1) * half]
            outs.append(yg * lax.rsqrt(jnp.mean(yg * yg, axis=-1, keepdims=True) + EPS))
        y_ref[j] = (jnp.concatenate(outs, axis=1) * ng_ref[...]).astype(y_ref.dtype)

    @pl.when(c == pl.num_programs(1) - 1)
    def _():
        hl_ref[...] = st_ref[...]


def _ssd(proj3, small, small_t, buf8, h0, p, *, li, nb, nch, L, nvalid):
    nbatch, r, _ = proj3.shape
    full = lambda shape: pl.BlockSpec(shape, lambda i, c: (0,) * len(shape))
    hd = (SSD_HEADS, SSD_HEADDIM, SSD_STATE)
    specs, out_specs = _seq_specs(nb, L, nch, li, hd, SSD_CONV_DIM, (B_SSDZ, B_SSDX, B_SSDBC))
    kern = functools.partial(_ssd_kernel, nb=nb, L=L, nvalid=nvalid)
    return pl.pallas_call(
        kern,
        grid=(nbatch // nb, nch),
        in_specs=specs + [full((CONV_K, SSD_CONV_DIM)), full((1, SSD_CONV_DIM)),
                          full((1, SSD_HEADS)), full((SSD_HEADS, 1)), full((1, SSD_HEADS)), full((SSD_HEADS, 1)),
                          full((1, BRANCH)), full((1, BRANCH))],
        out_specs=out_specs,
        out_shape=[jax.ShapeDtypeStruct((nbatch, r, BLK), BF16 if L % 16 == 0 else F32),
                   jax.ShapeDtypeStruct((nbatch,) + hd, F32)],
        scratch_shapes=[pltpu.VMEM((nb, 8, SSD_CONV_DIM), F32), pltpu.VMEM((nb,) + hd, F32)],
        compiler_params=_cparams(("arbitrary", "arbitrary")),
        name="ssd",
    )(proj3, proj3, proj3, small, small_t, buf8, h0,
      p['ssd_conv_w'], p['ssd_conv_b'].reshape(1, -1),
      p['ssd_dt_bias'].reshape(1, -1), p['ssd_dt_bias'].reshape(-1, 1),
      p['ssd_a_log'].reshape(1, -1), p['ssd_a_log'].reshape(-1, 1),
      jnp.repeat(p['ssd_d'], SSD_HEADDIM).reshape(1, BRANCH), p['ssd_norm_g'].reshape(1, BRANCH))


def _dn_kernel(q_ref, k_ref, v_ref, z_ref, sm_ref, smt_ref, buf_ref, s0_ref,
               cw_ref, al_ref, alt_ref, dtb_ref, dtbt_ref, ng_ref,
               y_ref, sl_ref, tail_ref, st_ref, *, nb, L, nvalid):
    c = pl.program_id(1)

    @pl.when(c == 0)
    def _():
        tail_ref[...] = buf_ref[...]
        st_ref[...] = s0_ref[0]

    a_off = SSD_HEADS
    b_off = SSD_HEADS + DN_HEADS
    tri = _tri(L)
    stri = _tri(L, strict=True)
    triu = (_iota((L, L), 0) <= _iota((L, L), 1)).astype(F32)
    eye = (_iota((L, L), 0) == _iota((L, L), 1)).astype(F32)
    kd = DN_HEADS * DN_DK
    chains = [(j, h) for j in range(nb) for h in range(DN_HEADS)]
    qs, ks, vs, gcis, bcols, gammas = {}, {}, {}, {}, {}, {}
    for j in range(nb):
        qkv = jnp.concatenate([q_ref[j], k_ref[j], v_ref[j]], axis=1)
        conv = _silu(_causal_conv(tail_ref[j], qkv, cw_ref, L))
        tail_ref[j] = qkv[L - 8:L]

        gc = -jnp.exp(al_ref[...]) * jax.nn.softplus(sm_ref[j, :, a_off:a_off + DN_HEADS] + dtb_ref[...])
        gr = -jnp.exp(alt_ref[...]) * jax.nn.softplus(smt_ref[j, 0, a_off:a_off + DN_HEADS, :] + dtbt_ref[...])
        beta = jax.nn.sigmoid(sm_ref[j, :, b_off:b_off + DN_HEADS])
        if nvalid < L:
            vc = _iota(gc.shape, 0) < nvalid
            gc = jnp.where(vc, gc, 0.0)
            beta = jnp.where(vc, beta, 0.0)
            gr = jnp.where(_iota(gr.shape, 1) < nvalid, gr, 0.0)
        gcc = _dot_exact_l(tri.astype(F32), gc)
        gcr = _dot_exact_r(gr, triu)
        for h in range(DN_HEADS):
            qh = conv[:, h * DN_DK:(h + 1) * DN_DK]
            kh = conv[:, kd + h * DN_DK:kd + (h + 1) * DN_DK]
            qs[j, h] = qh * lax.rsqrt(jnp.sum(qh * qh, axis=-1, keepdims=True) + EPS) * (DN_DK ** -0.5)
            ks[j, h] = kh * lax.rsqrt(jnp.sum(kh * kh, axis=-1, keepdims=True) + EPS)
            vs[j, h] = conv[:, 2 * kd + h * DN_DV:2 * kd + (h + 1) * DN_DV]
            gcis[j, h] = gcc[:, h:h + 1]
            bcols[j, h] = beta[:, h:h + 1]
            gammas[j, h] = jnp.exp(jnp.where(tri, gcc[:, h:h + 1] - gcr[h:h + 1, :], NEG))

    kk = {c_: _nt(ks[c_], ks[c_]) for c_ in chains}
    qk = {c_: _nt(qs[c_], ks[c_]) for c_ in chains}
    pw = {c_: jnp.where(stri, -(bcols[c_] * kk[c_] * gammas[c_]), 0.0) for c_ in chains}
    inv = {c_: eye + pw[c_] for c_ in chains}
    for _ in range(int(math.log2(L)) - 1):
        pw = {c_: _dot3(pw[c_], pw[c_]) for c_ in chains}
        inv = {c_: inv[c_] + _dot3(inv[c_], pw[c_]) for c_ in chains}
    sol = {c_: _dot3(inv[c_], jnp.concatenate([vs[c_] * bcols[c_], ks[c_] * (bcols[c_] * jnp.exp(gcis[c_]))], axis=1))
           for c_ in chains}
    st = {c_: st_ref[c_[0], c_[1]] for c_ in chains}
    v_new = {c_: sol[c_][:, :DN_DV] - _bdot(sol[c_][:, DN_DV:], st[c_]) for c_ in chains}
    o_s = {c_: _bdot(qs[c_] * jnp.exp(gcis[c_]), st[c_]) for c_ in chains}
    o_a = {c_: _bdot(jnp.where(tri, qk[c_] * gammas[c_], 0.0), v_new[c_]) for c_ in chains}
    upd = {c_: _tn(ks[c_] * jnp.exp(gcis[c_][L - 1:L] - gcis[c_]), v_new[c_]) for c_ in chains}
    for c_ in chains:
        st_ref[c_[0], c_[1]] = st[c_] * jnp.exp(gcis[c_][L - 1:L]) + upd[c_]
    for j in range(nb):
        outs = []
        for h in range(DN_HEADS):
            o = o_s[j, h] + o_a[j, h]
            o = o * lax.rsqrt(jnp.mean(o * o, axis=-1, keepdims=True) + EPS) * ng_ref[...]
            outs.append(o * _silu(z_ref[j, :, h * DN_DV:(h + 1) * DN_DV]))
        y_ref[j] = jnp.concatenate(outs, axis=1).astype(y_ref.dtype)

    @pl.when(c == pl.num_programs(1) - 1)
    def _():
        sl_ref[...] = st_ref[...]


def _dn(proj3, small, small_t, buf8, s0, p, *, li, nb, nch, L, nvalid):
    nbatch, r, _ = proj3.shape
    full = lambda shape: pl.BlockSpec(shape, lambda i, c: (0,) * len(shape))
    hd = (DN_HEADS, DN_DK, DN_DV)
    specs, out_specs = _seq_specs(nb, L, nch, li, hd, DN_CONV_DIM, (B_DNQ, B_DNK, B_DNV, B_DNZ))
    kern = functools.partial(_dn_kernel, nb=nb, L=L, nvalid=nvalid)
    return pl.pallas_call(
        kern,
        grid=(nbatch // nb, nch),
        in_specs=specs + [full((CONV_K, DN_CONV_DIM)),
                          full((1, DN_HEADS)), full((DN_HEADS, 1)), full((1, DN_HEADS)), full((DN_HEADS, 1)),
                          full((1, DN_DV))],
        out_specs=out_specs,
        out_shape=[jax.ShapeDtypeStruct((nbatch, r, BLK), BF16 if L % 16 == 0 else F32),
                   jax.ShapeDtypeStruct((nbatch,) + hd, F32)],
        scratch_shapes=[pltpu.VMEM((nb, 8, DN_CONV_DIM), F32), pltpu.VMEM((nb,) + hd, F32)],
        compiler_params=_cparams(("arbitrary", "arbitrary")),
        name="dn",
    )(proj3, proj3, proj3, proj3, small, small_t, buf8, s0,
      p['dn_conv_w'],
      p['dn_a_log'].reshape(1, -1), p['dn_a_log'].reshape(-1, 1),
      p['dn_dt_bias'].reshape(1, -1), p['dn_dt_bias'].reshape(-1, 1),
      p['dn_norm_g'].reshape(1, -1))


def _mla_pre_kernel(a_ref, b_ref, cos_ref, sin_ref, qng_ref, kvng_ref, wuq_ref, gq_ref, *rest, expand_kv):
    if expand_kv:
        wukv_ref, gk_ref, q_ref, lat_ref, kpe_ref, k_ref, v_ref = rest
    else:
        q_ref, lat_ref, kpe_ref = rest
    cos = cos_ref[...]
    sin = sin_ref[...]
    cq = a_ref[:, :MLA_Q_LORA]
    cqn = cq * lax.rsqrt(jnp.mean(cq * cq, axis=-1, keepdims=True) + EPS) * qng_ref[...]
    qp = _bdot(cqn, wuq_ref[...])
    for h in range(MLA_HEADS):
        nope = qp[:, h * 384:h * 384 + 128]
        ro = qp[:, h * 384 + 128:h * 384 + 256] * cos + qp[:, h * 384 + 256:h * 384 + 384] * sin
        ms = (jnp.sum(nope * nope, axis=-1, keepdims=True) + jnp.sum(ro * ro, axis=-1, keepdims=True)) / MLA_QK
        rinv = lax.rsqrt(ms + EPS)
        q_ref[:, h * 256:h * 256 + 128] = (nope * rinv * gq_ref[:, :128]).astype(q_ref.dtype)
        q_ref[:, h * 256 + 128:(h + 1) * 256] = (ro * rinv * gq_ref[:, 128:]).astype(q_ref.dtype)
    ckv = b_ref[:, :MLA_KV_LORA]
    lat = ckv * lax.rsqrt(jnp.mean(ckv * ckv, axis=-1, keepdims=True) + EPS) * kvng_ref[...]
    lat_ref[...] = lat
    kpe = a_ref[:, MLA_Q_LORA:] * cos + b_ref[:, MLA_KV_LORA:MLA_KV_LORA + 128] * sin
    kpe_ref[...] = kpe[:, :MLA_ROPE]
    if expand_kv:
        kv = _bdot(lat, wukv_ref[...])
        pe_sq = jnp.sum(kpe * kpe, axis=-1, keepdims=True)
        for h in range(MLA_HEADS):
            kn = kv[:, h * 256:h * 256 + 128]
            rinv = lax.rsqrt((jnp.sum(kn * kn, axis=-1, keepdims=True) + pe_sq) / MLA_QK + EPS)
            k_ref[:, h * 256:h * 256 + 128] = (kn * rinv * gk_ref[:, :128]).astype(BF16)
            k_ref[:, h * 256 + 128:(h + 1) * 256] = (kpe * rinv * gk_ref[:, 128:]).astype(BF16)
            v_ref[:, h * 256:h * 256 + 128] = kv[:, h * 256 + 128:(h + 1) * 256].astype(BF16)
            v_ref[:, h * 256 + 128:(h + 1) * 256] = jnp.ones((kn.shape[0], 128), BF16)


def _pad_gain(g, scale=1.0):
    return jnp.concatenate([g * scale, jnp.zeros((256 - MLA_QK,), F32)]).reshape(1, 256)


def _mla_pre(proj, cos, sin, p, *, tm, pos_blocks, expand_kv):
    t = proj.shape[0]
    half = MLA_ROPE // 2
    wq = p['mla_w_uq'].reshape(MLA_Q_LORA, MLA_HEADS, MLA_QK)
    nope, ro = wq[..., :MLA_NOPE], wq[..., MLA_NOPE:]
    zeros = jnp.zeros((MLA_Q_LORA, MLA_HEADS, 128 - MLA_ROPE), F32)
    partner = jnp.concatenate([-ro[..., half:], ro[..., :half]], axis=-1)
    wuq = jnp.concatenate([nope, ro, zeros, partner, zeros], axis=-1).reshape(MLA_Q_LORA, MLA_HEADS * 384).astype(BF16)
    row = lambda blk: pl.BlockSpec((tm, BLK), lambda i, blk=blk: (i, blk))
    tab = pl.BlockSpec((tm, 128), lambda i: (i % pos_blocks, 0))
    full = lambda shape: pl.BlockSpec(shape, lambda i: (0, 0))
    in_specs = [row(B_MLAQ), row(B_MLAKV), tab, tab, full((1, MLA_Q_LORA)), full((1, MLA_KV_LORA)),
                full((MLA_Q_LORA, MLA_HEADS * 384)), full((1, 256))]
    args = [proj, proj, cos, sin, p['mla_q_norm_g'].reshape(1, -1), p['mla_kv_norm_g'].reshape(1, -1),
            wuq, _pad_gain(p['mla_q_g'], MLA_QK ** -0.5)]
    out_specs = [pl.BlockSpec((tm, 1024), lambda i: (i, 0)), pl.BlockSpec((tm, MLA_KV_LORA), lambda i: (i, 0)),
                 pl.BlockSpec((tm, MLA_ROPE), lambda i: (i, 0))]
    out_shape = [jax.ShapeDtypeStruct((t, 1024), BF16 if expand_kv else F32),
                 jax.ShapeDtypeStruct((t, MLA_KV_LORA), F32), jax.ShapeDtypeStruct((t, MLA_ROPE), F32)]
    if expand_kv:
        in_specs += [full((MLA_KV_LORA, MLA_HEADS * 256)), full((1, 256))]
        args += [p['mla_w_ukv'].astype(BF16), _pad_gain(p['mla_k_g'])]
        out_specs += [pl.BlockSpec((tm, 1024), lambda i: (i, 0)), pl.BlockSpec((tm, 1024), lambda i: (i, 0))]
        out_shape += [jax.ShapeDtypeStruct((t, 1024), BF16), jax.ShapeDtypeStruct((t, 1024), BF16)]
    return pl.pallas_call(
        functools.partial(_mla_pre_kernel, expand_kv=expand_kv),
        grid=(t // tm,),
        in_specs=in_specs, out_specs=out_specs, out_shape=out_shape,
        compiler_params=_cparams(("arbitrary",)),
        name="mla_pre",
    )(*args)


def _rope_tables(pos):
    half = MLA_ROPE // 2
    inv = ROPE_THETA ** (-jnp.arange(half, dtype=F32) / half)
    ang = pos[:, None] * inv[None, :]
    z = jnp.zeros((pos.shape[0], 128 - MLA_ROPE), F32)
    cos, sin = jnp.cos(ang), jnp.sin(ang)
    return jnp.concatenate([cos, cos, z], axis=1), jnp.concatenate([sin, sin, z], axis=1)


def _flash_kernel(q_ref, k_ref, v_ref, z_ref, o_ref, m_ref, acc_ref, *, tq, tk):
    i = pl.program_id(2)
    m_ref[...] = jnp.full(m_ref.shape, NEG, F32)
    acc_ref[...] = jnp.zeros(acc_ref.shape, F32)
    q = q_ref[...]
    jd = (i * tq) // tk
    row = _iota((tq, tk), 0) + (i * tq - jd * tk)
    col = _iota((tq, tk), 1)

    def step(j, masked):
        k = k_ref[pl.ds(pl.multiple_of(j * tk, tk), tk), :]
        v = v_ref[pl.ds(pl.multiple_of(j * tk, tk), tk), :]
        s = lax.dot_general(q, k, (((1,), (1,)), ((), ())), preferred_element_type=F32)
        if masked:
            s = jnp.where(col <= row, s, NEG)
        m_prev = m_ref[...]
        m_new = jnp.maximum(m_prev, jnp.max(s, axis=-1, keepdims=True))
        pr = jnp.exp(s - m_new)
        corr = jnp.exp(m_prev - m_new)
        acc_ref[...] = acc_ref[...] * corr + jnp.dot(pr.astype(BF16), v, preferred_element_type=F32)
        m_ref[...] = m_new

    def body(j, carry):
        step(j, False)
        return carry

    lax.fori_loop(0, jd, body, 0)
    step(jd, True)
    o_ref[...] = (acc_ref[:, :MLA_V] / acc_ref[:, MLA_V:] * _silu(z_ref[...])).astype(BF16)


def _flash(q, k, v, proj, *, nbatch, seq, tq, tk):
    t = q.shape[0]
    nq = seq // tq
    kern = functools.partial(_flash_kernel, tq=tq, tk=tk)
    return pl.pallas_call(
        kern,
        grid=(nbatch, MLA_HEADS, nq),
        in_specs=[pl.BlockSpec((tq, 256), lambda b, h, i: (b * nq + i, h)),
                  pl.BlockSpec((seq, 256), lambda b, h, i: (b, h)),
                  pl.BlockSpec((seq, 2 * MLA_V), lambda b, h, i: (b, h)),
                  pl.BlockSpec((tq, MLA_V), lambda b, h, i: (b * nq + i, B_MLAZ * (BLK // MLA_V) + h))],
        out_specs=pl.BlockSpec((tq, MLA_V), lambda b, h, i: (b * nq + i, h)),
        out_shape=jax.ShapeDtypeStruct((t, BRANCH), BF16),
        scratch_shapes=[pltpu.VMEM((tq, 1), F32), pltpu.VMEM((tq, 2 * MLA_V), F32)],
        compiler_params=_cparams(("arbitrary", "arbitrary", "arbitrary")),
        name="flash",
    )(q, k, v, proj)


def _decode_kernel(pt_ref, q_ref, latn_ref, pen_ref, wukt_ref, gk_ref, lat_hbm, pet_hbm, o_ref,
                   latbuf, pebuf, sem, *, li, pps, ts, n_pages):
    b = pl.program_id(0)
    nseq = pl.num_programs(0)
    ng = n_pages // pps
    nrow = MLA_HEADS * ROWS_S
    nk = MLA_HEADS * MLA_NOPE
    per = ts // PAGE_SIZE

    def copies(seq, g, slot, k):
        ph = pt_ref[seq, g * pps + k]
        return (pltpu.make_async_copy(lat_hbm.at[li, ph], latbuf.at[slot, k], sem.at[slot]),
                pltpu.make_async_copy(pet_hbm.at[li, ph], pebuf.at[slot, k], sem.at[slot]))

    def fetch(seq, g, slot, pages=None):
        for k in (range(pps) if pages is None else pages):
            for cp in copies(seq, g, slot, k):
                cp.start(priority=k % 2)

    def wait(g, slot):
        for k in range(pps):
            for cp in copies(b, g, slot, k):
                cp.wait()

    @pl.when(b == 0)
    def _():
        fetch(0, 0, 0)

    q = q_ref[...]
    gk = gk_ref[...]
    qa, qpe = [], []
    for h in range(MLA_HEADS):
        qn = q[:, h * 256:h * 256 + 128] * gk[:, :128]
        qa.append(_bdot(qn, wukt_ref[h * 128:(h + 1) * 128, :]))
        qpe.append(q[:, h * 256 + 128:h * 256 + 128 + MLA_ROPE] * gk[:, 128:128 + MLA_ROPE])
    amats = [jnp.concatenate([wukt_ref[hp * 2 * MLA_NOPE:(hp + 1) * 2 * MLA_NOPE, :],
                              qa[2 * hp].astype(BF16), qa[2 * hp + 1].astype(BF16)], axis=0)
             for hp in range(MLA_HEADS // 2)]
    qpe = jnp.concatenate(qpe, axis=0).astype(BF16)

    amat = jnp.concatenate(amats, axis=0)
    npair = 2 * MLA_NOPE + 2 * ROWS_S

    def project(lat_b):
        n = lat_b.shape[0]
        if n < 512:
            return [lax.dot_general(a, lat_b, (((1,), (1,)), ((), ())), preferred_element_type=F32) for a in amats]
        r = jnp.concatenate([lax.dot_general(amat, lat_b[i * (n // 2):(i + 1) * (n // 2)], (((1,), (1,)), ((), ())),
                                             preferred_element_type=F32) for i in range(2)], axis=1)
        return [r[hp * npair:(hp + 1) * npair] for hp in range(MLA_HEADS // 2)]

    def scores(rs, s_pe, pe_sq):
        s = []
        for h in range(MLA_HEADS):
            r = rs[h // 2]
            o = (h % 2) * MLA_NOPE
            kn = r[o:o + MLA_NOPE]
            rinv = lax.rsqrt((jnp.sum(kn * kn, axis=0, keepdims=True) + pe_sq) / MLA_QK + EPS)
            sq = r[2 * MLA_NOPE + (h % 2) * ROWS_S:2 * MLA_NOPE + (h % 2 + 1) * ROWS_S]
            s.append((sq + s_pe[h * ROWS_S:(h + 1) * ROWS_S]) * rinv)
        return jnp.concatenate(s, axis=0)

    def partial_softmax(s, lat_b):
        m = jnp.max(s, axis=-1, keepdims=True)
        pr = jnp.exp(s - m)
        return m, jnp.sum(pr, axis=-1, keepdims=True), jnp.dot(pr.astype(BF16), lat_b, preferred_element_type=F32)

    def merge(carry, parts):
        m, l, acc = carry
        m_new = m
        for pm, _, _ in parts:
            m_new = jnp.maximum(m_new, pm)
        corr = jnp.exp(m - m_new)
        l, acc = l * corr, acc * corr
        for pm, pl_, pv in parts:
            w = jnp.exp(pm - m_new)
            l, acc = l + pl_ * w, acc + pv * w
        return m_new, l, acc

    lat_n = latn_ref[...].astype(BF16)
    pe_n = pen_ref[...]
    s_pe_n = _nt(qpe, pe_n)
    pe_sq_n = _nt(jnp.ones((8, MLA_ROPE), F32), pe_n * pe_n)[0:1]
    qrow = _iota((nrow, ROWS_S), 0) % ROWS_S
    s_n = jnp.where(_iota((nrow, ROWS_S), 1) <= qrow, scores(project(lat_n), s_pe_n, pe_sq_n), NEG)
    carry = (jnp.full((nrow, 1), NEG, F32), jnp.zeros((nrow, 1), F32), jnp.zeros((nrow, MLA_KV_LORA), F32))
    carry = merge(carry, [partial_softmax(s_n, lat_n)])

    def group(g, slot, carry, nxt):
        wait(g, slot)
        tiles = range(pps // per)
        lat_b = [latbuf[slot, k * per:(k + 1) * per].reshape(ts, MLA_KV_LORA).astype(BF16) for k in tiles]
        rs = []
        for k in tiles:
            fetch(*nxt, pages=range(k * per, (k + 1) * per))
            rs.append(project(lat_b[k]))
        pet = [jnp.concatenate([pebuf[slot, k * per + i] for i in range(per)], axis=1) for k in tiles]
        s_pe = [jnp.dot(qpe, x.astype(BF16), preferred_element_type=F32) for x in pet]
        pe_sq = [jnp.sum(x * x, axis=0, keepdims=True) for x in pet]
        ss = [scores(rs[k], s_pe[k], pe_sq[k]) for k in tiles]
        return merge(carry, [partial_softmax(ss[k], lat_b[k]) for k in tiles])

    def two_groups(gg, carry):
        carry = group(2 * gg, 0, carry, (b, 2 * gg + 1, 1))
        last = 2 * gg + 2 >= ng
        nxt = (jnp.where(last, jnp.minimum(b + 1, nseq - 1), b), jnp.where(last, 0, 2 * gg + 2), 0)
        return group(2 * gg + 1, 1, carry, nxt)

    m, l, acc = lax.fori_loop(0, ng // 2, two_groups, carry)
    o_ref[0] = acc / l

    @pl.when(b == nseq - 1)
    def _():
        wait(0, 0)


def _decode(page_table, q, lat_new, pe_new, cache_lat, cache_pet, p, *, li, pps, ts):
    nseq, n_pages = page_table.shape
    nrow = MLA_HEADS * ROWS_S
    wukv = p['mla_w_ukv'].reshape(MLA_KV_LORA, MLA_HEADS, MLA_NOPE + MLA_V)
    wukt = wukv[..., :MLA_NOPE].transpose(1, 2, 0).reshape(MLA_HEADS * MLA_NOPE, MLA_KV_LORA).astype(BF16)
    kern = functools.partial(_decode_kernel, li=li, pps=pps, ts=ts, n_pages=n_pages)
    grid_spec = pltpu.PrefetchScalarGridSpec(
        num_scalar_prefetch=1,
        grid=(nseq,),
        in_specs=[pl.BlockSpec((ROWS_S, 1024), lambda b, pt: (b, 0)),
                  pl.BlockSpec((ROWS_S, MLA_KV_LORA), lambda b, pt: (b, 0)),
                  pl.BlockSpec((ROWS_S, MLA_ROPE), lambda b, pt: (b, 0)),
                  pl.BlockSpec((MLA_HEADS * MLA_NOPE, MLA_KV_LORA), lambda b, pt: (0, 0)),
                  pl.BlockSpec((1, 256), lambda b, pt: (0, 0)),
                  pl.BlockSpec(memory_space=pl.ANY),
                  pl.BlockSpec(memory_space=pl.ANY)],
        out_specs=pl.BlockSpec((1, nrow, MLA_KV_LORA), lambda b, pt: (b, 0, 0)),
        scratch_shapes=[pltpu.VMEM((2, pps, PAGE_SIZE, MLA_KV_LORA), F32),
                        pltpu.VMEM((2, pps, MLA_ROPE, PAGE_SIZE), F32),
                        pltpu.SemaphoreType.DMA((2,))],
    )
    return pl.pallas_call(
        kern,
        grid_spec=grid_spec,
        out_shape=jax.ShapeDtypeStruct((nseq, nrow, MLA_KV_LORA), F32),
        compiler_params=_cparams(("arbitrary",)),
        name="decode",
    )(page_table, q, lat_new, pe_new, wukt, _pad_gain(p['mla_k_g']), cache_lat, cache_pet)


def _mla_post_kernel(o_ref, z_ref, wuv_ref, y_ref, *, sb):
    for h in range(MLA_HEADS):
        x = o_ref[:, h * ROWS_S:(h + 1) * ROWS_S, :].reshape(sb * ROWS_S, MLA_KV_LORA)
        y = _bdot(x, wuv_ref[h]) * _silu(z_ref[:, h * MLA_V:(h + 1) * MLA_V])
        y_ref[:, h * MLA_V:(h + 1) * MLA_V] = y.astype(BF16)


def _mla_post(o_lat, proj, p, *, sb):
    nseq = o_lat.shape[0]
    nrow = MLA_HEADS * ROWS_S
    wukv = p['mla_w_ukv'].reshape(MLA_KV_LORA, MLA_HEADS, MLA_NOPE + MLA_V)
    wuv = wukv[..., MLA_NOPE:].transpose(1, 0, 2).astype(BF16)
    return pl.pallas_call(
        functools.partial(_mla_post_kernel, sb=sb),
        grid=(nseq // sb,),
        in_specs=[pl.BlockSpec((sb, nrow, MLA_KV_LORA), lambda i: (i, 0, 0)),
                  pl.BlockSpec((sb * ROWS_S, BLK), lambda i: (i, B_MLAZ)),
                  pl.BlockSpec((MLA_HEADS, MLA_KV_LORA, MLA_V), lambda i: (0, 0, 0))],
        out_specs=pl.BlockSpec((sb * ROWS_S, BLK), lambda i: (i, 0)),
        out_shape=jax.ShapeDtypeStruct((nseq * ROWS_S, BLK), BF16),
        compiler_params=_cparams(("arbitrary",)),
        name="mla_post",
    )(o_lat, proj, wuv)


def _outproj_kernel(b0_ref, b1_ref, b2_ref, b3_ref, x_ref, gate_ref, w_ref, y_ref):
    mixed = _bdot(b0_ref[...], w_ref[0:BLK, :])
    for k, br in enumerate((b1_ref, b2_ref, b3_ref), start=1):
        mixed = mixed + _bdot(br[...], w_ref[k * BLK:(k + 1) * BLK, :])
    y_ref[...] = x_ref[...] + gate_ref[0] * mixed.reshape(x_ref.shape)


def _out_proj(branches, x, mod4, w, *, li, nb, rows, mod_row0):
    d = x.shape[-1]
    tm = nb * rows
    x_map, m_row, grid_m = _row_maps(x.shape, nb, rows, mod_row0)
    br = pl.BlockSpec((tm, BLK), lambda i: (i, 0))
    return pl.pallas_call(
        _outproj_kernel,
        grid=(grid_m,),
        in_specs=[br, br, br, br, pl.BlockSpec((nb, rows, d), x_map),
                  pl.BlockSpec((1, nb, 1, d), lambda i: (li, m_row(i), 0, 2)),
                  pl.BlockSpec((d, d), lambda i: (0, 0))],
        out_specs=pl.BlockSpec((nb, rows, d), x_map),
        out_shape=jax.ShapeDtypeStruct(x.shape, F32),
        compiler_params=_cparams(("arbitrary",)),
        name="out_proj",
    )(*[b.reshape(-1, BLK) for b in branches], x, mod4, w)


def _small_arrays(proj3, nch, L):
    nbatch = proj3.shape[0]
    small = proj3[:, :, B_MLAQ * BLK + SMALL_OFF:B_MLAQ * BLK + SMALL_OFF + 16]
    small_t = small.reshape(nbatch, nch, L, 16).transpose(0, 1, 3, 2)
    return small, small_t


def _pad_rows(a):
    return jnp.pad(a, ((0, 0), (8 - (CONV_K - 1), 0), (0, 0)))


def _layer_prompt(x, mod4, p, w_in, w_out, tabs, li, mod_row0):
    nbatch, seq, _ = x.shape
    t = nbatch * seq
    tm = min(512, seq)
    proj = _in_proj(x, mod4, p['norm_g'], w_in, li=li, nb=1, rows=min(1024, seq), mod_row0=mod_row0)
    p3 = proj.reshape(nbatch, seq, NCOL)
    nch5 = seq // S5_CHUNK
    u_g = proj[:, :BLK].reshape(nbatch * nch5, S5_CHUNK, S5_GROUPS, S5_CH).transpose(2, 0, 1, 3)
    u_g = u_g.reshape(S5_GROUPS, nbatch * nch5, S5_CHUNK * S5_CH).astype(BF16)
    h0 = jnp.zeros((S5_GROUPS, 8, 2 * S5_STATE), F32)
    y5, hl5 = _s5_core(u_g, h0, tabs['s5_prompt'], nbatch=nbatch, nch=nch5, scan=True)
    y5 = y5.reshape(S5_GROUPS, nbatch * nch5, S5_CHUNK, S5_CH).transpose(1, 2, 0, 3).reshape(t, BLK)
    s5_y = _s5_post(y5, proj, p, tm=tm)
    s5_h = jnp.stack([hl5[..., :S5_STATE], hl5[..., S5_STATE:]], axis=-1).transpose(1, 0, 2, 3)
    nch = seq // SSD_CHUNK
    small, small_t = _small_arrays(p3, nch, SSD_CHUNK)
    ssd_y, ssd_h = _ssd(p3, small, small_t, jnp.zeros((nbatch, 8, SSD_CONV_DIM), F32),
                        jnp.zeros((1, nbatch, SSD_HEADS, SSD_HEADDIM, SSD_STATE), F32), p,
                        li=0, nb=nbatch, nch=nch, L=SSD_CHUNK, nvalid=SSD_CHUNK)
    ssd_buf = p3[:, seq - (CONV_K - 1):, B_SSDX * BLK:B_SSDX * BLK + SSD_CONV_DIM]
    q, lat, kpe, k, v = _mla_pre(proj, tabs['cos_p'], tabs['sin_p'], p, tm=tm, pos_blocks=seq // tm, expand_kv=True)
    mla_y = _flash(q, k, v, proj, nbatch=nbatch, seq=seq, tq=tm, tk=min(1024, seq))
    nchd = seq // DN_CHUNK
    small, small_t = _small_arrays(p3, nchd, DN_CHUNK)
    dn_y, dn_s = _dn(p3, small, small_t, jnp.zeros((nbatch, 8, DN_CONV_DIM), F32),
                     jnp.zeros((1, nbatch, DN_HEADS, DN_DK, DN_DV), F32), p,
                     li=0, nb=nbatch, nch=nchd, L=DN_CHUNK, nvalid=DN_CHUNK)
    dn_buf = p3[:, seq - (CONV_K - 1):, B_DNQ * BLK:B_DNQ * BLK + DN_CONV_DIM]
    y = _out_proj((s5_y, ssd_y, mla_y, dn_y), x, mod4, w_out, li=li, nb=1, rows=tm, mod_row0=mod_row0)
    states = (lat.reshape(nbatch, seq, -1), kpe.reshape(nbatch, seq, -1), s5_h, ssd_h, ssd_buf, dn_s, dn_buf)
    return y, states


def _layer_sample(x, mod4, p, w_in, w_out, tabs, li, st, caches, page_table, td):
    nseq = x.shape[0]
    t = nseq * ROWS_S
    sb = min(32, nseq)
    tm = sb * ROWS_S
    nb = min(8, nseq)
    s5_h0, ssd_h0, ssd_buf, dn_s0, dn_buf = st
    proj = _in_proj(x, mod4, p['norm_g'], w_in, li=li, nb=sb, rows=ROWS_S, mod_row0=0)
    p3 = proj.reshape(nseq, ROWS_S, NCOL)
    u_g = p3[:, :td, :BLK].reshape(nseq, td, S5_GROUPS, S5_CH).transpose(2, 0, 1, 3)
    u_g = u_g.reshape(S5_GROUPS, nseq, td * S5_CH).astype(BF16)
    h0 = jnp.concatenate([s5_h0[li, ..., 0], s5_h0[li, ..., 1]], axis=-1).transpose(1, 0, 2)
    y5, hl5 = _s5_core(u_g, h0, tabs['s5_sample'], nbatch=nseq, nch=1, scan=False)
    y5 = y5.reshape(S5_GROUPS, nseq, td, S5_CH).transpose(1, 2, 0, 3).reshape(nseq, td, BLK)
    y5 = jnp.pad(y5, ((0, 0), (0, ROWS_S - td), (0, 0))).reshape(t, BLK)
    s5_y = _s5_post(y5, proj, p, tm=tm)
    s5_h = jnp.stack([hl5[..., :S5_STATE], hl5[..., S5_STATE:]], axis=-1).transpose(1, 0, 2, 3)
    small, small_t = _small_arrays(p3, 1, ROWS_S)
    ssd_y, ssd_h = _ssd(p3, small, small_t, _pad_rows(ssd_buf[li]), ssd_h0, p,
                        li=li, nb=nb, nch=1, L=ROWS_S, nvalid=td)
    ssd_buf_new = p3[:, td - (CONV_K - 1):td, B_SSDX * BLK:B_SSDX * BLK + SSD_CONV_DIM]
    dn_y, dn_s = _dn(p3, small, small_t, _pad_rows(dn_buf[li]), dn_s0, p,
                     li=li, nb=nb, nch=1, L=ROWS_S, nvalid=td)
    dn_buf_new = p3[:, td - (CONV_K - 1):td, B_DNQ * BLK:B_DNQ * BLK + DN_CONV_DIM]
    q, lat, kpe = _mla_pre(proj, tabs['cos_s'], tabs['sin_s'], p, tm=tm, pos_blocks=1, expand_kv=False)
    n_pages = page_table.shape[1]
    pps = min(32, n_pages // 2)
    o_lat = _decode(page_table, q, lat, kpe, caches[0], caches[1], p, li=li, pps=pps, ts=min(1024, pps * PAGE_SIZE))
    mla_y = _mla_post(o_lat, proj, p, sb=sb)
    y = _out_proj((s5_y, ssd_y, mla_y, dn_y), x, mod4, w_out, li=li, nb=sb, rows=ROWS_S, mod_row0=0)
    lat3 = lat.reshape(nseq, ROWS_S, -1)[:, :td]
    kpe3 = kpe.reshape(nseq, ROWS_S, -1)[:, :td]
    return y, (lat3, kpe3, s5_h, ssd_h, ssd_buf_new, dn_s, dn_buf_new)


def kernel(x_prompt, x_sample, c_prompt, c_sample, cache_kv_latent, cache_k_rope, state_s5, state_ssd, state_ssd_conv, state_dn, state_dn_conv, page_table, norm_g, ada_w, ada_b, w_in, w_out, s5_lam_re, s5_lam_im, s5_log_dt, s5_b_re, s5_b_im, s5_c_re, s5_c_im, s5_d, s5_glu_w, s5_glu_b, ssd_conv_w, ssd_conv_b, ssd_dt_bias, ssd_a_log, ssd_d, ssd_norm_g, mla_q_norm_g, mla_kv_norm_g, mla_w_uq, mla_w_ukv, mla_q_g, mla_k_g, dn_conv_w, dn_a_log, dn_dt_bias, dn_norm_g):
    weights = dict(
        norm_g=norm_g, s5_lam_re=s5_lam_re, s5_lam_im=s5_lam_im, s5_log_dt=s5_log_dt,
        s5_b_re=s5_b_re, s5_b_im=s5_b_im, s5_c_re=s5_c_re, s5_c_im=s5_c_im,
        s5_d=s5_d, s5_glu_w=s5_glu_w, s5_glu_b=s5_glu_b,
        ssd_conv_w=ssd_conv_w, ssd_conv_b=ssd_conv_b, ssd_dt_bias=ssd_dt_bias,
        ssd_a_log=ssd_a_log, ssd_d=ssd_d, ssd_norm_g=ssd_norm_g,
        mla_q_norm_g=mla_q_norm_g, mla_kv_norm_g=mla_kv_norm_g, mla_w_uq=mla_w_uq,
        mla_w_ukv=mla_w_ukv, mla_q_g=mla_q_g, mla_k_g=mla_k_g,
        dn_conv_w=dn_conv_w, dn_a_log=dn_a_log, dn_dt_bias=dn_dt_bias, dn_norm_g=dn_norm_g)
    depth = w_in.shape[0]
    bp, tp, d = x_prompt.shape
    nseq, td, _ = x_sample.shape
    past_len = page_table.shape[1] * PAGE_SIZE

    c_all = jnp.concatenate([c_sample, c_prompt, jnp.zeros((8 - bp, d), F32)], axis=0)
    mod4 = _ada(c_all, ada_w, ada_b).reshape(depth, nseq + 8, 1, 3 * d)

    cos_p, sin_p = _rope_tables(jnp.arange(tp, dtype=F32))
    cos_s, sin_s = _rope_tables(jnp.arange(ROWS_S, dtype=F32) + past_len)
    reps = min(32, nseq)
    tabs_pos = dict(cos_p=cos_p, sin_p=sin_p, cos_s=jnp.tile(cos_s, (reps, 1)), sin_s=jnp.tile(sin_s, (reps, 1)))
    caches = (cache_kv_latent, jnp.swapaxes(cache_k_rope, 2, 3))
    st = (state_s5, state_ssd, state_ssd_conv, state_dn, state_dn_conv)

    y_p = x_prompt
    y_s = jnp.pad(x_sample, ((0, 0), (0, ROWS_S - td), (0, 0)))
    p_states, s_states = [], []
    nsteps = max(1, int(math.log2(tp // S5_CHUNK)))
    for li in range(depth):
        p = {name: w[li] for name, w in weights.items()}
        w_in_l = _prep_w_in(w_in[li])
        w_out_l = w_out[li].astype(BF16)
        tabs = dict(tabs_pos, s5_prompt=_s5_tables(p, S5_CHUNK, nsteps), s5_sample=_s5_tables(p, td, 1))
        y_p, st_p = _layer_prompt(y_p, mod4, p, w_in_l, w_out_l, tabs, li, nseq)
        p_states.append(st_p)
        y_s, st_s = _layer_sample(y_s, mod4, p, w_in_l, w_out_l, tabs, li, st, caches, page_table, td)
        s_states.append(st_s)

    stack = lambda states, i: jnp.stack([s[i] for s in states], axis=0)
    return ((y_p, y_s[:, :td])
            + tuple(stack(p_states, i) for i in range(7))
            + tuple(stack(s_states, i) for i in range(7)))
```

```python
import functools
import math

import numpy as np
import jax
import jax.numpy as jnp
from jax import lax
from jax.experimental import pallas as pl
from jax.experimental.pallas import tpu as pltpu

F32 = jnp.float32
BF16 = jnp.bfloat16
EPS = 1e-6

D_MODEL = 2048
BRANCH = 512
CONV_K = 4
S5_CH = 16
S5_GROUPS = 32
S5_STATE = 64
S5_CHUNK = 8
SSD_HEADDIM = 64
SSD_HEADS = 8
SSD_GROUPS = 2
SSD_STATE = 128
SSD_CHUNK = 128
SSD_CONV_DIM = BRANCH + 2 * SSD_GROUPS * SSD_STATE
MLA_NOPE = 128
MLA_ROPE = 64
MLA_QK = MLA_NOPE + MLA_ROPE
MLA_V = 128
MLA_HEADS = 4
MLA_Q_LORA = 384
MLA_KV_LORA = 256
ROPE_THETA = 10000.0
DN_DK = 128
DN_DV = 128
DN_HEADS = 4
DN_CHUNK = 64
DN_CONV_DIM = 2 * DN_HEADS * DN_DK + DN_HEADS * DN_DV
PAGE_SIZE = 128

IN_SIZES = (BRANCH, BRANCH, BRANCH, SSD_CONV_DIM, SSD_HEADS, MLA_Q_LORA, MLA_KV_LORA, MLA_ROPE, BRANCH,
            DN_CONV_DIM, BRANCH, DN_HEADS, DN_HEADS)

ROWS_S = 8
BLK = 512
NBLK = 12
NCOL = NBLK * BLK
B_S5U, B_S5Z, B_SSDZ, B_SSDX, B_SSDBC, B_MLAZ, B_DNQ, B_DNK, B_DNV, B_DNZ, B_MLAQ, B_MLAKV = range(NBLK)
SMALL_OFF = MLA_Q_LORA + MLA_ROPE
NEG = -1e30
VMEM_LIMIT = 52 * 1024 * 1024


def _cparams(sem, vmem=VMEM_LIMIT):
    return pltpu.CompilerParams(dimension_semantics=sem, vmem_limit_bytes=vmem)


def _bdot(a, b):
    return jnp.dot(a.astype(BF16), b.astype(BF16), preferred_element_type=F32)


def _nt(a, b):
    return lax.dot_general(a.astype(BF16), b.astype(BF16), (((1,), (1,)), ((), ())), preferred_element_type=F32)


def _tn(a, b):
    return lax.dot_general(a.astype(BF16), b.astype(BF16), (((0,), (0,)), ((), ())), preferred_element_type=F32)


def _split(x, n):
    parts = []
    r = x
    for _ in range(n):
        p = r.astype(BF16)
        parts.append(p)
        r = r - p.astype(F32)
    return parts


def _dot_exact_l(t01, x):
    t = t01.astype(BF16)
    return sum(jnp.dot(t, p, preferred_element_type=F32) for p in _split(x, 3))


def _dot_exact_r(x, t01):
    t = t01.astype(BF16)
    return sum(jnp.dot(p, t, preferred_element_type=F32) for p in _split(x, 3))


def _dot3(a, b):
    a1, a2 = _split(a, 2)
    b1, b2 = _split(b, 2)
    d = functools.partial(jnp.dot, preferred_element_type=F32)
    return d(a1, b1) + (d(a1, b2) + d(a2, b1))


def _silu(x):
    return x * jax.nn.sigmoid(x)


def _iota(shape, dim):
    return lax.broadcasted_iota(jnp.int32, shape, dim)


def _ada_kernel(c_ref, w_ref, b_ref, o_ref):
    o_ref[0] = _bdot(_silu(c_ref[...]), w_ref[0]) + b_ref[0]


def _ada(c_all, ada_w, ada_b):
    depth, d, n3 = ada_w.shape
    rows = c_all.shape[0]
    tn = 1024
    return pl.pallas_call(
        _ada_kernel,
        grid=(depth, n3 // tn),
        in_specs=[pl.BlockSpec((rows, d), lambda l, j: (0, 0)),
                  pl.BlockSpec((1, d, tn), lambda l, j: (l, 0, j)),
                  pl.BlockSpec((1, 1, tn), lambda l, j: (l, 0, j))],
        out_specs=pl.BlockSpec((1, rows, tn), lambda l, j: (l, 0, j)),
        out_shape=jax.ShapeDtypeStruct((depth, rows, n3), F32),
        compiler_params=_cparams(("arbitrary", "arbitrary")),
        name="ada",
    )(c_all, ada_w, ada_b.reshape(depth, 1, n3))


def _inproj_kernel(x_ref, mod_ref, g_ref, w_ref, o_ref, h_ref):
    @pl.when(pl.program_id(1) == 0)
    def _():
        x = x_ref[...]
        xn = x * lax.rsqrt(jnp.mean(x * x, axis=-1, keepdims=True) + EPS) * g_ref[...]
        mod = mod_ref[0]
        h = xn * (1.0 + mod[..., D_MODEL:]) + mod[..., :D_MODEL]
        h_ref[...] = h.reshape(-1, D_MODEL).astype(BF16)

    o_ref[...] = jnp.dot(h_ref[...], w_ref[...], preferred_element_type=F32)


def _row_maps(x_shape, nb, rows, mod_row0):
    per = x_shape[1] // rows
    if nb == 1:
        x_map = lambda i: (i // per, i % per, 0)
        m_row = lambda i: mod_row0 + i // per
    else:
        x_map = lambda i: (i, 0, 0)
        m_row = lambda i: i
    return x_map, m_row, (x_shape[0] // nb) * per


def _in_proj(x, mod4, norm_g, w, *, li, nb, rows, mod_row0):
    d = x.shape[-1]
    tm = nb * rows
    tn = 1536 if tm <= 512 else 768
    x_map, m_row, grid_m = _row_maps(x.shape, nb, rows, mod_row0)
    return pl.pallas_call(
        _inproj_kernel,
        grid=(grid_m, NCOL // tn),
        in_specs=[pl.BlockSpec((nb, rows, d), lambda i, j: x_map(i)),
                  pl.BlockSpec((1, nb, 1, 2 * d), lambda i, j: (li, m_row(i), 0, 0)),
                  pl.BlockSpec((1, d), lambda i, j: (0, 0)),
                  pl.BlockSpec((d, tn), lambda i, j: (0, j))],
        out_specs=pl.BlockSpec((tm, tn), lambda i, j: (i, j)),
        out_shape=jax.ShapeDtypeStruct((x.shape[0] * x.shape[1], NCOL), F32),
        scratch_shapes=[pltpu.VMEM((tm, d), BF16)],
        compiler_params=_cparams(("arbitrary", "arbitrary")),
        name="in_proj",
    )(x, mod4, norm_g.reshape(1, d), w)


def _prep_w_in(w):
    o = np.concatenate([[0], np.cumsum(IN_SIZES)])
    s5_u, s5_z, ssd_z, xbc, dt, cq, ckv, kpe, mla_z, qkv, dn_z, dn_a, dn_b = [
        w[:, int(o[i]):int(o[i + 1])] for i in range(len(IN_SIZES))]
    half = MLA_ROPE // 2
    partner = jnp.concatenate([-kpe[:, half:], kpe[:, :half]], axis=1)
    zeros = lambda n: jnp.zeros((w.shape[0], n), w.dtype)
    cols = [s5_u, s5_z, ssd_z, xbc, mla_z, qkv, dn_z,
            cq, kpe, dt, dn_a, dn_b, zeros(BLK - SMALL_OFF - 16),
            ckv, partner, zeros(BLK - MLA_KV_LORA - MLA_ROPE)]
    return jnp.concatenate(cols, axis=1).astype(BF16)


S5_GT = 8
S5_LANES = S5_GT * 2 * S5_STATE


def _cmul(x, tr, ti):
    n = x.shape[1]
    lo = _iota(x.shape, 1) % (2 * S5_STATE) < S5_STATE
    swapped = jnp.where(lo, pltpu.roll(x, n - S5_STATE, axis=1), pltpu.roll(x, S5_STATE, axis=1))
    return x * tr + swapped * ti


def _s5_kernel(u_ref, toep_ref, w_ref, v_ref, h0_ref, tr_ref, ti_ref, y_ref, hl_ref, *, nbatch, nch, scan):
    u = u_ref[0]
    s_in = jnp.dot(u, w_ref[0], preferred_element_type=F32)
    tr = tr_ref[0]
    ti = ti_ref[0]
    if scan:
        rows = _iota((nch, S5_LANES), 0)
        h_in, h_last = [], []
        for b in range(nbatch):
            x = s_in[b * nch:(b + 1) * nch]
            for k in range(int(math.log2(nch))):
                d = 1 << k
                xs = jnp.where(rows >= d, pltpu.roll(x, d, axis=0), 0.0)
                x = x + _cmul(xs, tr[k:k + 1], ti[k:k + 1])
            h_in.append(jnp.where(rows >= 1, pltpu.roll(x, 1, axis=0), 0.0))
            h_last.append(x[nch - 1:nch])
        h_in = jnp.concatenate(h_in, axis=0)
        hl_ref[0] = jnp.concatenate(h_last, axis=0)
    else:
        h_in = h0_ref[0]
        hl_ref[0] = _cmul(h_in, tr[0:1], ti[0:1]) + s_in
    y_ref[0] = jnp.dot(u, toep_ref[0], preferred_element_type=F32) + _bdot(h_in, v_ref[0])


def _s5_tables(p, L, nsteps):
    lam_r, lam_i = p['s5_lam_re'], p['s5_lam_im']
    delta = jnp.exp(p['s5_log_dt'])[:, None]
    lr, li = lam_r * delta, lam_i * delta

    def apow(k):
        k = jnp.asarray(k, F32)
        e = jnp.exp(lr[:, None, :] * k[None, :, None])
        return e * jnp.cos(li[:, None, :] * k[None, :, None]), e * jnp.sin(li[:, None, :] * k[None, :, None])

    a_r, a_i = apow(jnp.ones((1,), F32))
    a_r, a_i = a_r[:, 0], a_i[:, 0]
    den = lam_r * lam_r + lam_i * lam_i
    q_r = ((a_r - 1.0) * lam_r + a_i * lam_i) / den
    q_i = (a_i * lam_r - (a_r - 1.0) * lam_i) / den
    b_r = q_r[..., None] * p['s5_b_re'] - q_i[..., None] * p['s5_b_im']
    b_i = q_r[..., None] * p['s5_b_im'] + q_i[..., None] * p['s5_b_re']
    c_r, c_i = p['s5_c_re'], p['s5_c_im']

    pk_r, pk_i = apow(np.arange(L + 1))
    d_r = pk_r[:, :L, :, None] * b_r[:, None] - pk_i[:, :L, :, None] * b_i[:, None]
    d_i = pk_r[:, :L, :, None] * b_i[:, None] + pk_i[:, :L, :, None] * b_r[:, None]
    kern = jnp.einsum('gcp,gtpd->gdtc', c_r, d_r) - jnp.einsum('gcp,gtpd->gdtc', c_i, d_i)
    toep = jnp.stack([jnp.pad(kern[:, :, :L - s, :], ((0, 0), (0, 0), (s, 0), (0, 0))) for s in range(L)], axis=1)
    toep = toep.reshape(S5_GROUPS, L * S5_CH, L * S5_CH)
    rev = L - 1 - np.arange(L)
    bt_r, bt_i = jnp.swapaxes(b_r, 1, 2)[:, None], jnp.swapaxes(b_i, 1, 2)[:, None]
    w_r = pk_r[:, rev][:, :, None, :] * bt_r - pk_i[:, rev][:, :, None, :] * bt_i
    w_i = pk_r[:, rev][:, :, None, :] * bt_i + pk_i[:, rev][:, :, None, :] * bt_r
    w_mat = jnp.concatenate([w_r, w_i], axis=-1).reshape(S5_GROUPS, L * S5_CH, 2 * S5_STATE)
    ct_r, ct_i = jnp.swapaxes(c_r, 1, 2)[:, :, None, :], jnp.swapaxes(c_i, 1, 2)[:, :, None, :]
    pt_r, pt_i = jnp.swapaxes(pk_r, 1, 2)[:, :, 1:, None], jnp.swapaxes(pk_i, 1, 2)[:, :, 1:, None]
    v_r = ct_r * pt_r - ct_i * pt_i
    v_i = ct_r * pt_i + ct_i * pt_r
    v_mat = jnp.concatenate([v_r, -v_i], axis=1).reshape(S5_GROUPS, 2 * S5_STATE, L * S5_CH)
    s_r, s_i = apow((2.0 ** np.arange(nsteps)) * L)
    tr = jnp.concatenate([s_r, s_r], axis=-1)
    ti = jnp.concatenate([-s_i, s_i], axis=-1)
    nt = S5_GROUPS // S5_GT
    eye = jnp.eye(S5_GT, dtype=F32)
    toep = jnp.einsum('agsctd,gh->asgcthd', toep.reshape(nt, S5_GT, L, S5_CH, L, S5_CH), eye)
    toep = toep.reshape(nt, L * S5_GT * S5_CH, L * S5_GT * S5_CH)
    w_mat = jnp.einsum('agscp,gh->asgchp', w_mat.reshape(nt, S5_GT, L, S5_CH, 2 * S5_STATE), eye)
    w_mat = w_mat.reshape(nt, L * S5_GT * S5_CH, S5_LANES)
    v_mat = jnp.einsum('agptc,gh->agpthc', v_mat.reshape(nt, S5_GT, 2 * S5_STATE, L, S5_CH), eye)
    v_mat = v_mat.reshape(nt, S5_LANES, L * S5_GT * S5_CH)
    lanes = lambda a: a.reshape(nt, S5_GT, nsteps, 2 * S5_STATE).transpose(0, 2, 1, 3).reshape(nt, nsteps, S5_LANES)
    return toep.astype(BF16), w_mat.astype(BF16), v_mat.astype(BF16), lanes(tr), lanes(ti)


def _s5_core(u_g, h0_g, tables, *, nbatch, nch, scan):
    toep, w_mat, v_mat, tr, ti = tables
    g, r, lc = u_g.shape
    nk = tr.shape[1]
    nb_out = nbatch if scan else r
    blk = lambda shape: pl.BlockSpec((1,) + shape, lambda i: (i, 0, 0))
    kern = functools.partial(_s5_kernel, nbatch=nbatch, nch=nch, scan=scan)
    return pl.pallas_call(
        kern,
        grid=(g,),
        in_specs=[blk((r, lc)), blk((lc, lc)), blk((lc, S5_LANES)), blk((S5_LANES, lc)),
                  blk(h0_g.shape[1:]), blk((nk, S5_LANES)), blk((nk, S5_LANES))],
        out_specs=[blk((r, lc)), blk((nb_out, S5_LANES))],
        out_shape=[jax.ShapeDtypeStruct((g, r, lc), F32), jax.ShapeDtypeStruct((g, nb_out, S5_LANES), F32)],
        compiler_params=_cparams(("arbitrary",)),
        name="s5_core",
    )(u_g, toep, w_mat, v_mat, h0_g, tr, ti)


def _s5_post_kernel(yc_ref, u_ref, z_ref, d_ref, w_ref, b_ref, o_ref):
    y = jax.nn.gelu(yc_ref[...] + d_ref[...] * u_ref[...])
    y = y * jax.nn.sigmoid(_bdot(y, w_ref[...]) + b_ref[...])
    o_ref[...] = (y * _silu(z_ref[...])).astype(BF16)


def _s5_post(ycore, proj, p, *, tm):
    t = ycore.shape[0]
    row = lambda blk: pl.BlockSpec((tm, BLK), lambda i, blk=blk: (i, blk))
    full = lambda shape: pl.BlockSpec(shape, lambda i: (0, 0))
    return pl.pallas_call(
        _s5_post_kernel,
        grid=(t // tm,),
        in_specs=[row(0), row(B_S5U), row(B_S5Z), full((1, BLK)), full((BLK, BLK)), full((1, BLK))],
        out_specs=row(0),
        out_shape=jax.ShapeDtypeStruct((t, BLK), BF16),
        compiler_params=_cparams(("arbitrary",)),
        name="s5_post",
    )(ycore, proj, proj, p['s5_d'].reshape(1, BLK), p['s5_glu_w'].astype(BF16), p['s5_glu_b'].reshape(1, BLK))


def _causal_conv(tail, cur, w_ref, L):
    ext = jnp.concatenate([tail, cur], axis=0)
    base = 8 - (CONV_K - 1)
    out = ext[base:base + L] * w_ref[0:1, :]
    for j in range(1, CONV_K):
        out = out + ext[base + j:base + j + L] * w_ref[j:j + 1, :]
    return out


def _tri(L, strict=False):
    r = _iota((L, L), 0)
    c = _iota((L, L), 1)
    return (r > c) if strict else (r >= c)


def _pair_select(L, a, b):
    return jnp.where(_iota((L, 128), 1) < 64, a, b)


def _seq_specs(nb, L, nch, li, hd, conv_dim, blocks):
    row = lambda blk: pl.BlockSpec((nb, L, BLK), lambda i, c, blk=blk: (i, c, blk))
    specs = [row(b) for b in blocks]
    specs += [pl.BlockSpec((nb, L, 16), lambda i, c: (i, c, 0)),
              pl.BlockSpec((nb, 1, 16, L), lambda i, c: (i, c, 0, 0)),
              pl.BlockSpec((nb, 8, conv_dim), lambda i, c: (i, 0, 0)),
              pl.BlockSpec((1, nb) + hd, lambda i, c: (li, i, 0, 0, 0))]
    out_specs = [pl.BlockSpec((nb, L, BLK), lambda i, c: (i, c, 0)),
                 pl.BlockSpec((nb,) + hd, lambda i, c: (i, 0, 0, 0))]
    return specs, out_specs


def _ssd_kernel(z_ref, x_ref, bc_ref, sm_ref, smt_ref, buf_ref, h0_ref,
                cw_ref, cb_ref, dtb_ref, dtbt_ref, al_ref, alt_ref, drow_ref, ng_ref,
                y_ref, hl_ref, tail_ref, st_ref, *, nb, L, nvalid):
    c = pl.program_id(1)

    @pl.when(c == 0)
    def _():
        tail_ref[...] = buf_ref[...]
        st_ref[...] = h0_ref[0]

    tri = _tri(L)
    triu = (_iota((L, L), 0) <= _iota((L, L), 1)).astype(F32)
    rows128 = _iota((128, SSD_STATE), 0)
    gn = SSD_GROUPS * SSD_STATE
    half = BRANCH // SSD_GROUPS
    lane_lo = _iota((L, 128), 1) < 64
    xs_, dtc_, cumc_, cumr_, bm_, cm_ = {}, {}, {}, {}, {}, {}
    for j in range(nb):
        xbc = jnp.concatenate([x_ref[j], bc_ref[j]], axis=1)
        conv = _silu(_causal_conv(tail_ref[j], xbc, cw_ref, L) + cb_ref[...])
        tail_ref[j] = xbc[L - 8:L]
        xs_[j] = conv[:, :BRANCH]
        dtc = jax.nn.softplus(sm_ref[j, :, 0:SSD_HEADS] + dtb_ref[...])
        dtr = jax.nn.softplus(smt_ref[j, 0, 0:SSD_HEADS, :] + dtbt_ref[...])
        if nvalid < L:
            dtc = jnp.where(_iota(dtc.shape, 0) < nvalid, dtc, 0.0)
            dtr = jnp.where(_iota(dtr.shape, 1) < nvalid, dtr, 0.0)
        dtc_[j] = dtc
        cumc_[j] = _dot_exact_l(tri.astype(F32), dtc * (-jnp.exp(al_ref[...])))
        cumr_[j] = _dot_exact_r(dtr * (-jnp.exp(alt_ref[...])), triu)
        for g in range(SSD_GROUPS):
            bm_[j, g] = conv[:, BRANCH + g * SSD_STATE:BRANCH + (g + 1) * SSD_STATE]
            cm_[j, g] = conv[:, BRANCH + gn + g * SSD_STATE:BRANCH + gn + (g + 1) * SSD_STATE]
    groups = [(j, g) for j in range(nb) for g in range(SSD_GROUPS)]
    pairs = [(j, g, pr) for (j, g) in groups for pr in range(2)]
    heads = lambda c_: (c_[1] * 4 + c_[2] * 2, c_[1] * 4 + c_[2] * 2 + 1)
    col = lambda a, h: a[:, h:h + 1]
    cb = {c_: _nt(cm_[c_], bm_[c_]) for c_ in groups}
    st = {c_: st_ref[c_[0], heads(c_)[0]:heads(c_)[0] + 2].reshape(2 * SSD_HEADDIM, SSD_STATE) for c_ in pairs}
    y_off = {c_: _nt(cm_[c_[:2]], st[c_]) for c_ in pairs}
    xdt, upd, y_diag = {}, {}, {}
    for c_ in pairs:
        j, (h0, h1) = c_[0], heads(c_)
        xdt[c_] = xs_[j][:, h0 * SSD_HEADDIM:(h0 + 2) * SSD_HEADDIM] * jnp.where(lane_lo, col(dtc_[j], h0), col(dtc_[j], h1))
    for c_ in pairs:
        j, (h0, h1) = c_[0], heads(c_)
        edec = jnp.exp(cumc_[j][L - 1:L] - cumc_[j])
        upd[c_] = _tn(xdt[c_] * jnp.where(lane_lo, col(edec, h0), col(edec, h1)), bm_[c_[:2]])
    for c_ in pairs:
        j, (h0, h1) = c_[0], heads(c_)
        yd = [_bdot(cb[c_[:2]] * jnp.exp(jnp.where(tri, col(cumc_[j], h) - cumr_[j][h:h + 1, :], NEG)), xdt[c_])
              for h in (h0, h1)]
        y_diag[c_] = jnp.where(lane_lo, yd[0], yd[1])
    for c_ in pairs:
        j, (h0, h1) = c_[0], heads(c_)
        etot = jnp.exp(cumc_[j][L - 1:L])
        st_new = st[c_] * jnp.where(rows128 < 64, col(etot, h0), col(etot, h1)) + upd[c_]
        st_ref[j, h0:h0 + 2] = st_new.reshape(2, SSD_HEADDIM, SSD_STATE)
    for j in range(nb):
        ecum = jnp.exp(cumc_[j])
        y_parts = []
        for g in range(SSD_GROUPS):
            for pr in range(2):
                h0, h1 = heads((j, g, pr))
                y_parts.append(y_diag[j, g, pr] + y_off[j, g, pr] * jnp.where(lane_lo, col(ecum, h0), col(ecum, h1)))
        xs = xs_[j]
        y = jnp.concatenate(y_parts, axis=1) + drow_ref[...] * xs
        y = y * _silu(z_ref[j])
        outs = []
        for g in range(SSD_GROUPS):
            yg = y[:, g * half:(g + 1) * half]
            outs.append(yg * lax.rsqrt(jnp.mean(yg * yg, axis=-1, keepdims=True) + EPS))
        y_ref[j] = (jnp.concatenate(outs, axis=1) * ng_ref[...]).astype(y_ref.dtype)

    @pl.when(c == pl.num_programs(1) - 1)
    def _():
        hl_ref[...] = st_ref[...]


def _ssd(proj3, small, small_t, buf8, h0, p, *, li, nb, nch, L, nvalid):
    nbatch, r, _ = proj3.shape
    full = lambda shape: pl.BlockSpec(shape, lambda i, c: (0,) * len(shape))
    hd = (SSD_HEADS, SSD_HEADDIM, SSD_STATE)
    specs, out_specs = _seq_specs(nb, L, nch, li, hd, SSD_CONV_DIM, (B_SSDZ, B_SSDX, B_SSDBC))
    kern = functools.partial(_ssd_kernel, nb=nb, L=L, nvalid=nvalid)
    return pl.pallas_call(
        kern,
        grid=(nbatch // nb, nch),
        in_specs=specs + [full((CONV_K, SSD_CONV_DIM)), full((1, SSD_CONV_DIM)),
                          full((1, SSD_HEADS)), full((SSD_HEADS, 1)), full((1, SSD_HEADS)), full((SSD_HEADS, 1)),
                          full((1, BRANCH)), full((1, BRANCH))],
        out_specs=out_specs,
        out_shape=[jax.ShapeDtypeStruct((nbatch, r, BLK), BF16 if L % 16 == 0 else F32),
                   jax.ShapeDtypeStruct((nbatch,) + hd, F32)],
        scratch_shapes=[pltpu.VMEM((nb, 8, SSD_CONV_DIM), F32), pltpu.VMEM((nb,) + hd, F32)],
        compiler_params=_cparams(("arbitrary", "arbitrary")),
        name="ssd",
    )(proj3, proj3, proj3, small, small_t, buf8, h0,
      p['ssd_conv_w'], p['ssd_conv_b'].reshape(1, -1),
      p['ssd_dt_bias'].reshape(1, -1), p['ssd_dt_bias'].reshape(-1, 1),
      p['ssd_a_log'].reshape(1, -1), p['ssd_a_log'].reshape(-1, 1),
      jnp.repeat(p['ssd_d'], SSD_HEADDIM).reshape(1, BRANCH), p['ssd_norm_g'].reshape(1, BRANCH))


def _dn_kernel(q_ref, k_ref, v_ref, z_ref, sm_ref, smt_ref, buf_ref, s0_ref,
               cw_ref, al_ref, alt_ref, dtb_ref, dtbt_ref, ng_ref,
               y_ref, sl_ref, tail_ref, st_ref, *, nb, L, nvalid):
    c = pl.program_id(1)

    @pl.when(c == 0)
    def _():
        tail_ref[...] = buf_ref[...]
        st_ref[...] = s0_ref[0]

    a_off = SSD_HEADS
    b_off = SSD_HEADS + DN_HEADS
    tri = _tri(L)
    stri = _tri(L, strict=True)
    triu = (_iota((L, L), 0) <= _iota((L, L), 1)).astype(F32)
    eye = (_iota((L, L), 0) == _iota((L, L), 1)).astype(F32)
    kd = DN_HEADS * DN_DK
    chains = [(j, h) for j in range(nb) for h in range(DN_HEADS)]
    qs, ks, vs, gcis, bcols, gammas = {}, {}, {}, {}, {}, {}
    for j in range(nb):
        qkv = jnp.concatenate([q_ref[j], k_ref[j], v_ref[j]], axis=1)
        conv = _silu(_causal_conv(tail_ref[j], qkv, cw_ref, L))
        tail_ref[j] = qkv[L - 8:L]

        gc = -jnp.exp(al_ref[...]) * jax.nn.softplus(sm_ref[j, :, a_off:a_off + DN_HEADS] + dtb_ref[...])
        gr = -jnp.exp(alt_ref[...]) * jax.nn.softplus(smt_ref[j, 0, a_off:a_off + DN_HEADS, :] + dtbt_ref[...])
        beta = jax.nn.sigmoid(sm_ref[j, :, b_off:b_off + DN_HEADS])
        if nvalid < L:
            vc = _iota(gc.shape, 0) < nvalid
            gc = jnp.where(vc, gc, 0.0)
            beta = jnp.where(vc, beta, 0.0)
            gr = jnp.where(_iota(gr.shape, 1) < nvalid, gr, 0.0)
        gcc = _dot_exact_l(tri.astype(F32), gc)
        gcr = _dot_exact_r(gr, triu)
        for h in range(DN_HEADS):
            qh = conv[:, h * DN_DK:(h + 1) * DN_DK]
            kh = conv[:, kd + h * DN_DK:kd + (h + 1) * DN_DK]
            qs[j, h] = qh * lax.rsqrt(jnp.sum(qh * qh, axis=-1, keepdims=True) + EPS) * (DN_DK ** -0.5)
            ks[j, h] = kh * lax.rsqrt(jnp.sum(kh * kh, axis=-1, keepdims=True) + EPS)
            vs[j, h] = conv[:, 2 * kd + h * DN_DV:2 * kd + (h + 1) * DN_DV]
            gcis[j, h] = gcc[:, h:h + 1]
            bcols[j, h] = beta[:, h:h + 1]
            gammas[j, h] = jnp.exp(jnp.where(tri, gcc[:, h:h + 1] - gcr[h:h + 1, :], NEG))

    kk = {c_: _nt(ks[c_], ks[c_]) for c_ in chains}
    qk = {c_: _nt(qs[c_], ks[c_]) for c_ in chains}
    pw = {c_: jnp.where(stri, -(bcols[c_] * kk[c_] * gammas[c_]), 0.0) for c_ in chains}
    inv = {c_: eye + pw[c_] for c_ in chains}
    for _ in range(int(math.log2(L)) - 1):
        pw = {c_: _dot3(pw[c_], pw[c_]) for c_ in chains}
        inv = {c_: inv[c_] + _dot3(inv[c_], pw[c_]) for c_ in chains}
    sol = {c_: _dot3(inv[c_], jnp.concatenate([vs[c_] * bcols[c_], ks[c_] * (bcols[c_] * jnp.exp(gcis[c_]))], axis=1))
           for c_ in chains}
    st = {c_: st_ref[c_[0], c_[1]] for c_ in chains}
    v_new = {c_: sol[c_][:, :DN_DV] - _bdot(sol[c_][:, DN_DV:], st[c_]) for c_ in chains}
    o_s = {c_: _bdot(qs[c_] * jnp.exp(gcis[c_]), st[c_]) for c_ in chains}
    o_a = {c_: _bdot(jnp.where(tri, qk[c_] * gammas[c_], 0.0), v_new[c_]) for c_ in chains}
    upd = {c_: _tn(ks[c_] * jnp.exp(gcis[c_][L - 1:L] - gcis[c_]), v_new[c_]) for c_ in chains}
    for c_ in chains:
        st_ref[c_[0], c_[1]] = st[c_] * jnp.exp(gcis[c_][L - 1:L]) + upd[c_]
    for j in range(nb):
        outs = []
        for h in range(DN_HEADS):
            o = o_s[j, h] + o_a[j, h]
            o = o * lax.rsqrt(jnp.mean(o * o, axis=-1, keepdims=True) + EPS) * ng_ref[...]
            outs.append(o * _silu(z_ref[j, :, h * DN_DV:(h + 1) * DN_DV]))
        y_ref[j] = jnp.concatenate(outs, axis=1).astype(y_ref.dtype)

    @pl.when(c == pl.num_programs(1) - 1)
    def _():
        sl_ref[...] = st_ref[...]


def _dn(proj3, small, small_t, buf8, s0, p, *, li, nb, nch, L, nvalid):
    nbatch, r, _ = proj3.shape
    full = lambda shape: pl.BlockSpec(shape, lambda i, c: (0,) * len(shape))
    hd = (DN_HEADS, DN_DK, DN_DV)
    specs, out_specs = _seq_specs(nb, L, nch, li, hd, DN_CONV_DIM, (B_DNQ, B_DNK, B_DNV, B_DNZ))
    kern = functools.partial(_dn_kernel, nb=nb, L=L, nvalid=nvalid)
    return pl.pallas_call(
        kern,
        grid=(nbatch // nb, nch),
        in_specs=specs + [full((CONV_K, DN_CONV_DIM)),
                          full((1, DN_HEADS)), full((DN_HEADS, 1)), full((1, DN_HEADS)), full((DN_HEADS, 1)),
                          full((1, DN_DV))],
        out_specs=out_specs,
        out_shape=[jax.ShapeDtypeStruct((nbatch, r, BLK), BF16 if L % 16 == 0 else F32),
                   jax.ShapeDtypeStruct((nbatch,) + hd, F32)],
        scratch_shapes=[pltpu.VMEM((nb, 8, DN_CONV_DIM), F32), pltpu.VMEM((nb,) + hd, F32)],
        compiler_params=_cparams(("arbitrary", "arbitrary")),
        name="dn",
    )(proj3, proj3, proj3, proj3, small, small_t, buf8, s0,
      p['dn_conv_w'],
      p['dn_a_log'].reshape(1, -1), p['dn_a_log'].reshape(-1, 1),
      p['dn_dt_bias'].reshape(1, -1), p['dn_dt_bias'].reshape(-1, 1),
      p['dn_norm_g'].reshape(1, -1))


def _mla_pre_kernel(a_ref, b_ref, cos_ref, sin_ref, qng_ref, kvng_ref, wuq_ref, gq_ref, *rest, expand_kv):
    if expand_kv:
        wukv_ref, gk_ref, q_ref, lat_ref, kpe_ref, k_ref, v_ref = rest
    else:
        q_ref, lat_ref, kpe_ref = rest
    cos = cos_ref[...]
    sin = sin_ref[...]
    cq = a_ref[:, :MLA_Q_LORA]
    cqn = cq * lax.rsqrt(jnp.mean(cq * cq, axis=-1, keepdims=True) + EPS) * qng_ref[...]
    qp = _bdot(cqn, wuq_ref[...])
    for h in range(MLA_HEADS):
        nope = qp[:, h * 384:h * 384 + 128]
        ro = qp[:, h * 384 + 128:h * 384 + 256] * cos + qp[:, h * 384 + 256:h * 384 + 384] * sin
        ms = (jnp.sum(nope * nope, axis=-1, keepdims=True) + jnp.sum(ro * ro, axis=-1, keepdims=True)) / MLA_QK
        rinv = lax.rsqrt(ms + EPS)
        q_ref[:, h * 256:h * 256 + 128] = (nope * rinv * gq_ref[:, :128]).astype(q_ref.dtype)
        q_ref[:, h * 256 + 128:(h + 1) * 256] = (ro * rinv * gq_ref[:, 128:]).astype(q_ref.dtype)
    ckv = b_ref[:, :MLA_KV_LORA]
    lat = ckv * lax.rsqrt(jnp.mean(ckv * ckv, axis=-1, keepdims=True) + EPS) * kvng_ref[...]
    lat_ref[...] = lat
    kpe = a_ref[:, MLA_Q_LORA:] * cos + b_ref[:, MLA_KV_LORA:MLA_KV_LORA + 128] * sin
    kpe_ref[...] = kpe[:, :MLA_ROPE]
    if expand_kv:
        kv = _bdot(lat, wukv_ref[...])
        pe_sq = jnp.sum(kpe * kpe, axis=-1, keepdims=True)
        for h in range(MLA_HEADS):
            kn = kv[:, h * 256:h * 256 + 128]
            rinv = lax.rsqrt((jnp.sum(kn * kn, axis=-1, keepdims=True) + pe_sq) / MLA_QK + EPS)
            k_ref[:, h * 256:h * 256 + 128] = (kn * rinv * gk_ref[:, :128]).astype(BF16)
            k_ref[:, h * 256 + 128:(h + 1) * 256] = (kpe * rinv * gk_ref[:, 128:]).astype(BF16)
            v_ref[:, h * 256:h * 256 + 128] = kv[:, h * 256 + 128:(h + 1) * 256].astype(BF16)
            v_ref[:, h * 256 + 128:(h + 1) * 256] = jnp.ones((kn.shape[0], 128), BF16)


def _pad_gain(g, scale=1.0):
    return jnp.concatenate([g * scale, jnp.zeros((256 - MLA_QK,), F32)]).reshape(1, 256)


def _mla_pre(proj, cos, sin, p, *, tm, pos_blocks, expand_kv):
    t = proj.shape[0]
    half = MLA_ROPE // 2
    wq = p['mla_w_uq'].reshape(MLA_Q_LORA, MLA_HEADS, MLA_QK)
    nope, ro = wq[..., :MLA_NOPE], wq[..., MLA_NOPE:]
    zeros = jnp.zeros((MLA_Q_LORA, MLA_HEADS, 128 - MLA_ROPE), F32)
    partner = jnp.concatenate([-ro[..., half:], ro[..., :half]], axis=-1)
    wuq = jnp.concatenate([nope, ro, zeros, partner, zeros], axis=-1).reshape(MLA_Q_LORA, MLA_HEADS * 384).astype(BF16)
    row = lambda blk: pl.BlockSpec((tm, BLK), lambda i, blk=blk: (i, blk))
    tab = pl.BlockSpec((tm, 128), lambda i: (i % pos_blocks, 0))
    full = lambda shape: pl.BlockSpec(shape, lambda i: (0, 0))
    in_specs = [row(B_MLAQ), row(B_MLAKV), tab, tab, full((1, MLA_Q_LORA)), full((1, MLA_KV_LORA)),
                full((MLA_Q_LORA, MLA_HEADS * 384)), full((1, 256))]
    args = [proj, proj, cos, sin, p['mla_q_norm_g'].reshape(1, -1), p['mla_kv_norm_g'].reshape(1, -1),
            wuq, _pad_gain(p['mla_q_g'], MLA_QK ** -0.5)]
    out_specs = [pl.BlockSpec((tm, 1024), lambda i: (i, 0)), pl.BlockSpec((tm, MLA_KV_LORA), lambda i: (i, 0)),
                 pl.BlockSpec((tm, MLA_ROPE), lambda i: (i, 0))]
    out_shape = [jax.ShapeDtypeStruct((t, 1024), BF16 if expand_kv else F32),
                 jax.ShapeDtypeStruct((t, MLA_KV_LORA), F32), jax.ShapeDtypeStruct((t, MLA_ROPE), F32)]
    if expand_kv:
        in_specs += [full((MLA_KV_LORA, MLA_HEADS * 256)), full((1, 256))]
        args += [p['mla_w_ukv'].astype(BF16), _pad_gain(p['mla_k_g'])]
        out_specs += [pl.BlockSpec((tm, 1024), lambda i: (i, 0)), pl.BlockSpec((tm, 1024), lambda i: (i, 0))]
        out_shape += [jax.ShapeDtypeStruct((t, 1024), BF16), jax.ShapeDtypeStruct((t, 1024), BF16)]
    return pl.pallas_call(
        functools.partial(_mla_pre_kernel, expand_kv=expand_kv),
        grid=(t // tm,),
        in_specs=in_specs, out_specs=out_specs, out_shape=out_shape,
        compiler_params=_cparams(("arbitrary",)),
        name="mla_pre",
    )(*args)


def _rope_tables(pos):
    half = MLA_ROPE // 2
    inv = ROPE_THETA ** (-jnp.arange(half, dtype=F32) / half)
    ang = pos[:, None] * inv[None, :]
    z = jnp.zeros((pos.shape[0], 128 - MLA_ROPE), F32)
    cos, sin = jnp.cos(ang), jnp.sin(ang)
    return jnp.concatenate([cos, cos, z], axis=1), jnp.concatenate([sin, sin, z], axis=1)


def _flash_kernel(q_ref, k_ref, v_ref, z_ref, o_ref, m_ref, acc_ref, *, tq, tk):
    i = pl.program_id(2)
    m_ref[...] = jnp.full(m_ref.shape, NEG, F32)
    acc_ref[...] = jnp.zeros(acc_ref.shape, F32)
    q = q_ref[...]
    jd = (i * tq) // tk
    row = _iota((tq, tk), 0) + (i * tq - jd * tk)
    col = _iota((tq, tk), 1)

    def step(j, masked):
        k = k_ref[pl.ds(pl.multiple_of(j * tk, tk), tk), :]
        v = v_ref[pl.ds(pl.multiple_of(j * tk, tk), tk), :]
        s = lax.dot_general(q, k, (((1,), (1,)), ((), ())), preferred_element_type=F32)
        if masked:
            s = jnp.where(col <= row, s, NEG)
        m_prev = m_ref[...]
        m_new = jnp.maximum(m_prev, jnp.max(s, axis=-1, keepdims=True))
        pr = jnp.exp(s - m_new)
        corr = jnp.exp(m_prev - m_new)
        acc_ref[...] = acc_ref[...] * corr + jnp.dot(pr.astype(BF16), v, preferred_element_type=F32)
        m_ref[...] = m_new

    def body(j, carry):
        step(j, False)
        return carry

    lax.fori_loop(0, jd, body, 0)
    step(jd, True)
    o_ref[...] = (acc_ref[:, :MLA_V] / acc_ref[:, MLA_V:] * _silu(z_ref[...])).astype(BF16)


def _flash(q, k, v, proj, *, nbatch, seq, tq, tk):
    t = q.shape[0]
    nq = seq // tq
    kern = functools.partial(_flash_kernel, tq=tq, tk=tk)
    return pl.pallas_call(
        kern,
        grid=(nbatch, MLA_HEADS, nq),
        in_specs=[pl.BlockSpec((tq, 256), lambda b, h, i: (b * nq + i, h)),
                  pl.BlockSpec((seq, 256), lambda b, h, i: (b, h)),
                  pl.BlockSpec((seq, 2 * MLA_V), lambda b, h, i: (b, h)),
                  pl.BlockSpec((tq, MLA_V), lambda b, h, i: (b * nq + i, B_MLAZ * (BLK // MLA_V) + h))],
        out_specs=pl.BlockSpec((tq, MLA_V), lambda b, h, i: (b * nq + i, h)),
        out_shape=jax.ShapeDtypeStruct((t, BRANCH), BF16),
        scratch_shapes=[pltpu.VMEM((tq, 1), F32), pltpu.VMEM((tq, 2 * MLA_V), F32)],
        compiler_params=_cparams(("arbitrary", "arbitrary", "arbitrary")),
        name="flash",
    )(q, k, v, proj)


def _decode_kernel(pt_ref, q_ref, latn_ref, pen_ref, wukt_ref, gk_ref, lat_hbm, pet_hbm, o_ref,
                   latbuf, pebuf, sem, *, li, pps, ts, n_pages):
    b = pl.program_id(0)
    nseq = pl.num_programs(0)
    ng = n_pages // pps
    nrow = MLA_HEADS * ROWS_S
    nk = MLA_HEADS * MLA_NOPE
    per = ts // PAGE_SIZE

    def copies(seq, g, slot, k):
        ph = pt_ref[seq, g * pps + k]
        return (pltpu.make_async_copy(lat_hbm.at[li, ph], latbuf.at[slot, k], sem.at[slot]),
                pltpu.make_async_copy(pet_hbm.at[li, ph], pebuf.at[slot, k], sem.at[slot]))

    def fetch(seq, g, slot, pages=None):
        for k in (range(pps) if pages is None else pages):
            for cp in copies(seq, g, slot, k):
                cp.start(priority=k % 2)

    def wait(g, slot):
        for k in range(pps):
            for cp in copies(b, g, slot, k):
                cp.wait()

    @pl.when(b == 0)
    def _():
        fetch(0, 0, 0)

    q = q_ref[...]
    gk = gk_ref[...]
    qa, qpe = [], []
    for h in range(MLA_HEADS):
        qn = q[:, h * 256:h * 256 + 128] * gk[:, :128]
        qa.append(_bdot(qn, wukt_ref[h * 128:(h + 1) * 128, :]))
        qpe.append(q[:, h * 256 + 128:h * 256 + 128 + MLA_ROPE] * gk[:, 128:128 + MLA_ROPE])
    amats = [jnp.concatenate([wukt_ref[hp * 2 * MLA_NOPE:(hp + 1) * 2 * MLA_NOPE, :],
                              qa[2 * hp].astype(BF16), qa[2 * hp + 1].astype(BF16)], axis=0)
             for hp in range(MLA_HEADS // 2)]
    qpe = jnp.concatenate(qpe, axis=0).astype(BF16)

    amat = jnp.concatenate(amats, axis=0)
    npair = 2 * MLA_NOPE + 2 * ROWS_S

    def project(lat_b):
        n = lat_b.shape[0]
        if n < 512:
            return [lax.dot_general(a, lat_b, (((1,), (1,)), ((), ())), preferred_element_type=F32) for a in amats]
        r = jnp.concatenate([lax.dot_general(amat, lat_b[i * (n // 2):(i + 1) * (n // 2)], (((1,), (1,)), ((), ())),
                                             preferred_element_type=F32) for i in range(2)], axis=1)
        return [r[hp * npair:(hp + 1) * npair] for hp in range(MLA_HEADS // 2)]

    def scores(rs, s_pe, pe_sq):
        s = []
        for h in range(MLA_HEADS):
            r = rs[h // 2]
            o = (h % 2) * MLA_NOPE
            kn = r[o:o + MLA_NOPE]
            rinv = lax.rsqrt((jnp.sum(kn * kn, axis=0, keepdims=True) + pe_sq) / MLA_QK + EPS)
            sq = r[2 * MLA_NOPE + (h % 2) * ROWS_S:2 * MLA_NOPE + (h % 2 + 1) * ROWS_S]
            s.append((sq + s_pe[h * ROWS_S:(h + 1) * ROWS_S]) * rinv)
        return jnp.concatenate(s, axis=0)

    def partial_softmax(s, lat_b):
        m = jnp.max(s, axis=-1, keepdims=True)
        pr = jnp.exp(s - m)
        return m, jnp.sum(pr, axis=-1, keepdims=True), jnp.dot(pr.astype(BF16), lat_b, preferred_element_type=F32)

    def merge(carry, parts):
        m, l, acc = carry
        m_new = m
        for pm, _, _ in parts:
            m_new = jnp.maximum(m_new, pm)
        corr = jnp.exp(m - m_new)
        l, acc = l * corr, acc * corr
        for pm, pl_, pv in parts:
            w = jnp.exp(pm - m_new)
            l, acc = l + pl_ * w, acc + pv * w
        return m_new, l, acc

    lat_n = latn_ref[...].astype(BF16)
    pe_n = pen_ref[...]
    s_pe_n = _nt(qpe, pe_n)
    pe_sq_n = _nt(jnp.ones((8, MLA_ROPE), F32), pe_n * pe_n)[0:1]
    qrow = _iota((nrow, ROWS_S), 0) % ROWS_S
    s_n = jnp.where(_iota((nrow, ROWS_S), 1) <= qrow, scores(project(lat_n), s_pe_n, pe_sq_n), NEG)
    carry = (jnp.full((nrow, 1), NEG, F32), jnp.zeros((nrow, 1), F32), jnp.zeros((nrow, MLA_KV_LORA), F32))
    carry = merge(carry, [partial_softmax(s_n, lat_n)])

    def group(g, slot, carry, nxt):
        wait(g, slot)
        tiles = range(pps // per)
        lat_b = [latbuf[slot, k * per:(k + 1) * per].reshape(ts, MLA_KV_LORA).astype(BF16) for k in tiles]
        rs = []
        for k in tiles:
            fetch(*nxt, pages=range(k * per, (k + 1) * per))
            rs.append(project(lat_b[k]))
        pet = [jnp.concatenate([pebuf[slot, k * per + i] for i in range(per)], axis=1) for k in tiles]
        s_pe = [jnp.dot(qpe, x.astype(BF16), preferred_element_type=F32) for x in pet]
        pe_sq = [jnp.sum(x * x, axis=0, keepdims=True) for x in pet]
        ss = [scores(rs[k], s_pe[k], pe_sq[k]) for k in tiles]
        return merge(carry, [partial_softmax(ss[k], lat_b[k]) for k in tiles])

    def two_groups(gg, carry):
        carry = group(2 * gg, 0, carry, (b, 2 * gg + 1, 1))
        last = 2 * gg + 2 >= ng
        nxt = (jnp.where(last, jnp.minimum(b + 1, nseq - 1), b), jnp.where(last, 0, 2 * gg + 2), 0)
        return group(2 * gg + 1, 1, carry, nxt)

    m, l, acc = lax.fori_loop(0, ng // 2, two_groups, carry)
    o_ref[0] = acc / l

    @pl.when(b == nseq - 1)
    def _():
        wait(0, 0)


def _decode(page_table, q, lat_new, pe_new, cache_lat, cache_pet, p, *, li, pps, ts):
    nseq, n_pages = page_table.shape
    nrow = MLA_HEADS * ROWS_S
    wukv = p['mla_w_ukv'].reshape(MLA_KV_LORA, MLA_HEADS, MLA_NOPE + MLA_V)
    wukt = wukv[..., :MLA_NOPE].transpose(1, 2, 0).reshape(MLA_HEADS * MLA_NOPE, MLA_KV_LORA).astype(BF16)
    kern = functools.partial(_decode_kernel, li=li, pps=pps, ts=ts, n_pages=n_pages)
    grid_spec = pltpu.PrefetchScalarGridSpec(
        num_scalar_prefetch=1,
        grid=(nseq,),
        in_specs=[pl.BlockSpec((ROWS_S, 1024), lambda b, pt: (b, 0)),
                  pl.BlockSpec((ROWS_S, MLA_KV_LORA), lambda b, pt: (b, 0)),
                  pl.BlockSpec((ROWS_S, MLA_ROPE), lambda b, pt: (b, 0)),
                  pl.BlockSpec((MLA_HEADS * MLA_NOPE, MLA_KV_LORA), lambda b, pt: (0, 0)),
                  pl.BlockSpec((1, 256), lambda b, pt: (0, 0)),
                  pl.BlockSpec(memory_space=pl.ANY),
                  pl.BlockSpec(memory_space=pl.ANY)],
        out_specs=pl.BlockSpec((1, nrow, MLA_KV_LORA), lambda b, pt: (b, 0, 0)),
        scratch_shapes=[pltpu.VMEM((2, pps, PAGE_SIZE, MLA_KV_LORA), F32),
                        pltpu.VMEM((2, pps, MLA_ROPE, PAGE_SIZE), F32),
                        pltpu.SemaphoreType.DMA((2,))],
    )
    return pl.pallas_call(
        kern,
        grid_spec=grid_spec,
        out_shape=jax.ShapeDtypeStruct((nseq, nrow, MLA_KV_LORA), F32),
        compiler_params=_cparams(("arbitrary",)),
        name="decode",
    )(page_table, q, lat_new, pe_new, wukt, _pad_gain(p['mla_k_g']), cache_lat, cache_pet)


def _mla_post_kernel(o_ref, z_ref, wuv_ref, y_ref, *, sb):
    for h in range(MLA_HEADS):
        x = o_ref[:, h * ROWS_S:(h + 1) * ROWS_S, :].reshape(sb * ROWS_S, MLA_KV_LORA)
        y = _bdot(x, wuv_ref[h]) * _silu(z_ref[:, h * MLA_V:(h + 1) * MLA_V])
        y_ref[:, h * MLA_V:(h + 1) * MLA_V] = y.astype(BF16)


def _mla_post(o_lat, proj, p, *, sb):
    nseq = o_lat.shape[0]
    nrow = MLA_HEADS * ROWS_S
    wukv = p['mla_w_ukv'].reshape(MLA_KV_LORA, MLA_HEADS, MLA_NOPE + MLA_V)
    wuv = wukv[..., MLA_NOPE:].transpose(1, 0, 2).astype(BF16)
    return pl.pallas_call(
        functools.partial(_mla_post_kernel, sb=sb),
        grid=(nseq // sb,),
        in_specs=[pl.BlockSpec((sb, nrow, MLA_KV_LORA), lambda i: (i, 0, 0)),
                  pl.BlockSpec((sb * ROWS_S, BLK), lambda i: (i, B_MLAZ)),
                  pl.BlockSpec((MLA_HEADS, MLA_KV_LORA, MLA_V), lambda i: (0, 0, 0))],
        out_specs=pl.BlockSpec((sb * ROWS_S, BLK), lambda i: (i, 0)),
        out_shape=jax.ShapeDtypeStruct((nseq * ROWS_S, BLK), BF16),
        compiler_params=_cparams(("arbitrary",)),
        name="mla_post",
    )(o_lat, proj, wuv)


def _outproj_kernel(b0_ref, b1_ref, b2_ref, b3_ref, x_ref, gate_ref, w_ref, y_ref):
    mixed = _bdot(b0_ref[...], w_ref[0:BLK, :])
    for k, br in enumerate((b1_ref, b2_ref, b3_ref), start=1):
        mixed = mixed + _bdot(br[...], w_ref[k * BLK:(k + 1) * BLK, :])
    y_ref[...] = x_ref[...] + gate_ref[0] * mixed.reshape(x_ref.shape)


def _out_proj(branches, x, mod4, w, *, li, nb, rows, mod_row0):
    d = x.shape[-1]
    tm = nb * rows
    x_map, m_row, grid_m = _row_maps(x.shape, nb, rows, mod_row0)
    br = pl.BlockSpec((tm, BLK), lambda i: (i, 0))
    return pl.pallas_call(
        _outproj_kernel,
        grid=(grid_m,),
        in_specs=[br, br, br, br, pl.BlockSpec((nb, rows, d), x_map),
                  pl.BlockSpec((1, nb, 1, d), lambda i: (li, m_row(i), 0, 2)),
                  pl.BlockSpec((d, d), lambda i: (0, 0))],
        out_specs=pl.BlockSpec((nb, rows, d), x_map),
        out_shape=jax.ShapeDtypeStruct(x.shape, F32),
        compiler_params=_cparams(("arbitrary",)),
        name="out_proj",
    )(*[b.reshape(-1, BLK) for b in branches], x, mod4, w)


def _small_arrays(proj3, nch, L):
    nbatch = proj3.shape[0]
    small = proj3[:, :, B_MLAQ * BLK + SMALL_OFF:B_MLAQ * BLK + SMALL_OFF + 16]
    small_t = small.reshape(nbatch, nch, L, 16).transpose(0, 1, 3, 2)
    return small, small_t


def _pad_rows(a):
    return jnp.pad(a, ((0, 0), (8 - (CONV_K - 1), 0), (0, 0)))


def _s5_state_out(hl):
    nt, nb, _ = hl.shape
    h = hl.reshape(nt, nb, S5_GT, 2, S5_STATE).transpose(1, 0, 2, 4, 3)
    return h.reshape(nb, S5_GROUPS, S5_STATE, 2)


def _layer_prompt(x, mod4, p, w_in, w_out, tabs, li, mod_row0):
    nbatch, seq, _ = x.shape
    t = nbatch * seq
    tm = min(512, seq)
    proj = _in_proj(x, mod4, p['norm_g'], w_in, li=li, nb=1, rows=min(1024, seq), mod_row0=mod_row0)
    p3 = proj.reshape(nbatch, seq, NCOL)
    nch5 = seq // S5_CHUNK
    nt = S5_GROUPS // S5_GT
    u_g = proj[:, :BLK].reshape(nbatch * nch5, S5_CHUNK, nt, 128).transpose(2, 0, 1, 3)
    u_g = u_g.reshape(nt, nbatch * nch5, S5_CHUNK * 128).astype(BF16)
    h0 = jnp.zeros((nt, 8, S5_LANES), F32)
    y5, hl5 = _s5_core(u_g, h0, tabs['s5_prompt'], nbatch=nbatch, nch=nch5, scan=True)
    y5 = y5.reshape(nt, nbatch * nch5, S5_CHUNK, 128).transpose(1, 2, 0, 3).reshape(t, BLK)
    s5_y = _s5_post(y5, proj, p, tm=tm)
    s5_h = _s5_state_out(hl5)
    nch = seq // SSD_CHUNK
    small, small_t = _small_arrays(p3, nch, SSD_CHUNK)
    ssd_y, ssd_h = _ssd(p3, small, small_t, jnp.zeros((nbatch, 8, SSD_CONV_DIM), F32),
                        jnp.zeros((1, nbatch, SSD_HEADS, SSD_HEADDIM, SSD_STATE), F32), p,
                        li=0, nb=nbatch, nch=nch, L=SSD_CHUNK, nvalid=SSD_CHUNK)
    ssd_buf = p3[:, seq - (CONV_K - 1):, B_SSDX * BLK:B_SSDX * BLK + SSD_CONV_DIM]
    q, lat, kpe, k, v = _mla_pre(proj, tabs['cos_p'], tabs['sin_p'], p, tm=tm, pos_blocks=seq // tm, expand_kv=True)
    mla_y = _flash(q, k, v, proj, nbatch=nbatch, seq=seq, tq=tm, tk=min(1024, seq))
    nchd = seq // DN_CHUNK
    small, small_t = _small_arrays(p3, nchd, DN_CHUNK)
    dn_y, dn_s = _dn(p3, small, small_t, jnp.zeros((nbatch, 8, DN_CONV_DIM), F32),
                     jnp.zeros((1, nbatch, DN_HEADS, DN_DK, DN_DV), F32), p,
                     li=0, nb=nbatch, nch=nchd, L=DN_CHUNK, nvalid=DN_CHUNK)
    dn_buf = p3[:, seq - (CONV_K - 1):, B_DNQ * BLK:B_DNQ * BLK + DN_CONV_DIM]
    y = _out_proj((s5_y, ssd_y, mla_y, dn_y), x, mod4, w_out, li=li, nb=1, rows=tm, mod_row0=mod_row0)
    states = (lat.reshape(nbatch, seq, -1), kpe.reshape(nbatch, seq, -1), s5_h, ssd_h, ssd_buf, dn_s, dn_buf)
    return y, states


def _layer_sample(x, mod4, p, w_in, w_out, tabs, li, st, caches, page_table, td):
    nseq = x.shape[0]
    t = nseq * ROWS_S
    sb = min(32, nseq)
    tm = sb * ROWS_S
    nb = min(8, nseq)
    s5_h0, ssd_h0, ssd_buf, dn_s0, dn_buf = st
    proj = _in_proj(x, mod4, p['norm_g'], w_in, li=li, nb=sb, rows=ROWS_S, mod_row0=0)
    p3 = proj.reshape(nseq, ROWS_S, NCOL)
    nt = S5_GROUPS // S5_GT
    u_g = p3[:, :td, :BLK].reshape(nseq, td, nt, 128).transpose(2, 0, 1, 3)
    u_g = u_g.reshape(nt, nseq, td * 128).astype(BF16)
    h0 = jnp.concatenate([s5_h0[li, ..., 0], s5_h0[li, ..., 1]], axis=-1)
    h0 = h0.reshape(nseq, nt, S5_LANES).transpose(1, 0, 2)
    y5, hl5 = _s5_core(u_g, h0, tabs['s5_sample'], nbatch=nseq, nch=1, scan=False)
    y5 = y5.reshape(nt, nseq, td, 128).transpose(1, 2, 0, 3).reshape(nseq, td, BLK)
    y5 = jnp.pad(y5, ((0, 0), (0, ROWS_S - td), (0, 0))).reshape(t, BLK)
    s5_y = _s5_post(y5, proj, p, tm=tm)
    s5_h = _s5_state_out(hl5)
    small, small_t = _small_arrays(p3, 1, ROWS_S)
    ssd_y, ssd_h = _ssd(p3, small, small_t, _pad_rows(ssd_buf[li]), ssd_h0, p,
                        li=li, nb=nb, nch=1, L=ROWS_S, nvalid=td)
    ssd_buf_new = p3[:, td - (CONV_K - 1):td, B_SSDX * BLK:B_SSDX * BLK + SSD_CONV_DIM]
    dn_y, dn_s = _dn(p3, small, small_t, _pad_rows(dn_buf[li]), dn_s0, p,
                     li=li, nb=nb, nch=1, L=ROWS_S, nvalid=td)
    dn_buf_new = p3[:, td - (CONV_K - 1):td, B_DNQ * BLK:B_DNQ * BLK + DN_CONV_DIM]
    q, lat, kpe = _mla_pre(proj, tabs['cos_s'], tabs['sin_s'], p, tm=tm, pos_blocks=1, expand_kv=False)
    n_pages = page_table.shape[1]
    pps = min(32, n_pages // 2)
    o_lat = _decode(page_table, q, lat, kpe, caches[0], caches[1], p, li=li, pps=pps, ts=min(1024, pps * PAGE_SIZE))
    mla_y = _mla_post(o_lat, proj, p, sb=sb)
    y = _out_proj((s5_y, ssd_y, mla_y, dn_y), x, mod4, w_out, li=li, nb=sb, rows=ROWS_S, mod_row0=0)
    lat3 = lat.reshape(nseq, ROWS_S, -1)[:, :td]
    kpe3 = kpe.reshape(nseq, ROWS_S, -1)[:, :td]
    return y, (lat3, kpe3, s5_h, ssd_h, ssd_buf_new, dn_s, dn_buf_new)


def kernel(x_prompt, x_sample, c_prompt, c_sample, cache_kv_latent, cache_k_rope, state_s5, state_ssd, state_ssd_conv, state_dn, state_dn_conv, page_table, norm_g, ada_w, ada_b, w_in, w_out, s5_lam_re, s5_lam_im, s5_log_dt, s5_b_re, s5_b_im, s5_c_re, s5_c_im, s5_d, s5_glu_w, s5_glu_b, ssd_conv_w, ssd_conv_b, ssd_dt_bias, ssd_a_log, ssd_d, ssd_norm_g, mla_q_norm_g, mla_kv_norm_g, mla_w_uq, mla_w_ukv, mla_q_g, mla_k_g, dn_conv_w, dn_a_log, dn_dt_bias, dn_norm_g):
    weights = dict(
        norm_g=norm_g, s5_lam_re=s5_lam_re, s5_lam_im=s5_lam_im, s5_log_dt=s5_log_dt,
        s5_b_re=s5_b_re, s5_b_im=s5_b_im, s5_c_re=s5_c_re, s5_c_im=s5_c_im,
        s5_d=s5_d, s5_glu_w=s5_glu_w, s5_glu_b=s5_glu_b,
        ssd_conv_w=ssd_conv_w, ssd_conv_b=ssd_conv_b, ssd_dt_bias=ssd_dt_bias,
        ssd_a_log=ssd_a_log, ssd_d=ssd_d, ssd_norm_g=ssd_norm_g,
        mla_q_norm_g=mla_q_norm_g, mla_kv_norm_g=mla_kv_norm_g, mla_w_uq=mla_w_uq,
        mla_w_ukv=mla_w_ukv, mla_q_g=mla_q_g, mla_k_g=mla_k_g,
        dn_conv_w=dn_conv_w, dn_a_log=dn_a_log, dn_dt_bias=dn_dt_bias, dn_norm_g=dn_norm_g)
    depth = w_in.shape[0]
    bp, tp, d = x_prompt.shape
    nseq, td, _ = x_sample.shape
    past_len = page_table.shape[1] * PAGE_SIZE

    c_all = jnp.concatenate([c_sample, c_prompt, jnp.zeros((8 - bp, d), F32)], axis=0)
    mod4 = _ada(c_all, ada_w, ada_b).reshape(depth, nseq + 8, 1, 3 * d)

    cos_p, sin_p = _rope_tables(jnp.arange(tp, dtype=F32))
    cos_s, sin_s = _rope_tables(jnp.arange(ROWS_S, dtype=F32) + past_len)
    reps = min(32, nseq)
    tabs_pos = dict(cos_p=cos_p, sin_p=sin_p, cos_s=jnp.tile(cos_s, (reps, 1)), sin_s=jnp.tile(sin_s, (reps, 1)))
    caches = (cache_kv_latent, jnp.swapaxes(cache_k_rope, 2, 3))
    st = (state_s5, state_ssd, state_ssd_conv, state_dn, state_dn_conv)

    y_p = x_prompt
    y_s = jnp.pad(x_sample, ((0, 0), (0, ROWS_S - td), (0, 0)))
    p_states, s_states = [], []
    nsteps = max(1, int(math.log2(tp // S5_CHUNK)))
    for li in range(depth):
        p = {name: w[li] for name, w in weights.items()}
        w_in_l = _prep_w_in(w_in[li])
        w_out_l = w_out[li].astype(BF16)
        tabs = dict(tabs_pos, s5_prompt=_s5_tables(p, S5_CHUNK, nsteps), s5_sample=_s5_tables(p, td, 1))
        y_p, st_p = _layer_prompt(y_p, mod4, p, w_in_l, w_out_l, tabs, li, nseq)
        p_states.append(st_p)
        y_s, st_s = _layer_sample(y_s, mod4, p, w_in_l, w_out_l, tabs, li, st, caches, page_table, td)
        s_states.append(st_s)

    stack = lambda states, i: jnp.stack([s[i] for s in states], axis=0)
    return ((y_p, y_s[:, :td])
            + tuple(stack(p_states, i) for i in range(7))
            + tuple(stack(s_states, i) for i in range(7)))
```

```python
import functools
import math

import numpy as np
import jax
import jax.numpy as jnp
from jax import lax
from jax.experimental import pallas as pl
from jax.experimental.pallas import tpu as pltpu

F32 = jnp.float32
BF16 = jnp.bfloat16
EPS = 1e-6

D_MODEL = 2048
BRANCH = 512
CONV_K = 4
S5_CH = 16
S5_GROUPS = 32
S5_STATE = 64
S5_CHUNK = 8
SSD_HEADDIM = 64
SSD_HEADS = 8
SSD_GROUPS = 2
SSD_STATE = 128
SSD_CHUNK = 128
SSD_CONV_DIM = BRANCH + 2 * SSD_GROUPS * SSD_STATE
MLA_NOPE = 128
MLA_ROPE = 64
MLA_QK = MLA_NOPE + MLA_ROPE
MLA_V = 128
MLA_HEADS = 4
MLA_Q_LORA = 384
MLA_KV_LORA = 256
ROPE_THETA = 10000.0
DN_DK = 128
DN_DV = 128
DN_HEADS = 4
DN_CHUNK = 64
DN_CONV_DIM = 2 * DN_HEADS * DN_DK + DN_HEADS * DN_DV
PAGE_SIZE = 128

IN_SIZES = (BRANCH, BRANCH, BRANCH, SSD_CONV_DIM, SSD_HEADS, MLA_Q_LORA, MLA_KV_LORA, MLA_ROPE, BRANCH,
            DN_CONV_DIM, BRANCH, DN_HEADS, DN_HEADS)

ROWS_S = 8
BLK = 512
NBLK = 12
NCOL = NBLK * BLK
B_S5U, B_S5Z, B_SSDZ, B_SSDX, B_SSDBC, B_MLAZ, B_DNQ, B_DNK, B_DNV, B_DNZ, B_MLAQ, B_MLAKV = range(NBLK)
SMALL_OFF = MLA_Q_LORA + MLA_ROPE
NEG = -1e30
VMEM_LIMIT = 52 * 1024 * 1024


def _cparams(sem, vmem=VMEM_LIMIT):
    return pltpu.CompilerParams(dimension_semantics=sem, vmem_limit_bytes=vmem)


def _bdot(a, b):
    return jnp.dot(a.astype(BF16), b.astype(BF16), preferred_element_type=F32)


def _nt(a, b):
    return lax.dot_general(a.astype(BF16), b.astype(BF16), (((1,), (1,)), ((), ())), preferred_element_type=F32)


def _tn(a, b):
    return lax.dot_general(a.astype(BF16), b.astype(BF16), (((0,), (0,)), ((), ())), preferred_element_type=F32)


def _split(x, n):
    parts = []
    r = x
    for _ in range(n):
        p = r.astype(BF16)
        parts.append(p)
        r = r - p.astype(F32)
    return parts


def _dot_exact_l(t01, x):
    t = t01.astype(BF16)
    return sum(jnp.dot(t, p, preferred_element_type=F32) for p in _split(x, 3))


def _dot_exact_r(x, t01):
    t = t01.astype(BF16)
    return sum(jnp.dot(p, t, preferred_element_type=F32) for p in _split(x, 3))


def _dot3(a, b):
    a1, a2 = _split(a, 2)
    b1, b2 = _split(b, 2)
    d = functools.partial(jnp.dot, preferred_element_type=F32)
    return d(a1, b1) + (d(a1, b2) + d(a2, b1))


def _silu(x):
    return x * jax.nn.sigmoid(x)


def _iota(shape, dim):
    return lax.broadcasted_iota(jnp.int32, shape, dim)


def _ada_kernel(c_ref, w_ref, b_ref, o_ref):
    o_ref[0] = _bdot(_silu(c_ref[...]), w_ref[0]) + b_ref[0]


def _ada(c_all, ada_w, ada_b):
    depth, d, n3 = ada_w.shape
    rows = c_all.shape[0]
    tn = 1024
    return pl.pallas_call(
        _ada_kernel,
        grid=(depth, n3 // tn),
        in_specs=[pl.BlockSpec((rows, d), lambda l, j: (0, 0)),
                  pl.BlockSpec((1, d, tn), lambda l, j: (l, 0, j)),
                  pl.BlockSpec((1, 1, tn), lambda l, j: (l, 0, j))],
        out_specs=pl.BlockSpec((1, rows, tn), lambda l, j: (l, 0, j)),
        out_shape=jax.ShapeDtypeStruct((depth, rows, n3), F32),
        compiler_params=_cparams(("arbitrary", "arbitrary")),
        name="ada",
    )(c_all, ada_w, ada_b.reshape(depth, 1, n3))


def _inproj_kernel(x_ref, mod_ref, g_ref, w_ref, o_ref, h_ref):
    @pl.when(pl.program_id(1) == 0)
    def _():
        x = x_ref[...]
        xn = x * lax.rsqrt(jnp.mean(x * x, axis=-1, keepdims=True) + EPS) * g_ref[...]
        mod = mod_ref[0]
        h = xn * (1.0 + mod[..., D_MODEL:]) + mod[..., :D_MODEL]
        h_ref[...] = h.reshape(-1, D_MODEL).astype(BF16)

    o_ref[...] = jnp.dot(h_ref[...], w_ref[...], preferred_element_type=F32)


def _row_maps(x_shape, nb, rows, mod_row0):
    per = x_shape[1] // rows
    if nb == 1:
        x_map = lambda i: (i // per, i % per, 0)
        m_row = lambda i: mod_row0 + i // per
    else:
        x_map = lambda i: (i, 0, 0)
        m_row = lambda i: i
    return x_map, m_row, (x_shape[0] // nb) * per


def _in_proj(x, mod4, norm_g, w, *, li, nb, rows, mod_row0):
    d = x.shape[-1]
    tm = nb * rows
    tn = 1536 if tm <= 512 else 768
    x_map, m_row, grid_m = _row_maps(x.shape, nb, rows, mod_row0)
    return pl.pallas_call(
        _inproj_kernel,
        grid=(grid_m, NCOL // tn),
        in_specs=[pl.BlockSpec((nb, rows, d), lambda i, j: x_map(i)),
                  pl.BlockSpec((1, nb, 1, 2 * d), lambda i, j: (li, m_row(i), 0, 0)),
                  pl.BlockSpec((1, d), lambda i, j: (0, 0)),
                  pl.BlockSpec((d, tn), lambda i, j: (0, j))],
        out_specs=pl.BlockSpec((tm, tn), lambda i, j: (i, j)),
        out_shape=jax.ShapeDtypeStruct((x.shape[0] * x.shape[1], NCOL), F32),
        scratch_shapes=[pltpu.VMEM((tm, d), BF16)],
        compiler_params=_cparams(("arbitrary", "arbitrary")),
        name="in_proj",
    )(x, mod4, norm_g.reshape(1, d), w)


def _prep_w_in(w):
    o = np.concatenate([[0], np.cumsum(IN_SIZES)])
    s5_u, s5_z, ssd_z, xbc, dt, cq, ckv, kpe, mla_z, qkv, dn_z, dn_a, dn_b = [
        w[:, int(o[i]):int(o[i + 1])] for i in range(len(IN_SIZES))]
    half = MLA_ROPE // 2
    partner = jnp.concatenate([-kpe[:, half:], kpe[:, :half]], axis=1)
    zeros = lambda n: jnp.zeros((w.shape[0], n), w.dtype)
    cols = [s5_u, s5_z, ssd_z, xbc, mla_z, qkv, dn_z,
            cq, kpe, dt, dn_a, dn_b, zeros(BLK - SMALL_OFF - 16),
            ckv, partner, zeros(BLK - MLA_KV_LORA - MLA_ROPE)]
    return jnp.concatenate(cols, axis=1).astype(BF16)


S5_GT = 8
S5_LANES = S5_GT * 2 * S5_STATE


def _cmul(x, tr, ti):
    n = x.shape[1]
    lo = _iota(x.shape, 1) % (2 * S5_STATE) < S5_STATE
    swapped = jnp.where(lo, pltpu.roll(x, n - S5_STATE, axis=1), pltpu.roll(x, S5_STATE, axis=1))
    return x * tr + swapped * ti


def _s5_kernel(u_ref, toep_ref, w_ref, v_ref, h0_ref, tr_ref, ti_ref, y_ref, hl_ref, *, nbatch, nch, scan):
    u = u_ref[0]
    s_in = jnp.dot(u, w_ref[0], preferred_element_type=F32)
    tr = tr_ref[0]
    ti = ti_ref[0]
    if scan:
        rows = _iota((nch, S5_LANES), 0)
        h_in, h_last = [], []
        for b in range(nbatch):
            x = s_in[b * nch:(b + 1) * nch]
            for k in range(int(math.log2(nch))):
                d = 1 << k
                xs = jnp.where(rows >= d, pltpu.roll(x, d, axis=0), 0.0)
                x = x + _cmul(xs, tr[k:k + 1], ti[k:k + 1])
            h_in.append(jnp.where(rows >= 1, pltpu.roll(x, 1, axis=0), 0.0))
            h_last.append(x[nch - 1:nch])
        h_in = jnp.concatenate(h_in, axis=0)
        hl_ref[0] = jnp.concatenate(h_last, axis=0)
    else:
        h_in = h0_ref[0]
        hl_ref[0] = _cmul(h_in, tr[0:1], ti[0:1]) + s_in
    y_ref[0] = jnp.dot(u, toep_ref[0], preferred_element_type=F32) + _bdot(h_in, v_ref[0])


def _s5_tables(p, L, nsteps):
    lam_r, lam_i = p['s5_lam_re'], p['s5_lam_im']
    delta = jnp.exp(p['s5_log_dt'])[:, None]
    lr, li = lam_r * delta, lam_i * delta

    def apow(k):
        k = jnp.asarray(k, F32)
        e = jnp.exp(lr[:, None, :] * k[None, :, None])
        return e * jnp.cos(li[:, None, :] * k[None, :, None]), e * jnp.sin(li[:, None, :] * k[None, :, None])

    a_r, a_i = apow(jnp.ones((1,), F32))
    a_r, a_i = a_r[:, 0], a_i[:, 0]
    den = lam_r * lam_r + lam_i * lam_i
    q_r = ((a_r - 1.0) * lam_r + a_i * lam_i) / den
    q_i = (a_i * lam_r - (a_r - 1.0) * lam_i) / den
    b_r = q_r[..., None] * p['s5_b_re'] - q_i[..., None] * p['s5_b_im']
    b_i = q_r[..., None] * p['s5_b_im'] + q_i[..., None] * p['s5_b_re']
    c_r, c_i = p['s5_c_re'], p['s5_c_im']

    pk_r, pk_i = apow(np.arange(L + 1))
    d_r = pk_r[:, :L, :, None] * b_r[:, None] - pk_i[:, :L, :, None] * b_i[:, None]
    d_i = pk_r[:, :L, :, None] * b_i[:, None] + pk_i[:, :L, :, None] * b_r[:, None]
    kern = jnp.einsum('gcp,gtpd->gdtc', c_r, d_r) - jnp.einsum('gcp,gtpd->gdtc', c_i, d_i)
    toep = jnp.stack([jnp.pad(kern[:, :, :L - s, :], ((0, 0), (0, 0), (s, 0), (0, 0))) for s in range(L)], axis=1)
    toep = toep.reshape(S5_GROUPS, L * S5_CH, L * S5_CH)
    rev = L - 1 - np.arange(L)
    bt_r, bt_i = jnp.swapaxes(b_r, 1, 2)[:, None], jnp.swapaxes(b_i, 1, 2)[:, None]
    w_r = pk_r[:, rev][:, :, None, :] * bt_r - pk_i[:, rev][:, :, None, :] * bt_i
    w_i = pk_r[:, rev][:, :, None, :] * bt_i + pk_i[:, rev][:, :, None, :] * bt_r
    w_mat = jnp.concatenate([w_r, w_i], axis=-1).reshape(S5_GROUPS, L * S5_CH, 2 * S5_STATE)
    ct_r, ct_i = jnp.swapaxes(c_r, 1, 2)[:, :, None, :], jnp.swapaxes(c_i, 1, 2)[:, :, None, :]
    pt_r, pt_i = jnp.swapaxes(pk_r, 1, 2)[:, :, 1:, None], jnp.swapaxes(pk_i, 1, 2)[:, :, 1:, None]
    v_r = ct_r * pt_r - ct_i * pt_i
    v_i = ct_r * pt_i + ct_i * pt_r
    v_mat = jnp.concatenate([v_r, -v_i], axis=1).reshape(S5_GROUPS, 2 * S5_STATE, L * S5_CH)
    s_r, s_i = apow((2.0 ** np.arange(nsteps)) * L)
    tr = jnp.concatenate([s_r, s_r], axis=-1)
    ti = jnp.concatenate([-s_i, s_i], axis=-1)
    nt = S5_GROUPS // S5_GT
    lc = L * S5_CH
    pc = np.zeros((S5_GT, lc, L * S5_GT * S5_CH), np.float32)
    pq = np.zeros((S5_GT, 2 * S5_STATE, S5_LANES), np.float32)
    for g in range(S5_GT):
        for s in range(L):
            for c in range(S5_CH):
                pc[g, s * S5_CH + c, (s * S5_GT + g) * S5_CH + c] = 1.0
        pq[g, np.arange(2 * S5_STATE), g * 2 * S5_STATE + np.arange(2 * S5_STATE)] = 1.0
    place = lambda rows, m, cols: jnp.einsum('gki,agkl,glj->aij', rows, m, cols)
    toep = place(pc, toep.reshape(nt, S5_GT, lc, lc), pc)
    w_mat = place(pc, w_mat.reshape(nt, S5_GT, lc, 2 * S5_STATE), pq)
    v_mat = place(pq, v_mat.reshape(nt, S5_GT, 2 * S5_STATE, lc), pc)
    lanes = lambda a: a.reshape(nt, S5_GT, nsteps, 2 * S5_STATE).transpose(0, 2, 1, 3).reshape(nt, nsteps, S5_LANES)
    return toep.astype(BF16), w_mat.astype(BF16), v_mat.astype(BF16), lanes(tr), lanes(ti)


def _s5_core(u_g, h0_g, tables, *, nbatch, nch, scan):
    toep, w_mat, v_mat, tr, ti = tables
    g, r, lc = u_g.shape
    nk = tr.shape[1]
    nb_out = nbatch if scan else r
    blk = lambda shape: pl.BlockSpec((1,) + shape, lambda i: (i, 0, 0))
    kern = functools.partial(_s5_kernel, nbatch=nbatch, nch=nch, scan=scan)
    return pl.pallas_call(
        kern,
        grid=(g,),
        in_specs=[blk((r, lc)), blk((lc, lc)), blk((lc, S5_LANES)), blk((S5_LANES, lc)),
                  blk(h0_g.shape[1:]), blk((nk, S5_LANES)), blk((nk, S5_LANES))],
        out_specs=[blk((r, lc)), blk((nb_out, S5_LANES))],
        out_shape=[jax.ShapeDtypeStruct((g, r, lc), F32), jax.ShapeDtypeStruct((g, nb_out, S5_LANES), F32)],
        compiler_params=_cparams(("arbitrary",)),
        name="s5_core",
    )(u_g, toep, w_mat, v_mat, h0_g, tr, ti)


def _s5_post_kernel(yc_ref, u_ref, z_ref, d_ref, w_ref, b_ref, o_ref):
    y = jax.nn.gelu(yc_ref[...] + d_ref[...] * u_ref[...])
    y = y * jax.nn.sigmoid(_bdot(y, w_ref[...]) + b_ref[...])
    o_ref[...] = (y * _silu(z_ref[...])).astype(BF16)


def _s5_post(ycore, proj, p, *, tm):
    t = ycore.shape[0]
    row = lambda blk: pl.BlockSpec((tm, BLK), lambda i, blk=blk: (i, blk))
    full = lambda shape: pl.BlockSpec(shape, lambda i: (0, 0))
    return pl.pallas_call(
        _s5_post_kernel,
        grid=(t // tm,),
        in_specs=[row(0), row(B_S5U), row(B_S5Z), full((1, BLK)), full((BLK, BLK)), full((1, BLK))],
        out_specs=row(0),
        out_shape=jax.ShapeDtypeStruct((t, BLK), BF16),
        compiler_params=_cparams(("arbitrary",)),
        name="s5_post",
    )(ycore, proj, proj, p['s5_d'].reshape(1, BLK), p['s5_glu_w'].astype(BF16), p['s5_glu_b'].reshape(1, BLK))


def _causal_conv(tail, cur, w_ref, L):
    ext = jnp.concatenate([tail, cur], axis=0)
    base = 8 - (CONV_K - 1)
    out = ext[base:base + L] * w_ref[0:1, :]
    for j in range(1, CONV_K):
        out = out + ext[base + j:base + j + L] * w_ref[j:j + 1, :]
    return out


def _tri(L, strict=False):
    r = _iota((L, L), 0)
    c = _iota((L, L), 1)
    return (r > c) if strict else (r >= c)


def _pair_select(L, a, b):
    return jnp.where(_iota((L, 128), 1) < 64, a, b)


def _seq_specs(nb, L, nch, li, hd, conv_dim, blocks):
    row = lambda blk: pl.BlockSpec((nb, L, BLK), lambda i, c, blk=blk: (i, c, blk))
    specs = [row(b) for b in blocks]
    specs += [pl.BlockSpec((nb, L, 16), lambda i, c: (i, c, 0)),
              pl.BlockSpec((nb, 1, 16, L), lambda i, c: (i, c, 0, 0)),
              pl.BlockSpec((nb, 8, conv_dim), lambda i, c: (i, 0, 0)),
              pl.BlockSpec((1, nb) + hd, lambda i, c: (li, i, 0, 0, 0))]
    out_specs = [pl.BlockSpec((nb, L, BLK), lambda i, c: (i, c, 0)),
                 pl.BlockSpec((nb,) + hd, lambda i, c: (i, 0, 0, 0))]
    return specs, out_specs


def _ssd_kernel(z_ref, x_ref, bc_ref, sm_ref, smt_ref, buf_ref, h0_ref,
                cw_ref, cb_ref, dtb_ref, dtbt_ref, al_ref, alt_ref, drow_ref, ng_ref,
                y_ref, hl_ref, tail_ref, st_ref, *, nb, L, nvalid):
    c = pl.program_id(1)

    @pl.when(c == 0)
    def _():
        tail_ref[...] = buf_ref[...]
        st_ref[...] = h0_ref[0]

    tri = _tri(L)
    triu = (_iota((L, L), 0) <= _iota((L, L), 1)).astype(F32)
    rows128 = _iota((128, SSD_STATE), 0)
    gn = SSD_GROUPS * SSD_STATE
    half = BRANCH // SSD_GROUPS
    lane_lo = _iota((L, 128), 1) < 64
    xs_, dtc_, cumc_, cumr_, bm_, cm_ = {}, {}, {}, {}, {}, {}
    for j in range(nb):
        xbc = jnp.concatenate([x_ref[j], bc_ref[j]], axis=1)
        conv = _silu(_causal_conv(tail_ref[j], xbc, cw_ref, L) + cb_ref[...])
        tail_ref[j] = xbc[L - 8:L]
        xs_[j] = conv[:, :BRANCH]
        dtc = jax.nn.softplus(sm_ref[j, :, 0:SSD_HEADS] + dtb_ref[...])
        dtr = jax.nn.softplus(smt_ref[j, 0, 0:SSD_HEADS, :] + dtbt_ref[...])
        if nvalid < L:
            dtc = jnp.where(_iota(dtc.shape, 0) < nvalid, dtc, 0.0)
            dtr = jnp.where(_iota(dtr.shape, 1) < nvalid, dtr, 0.0)
        dtc_[j] = dtc
        cumc_[j] = _dot_exact_l(tri.astype(F32), dtc * (-jnp.exp(al_ref[...])))
        cumr_[j] = _dot_exact_r(dtr * (-jnp.exp(alt_ref[...])), triu)
        for g in range(SSD_GROUPS):
            bm_[j, g] = conv[:, BRANCH + g * SSD_STATE:BRANCH + (g + 1) * SSD_STATE]
            cm_[j, g] = conv[:, BRANCH + gn + g * SSD_STATE:BRANCH + gn + (g + 1) * SSD_STATE]
    groups = [(j, g) for j in range(nb) for g in range(SSD_GROUPS)]
    pairs = [(j, g, pr) for (j, g) in groups for pr in range(2)]
    heads = lambda c_: (c_[1] * 4 + c_[2] * 2, c_[1] * 4 + c_[2] * 2 + 1)
    col = lambda a, h: a[:, h:h + 1]
    cb = {c_: _nt(cm_[c_], bm_[c_]) for c_ in groups}
    st = {c_: st_ref[c_[0], heads(c_)[0]:heads(c_)[0] + 2].reshape(2 * SSD_HEADDIM, SSD_STATE) for c_ in pairs}
    y_off = {c_: _nt(cm_[c_[:2]], st[c_]) for c_ in pairs}
    xdt, upd, y_diag = {}, {}, {}
    for c_ in pairs:
        j, (h0, h1) = c_[0], heads(c_)
        xdt[c_] = xs_[j][:, h0 * SSD_HEADDIM:(h0 + 2) * SSD_HEADDIM] * jnp.where(lane_lo, col(dtc_[j], h0), col(dtc_[j], h1))
    for c_ in pairs:
        j, (h0, h1) = c_[0], heads(c_)
        edec = jnp.exp(cumc_[j][L - 1:L] - cumc_[j])
        upd[c_] = _tn(xdt[c_] * jnp.where(lane_lo, col(edec, h0), col(edec, h1)), bm_[c_[:2]])
    for c_ in pairs:
        j, (h0, h1) = c_[0], heads(c_)
        yd = [_bdot(cb[c_[:2]] * jnp.exp(jnp.where(tri, col(cumc_[j], h) - cumr_[j][h:h + 1, :], NEG)), xdt[c_])
              for h in (h0, h1)]
        y_diag[c_] = jnp.where(lane_lo, yd[0], yd[1])
    for c_ in pairs:
        j, (h0, h1) = c_[0], heads(c_)
        etot = jnp.exp(cumc_[j][L - 1:L])
        st_new = st[c_] * jnp.where(rows128 < 64, col(etot, h0), col(etot, h1)) + upd[c_]
        st_ref[j, h0:h0 + 2] = st_new.reshape(2, SSD_HEADDIM, SSD_STATE)
    for j in range(nb):
        ecum = jnp.exp(cumc_[j])
        y_parts = []
        for g in range(SSD_GROUPS):
            for pr in range(2):
                h0, h1 = heads((j, g, pr))
                y_parts.append(y_diag[j, g, pr] + y_off[j, g, pr] * jnp.where(lane_lo, col(ecum, h0), col(ecum, h1)))
        xs = xs_[j]
        y = jnp.concatenate(y_parts, axis=1) + drow_ref[...] * xs
        y = y * _silu(z_ref[j])
        outs = []
        for g in range(SSD_GROUPS):
            yg = y[:, g * half:(g + 1) * half]
            outs.append(yg * lax.rsqrt(jnp.mean(yg * yg, axis=-1, keepdims=True) + EPS))
        y_ref[j] = (jnp.concatenate(outs, axis=1) * ng_ref[...]).astype(y_ref.dtype)

    @pl.when(c == pl.num_programs(1) - 1)
    def _():
        hl_ref[...] = st_ref[...]


def _ssd(proj3, small, small_t, buf8, h0, p, *, li, nb, nch, L, nvalid):
    nbatch, r, _ = proj3.shape
    full = lambda shape: pl.BlockSpec(shape, lambda i, c: (0,) * len(shape))
    hd = (SSD_HEADS, SSD_HEADDIM, SSD_STATE)
    specs, out_specs = _seq_specs(nb, L, nch, li, hd, SSD_CONV_DIM, (B_SSDZ, B_SSDX, B_SSDBC))
    kern = functools.partial(_ssd_kernel, nb=nb, L=L, nvalid=nvalid)
    return pl.pallas_call(
        kern,
        grid=(nbatch // nb, nch),
        in_specs=specs + [full((CONV_K, SSD_CONV_DIM)), full((1, SSD_CONV_DIM)),
                          full((1, SSD_HEADS)), full((SSD_HEADS, 1)), full((1, SSD_HEADS)), full((SSD_HEADS, 1)),
                          full((1, BRANCH)), full((1, BRANCH))],
        out_specs=out_specs,
        out_shape=[jax.ShapeDtypeStruct((nbatch, r, BLK), BF16 if L % 16 == 0 else F32),
                   jax.ShapeDtypeStruct((nbatch,) + hd, F32)],
        scratch_shapes=[pltpu.VMEM((nb, 8, SSD_CONV_DIM), F32), pltpu.VMEM((nb,) + hd, F32)],
        compiler_params=_cparams(("arbitrary", "arbitrary")),
        name="ssd",
    )(proj3, proj3, proj3, small, small_t, buf8, h0,
      p['ssd_conv_w'], p['ssd_conv_b'].reshape(1, -1),
      p['ssd_dt_bias'].reshape(1, -1), p['ssd_dt_bias'].reshape(-1, 1),
      p['ssd_a_log'].reshape(1, -1), p['ssd_a_log'].reshape(-1, 1),
      jnp.repeat(p['ssd_d'], SSD_HEADDIM).reshape(1, BRANCH), p['ssd_norm_g'].reshape(1, BRANCH))


def _dn_kernel(q_ref, k_ref, v_ref, z_ref, sm_ref, smt_ref, buf_ref, s0_ref,
               cw_ref, al_ref, alt_ref, dtb_ref, dtbt_ref, ng_ref,
               y_ref, sl_ref, tail_ref, st_ref, *, nb, L, nvalid):
    c = pl.program_id(1)

    @pl.when(c == 0)
    def _():
        tail_ref[...] = buf_ref[...]
        st_ref[...] = s0_ref[0]

    a_off = SSD_HEADS
    b_off = SSD_HEADS + DN_HEADS
    tri = _tri(L)
    stri = _tri(L, strict=True)
    triu = (_iota((L, L), 0) <= _iota((L, L), 1)).astype(F32)
    eye = (_iota((L, L), 0) == _iota((L, L), 1)).astype(F32)
    kd = DN_HEADS * DN_DK
    chains = [(j, h) for j in range(nb) for h in range(DN_HEADS)]
    qs, ks, vs, gcis, bcols, gammas = {}, {}, {}, {}, {}, {}
    for j in range(nb):
        qkv = jnp.concatenate([q_ref[j], k_ref[j], v_ref[j]], axis=1)
        conv = _silu(_causal_conv(tail_ref[j], qkv, cw_ref, L))
        tail_ref[j] = qkv[L - 8:L]

        gc = -jnp.exp(al_ref[...]) * jax.nn.softplus(sm_ref[j, :, a_off:a_off + DN_HEADS] + dtb_ref[...])
        gr = -jnp.exp(alt_ref[...]) * jax.nn.softplus(smt_ref[j, 0, a_off:a_off + DN_HEADS, :] + dtbt_ref[...])
        beta = jax.nn.sigmoid(sm_ref[j, :, b_off:b_off + DN_HEADS])
        if nvalid < L:
            vc = _iota(gc.shape, 0) < nvalid
            gc = jnp.where(vc, gc, 0.0)
            beta = jnp.where(vc, beta, 0.0)
            gr = jnp.where(_iota(gr.shape, 1) < nvalid, gr, 0.0)
        gcc = _dot_exact_l(tri.astype(F32), gc)
        gcr = _dot_exact_r(gr, triu)
        for h in range(DN_HEADS):
            qh = conv[:, h * DN_DK:(h + 1) * DN_DK]
            kh = conv[:, kd + h * DN_DK:kd + (h + 1) * DN_DK]
            qs[j, h] = qh * lax.rsqrt(jnp.sum(qh * qh, axis=-1, keepdims=True) + EPS) * (DN_DK ** -0.5)
            ks[j, h] = kh * lax.rsqrt(jnp.sum(kh * kh, axis=-1, keepdims=True) + EPS)
            vs[j, h] = conv[:, 2 * kd + h * DN_DV:2 * kd + (h + 1) * DN_DV]
            gcis[j, h] = gcc[:, h:h + 1]
            bcols[j, h] = beta[:, h:h + 1]
            gammas[j, h] = jnp.exp(jnp.where(tri, gcc[:, h:h + 1] - gcr[h:h + 1, :], NEG))

    kk = {c_: _nt(ks[c_], ks[c_]) for c_ in chains}
    qk = {c_: _nt(qs[c_], ks[c_]) for c_ in chains}
    pw = {c_: jnp.where(stri, -(bcols[c_] * kk[c_] * gammas[c_]), 0.0) for c_ in chains}
    inv = {c_: eye + pw[c_] for c_ in chains}
    for _ in range(int(math.log2(L)) - 1):
        pw = {c_: _dot3(pw[c_], pw[c_]) for c_ in chains}
        inv = {c_: inv[c_] + _dot3(inv[c_], pw[c_]) for c_ in chains}
    sol = {c_: _dot3(inv[c_], jnp.concatenate([vs[c_] * bcols[c_], ks[c_] * (bcols[c_] * jnp.exp(gcis[c_]))], axis=1))
           for c_ in chains}
    st = {c_: st_ref[c_[0], c_[1]] for c_ in chains}
    v_new = {c_: sol[c_][:, :DN_DV] - _bdot(sol[c_][:, DN_DV:], st[c_]) for c_ in chains}
    o_s = {c_: _bdot(qs[c_] * jnp.exp(gcis[c_]), st[c_]) for c_ in chains}
    o_a = {c_: _bdot(jnp.where(tri, qk[c_] * gammas[c_], 0.0), v_new[c_]) for c_ in chains}
    upd = {c_: _tn(ks[c_] * jnp.exp(gcis[c_][L - 1:L] - gcis[c_]), v_new[c_]) for c_ in chains}
    for c_ in chains:
        st_ref[c_[0], c_[1]] = st[c_] * jnp.exp(gcis[c_][L - 1:L]) + upd[c_]
    for j in range(nb):
        outs = []
        for h in range(DN_HEADS):
            o = o_s[j, h] + o_a[j, h]
            o = o * lax.rsqrt(jnp.mean(o * o, axis=-1, keepdims=True) + EPS) * ng_ref[...]
            outs.append(o * _silu(z_ref[j, :, h * DN_DV:(h + 1) * DN_DV]))
        y_ref[j] = jnp.concatenate(outs, axis=1).astype(y_ref.dtype)

    @pl.when(c == pl.num_programs(1) - 1)
    def _():
        sl_ref[...] = st_ref[...]


def _dn(proj3, small, small_t, buf8, s0, p, *, li, nb, nch, L, nvalid):
    nbatch, r, _ = proj3.shape
    full = lambda shape: pl.BlockSpec(shape, lambda i, c: (0,) * len(shape))
    hd = (DN_HEADS, DN_DK, DN_DV)
    specs, out_specs = _seq_specs(nb, L, nch, li, hd, DN_CONV_DIM, (B_DNQ, B_DNK, B_DNV, B_DNZ))
    kern = functools.partial(_dn_kernel, nb=nb, L=L, nvalid=nvalid)
    return pl.pallas_call(
        kern,
        grid=(nbatch // nb, nch),
        in_specs=specs + [full((CONV_K, DN_CONV_DIM)),
                          full((1, DN_HEADS)), full((DN_HEADS, 1)), full((1, DN_HEADS)), full((DN_HEADS, 1)),
                          full((1, DN_DV))],
        out_specs=out_specs,
        out_shape=[jax.ShapeDtypeStruct((nbatch, r, BLK), BF16 if L % 16 == 0 else F32),
                   jax.ShapeDtypeStruct((nbatch,) + hd, F32)],
        scratch_shapes=[pltpu.VMEM((nb, 8, DN_CONV_DIM), F32), pltpu.VMEM((nb,) + hd, F32)],
        compiler_params=_cparams(("arbitrary", "arbitrary")),
        name="dn",
    )(proj3, proj3, proj3, proj3, small, small_t, buf8, s0,
      p['dn_conv_w'],
      p['dn_a_log'].reshape(1, -1), p['dn_a_log'].reshape(-1, 1),
      p['dn_dt_bias'].reshape(1, -1), p['dn_dt_bias'].reshape(-1, 1),
      p['dn_norm_g'].reshape(1, -1))


def _mla_pre_kernel(a_ref, b_ref, cos_ref, sin_ref, qng_ref, kvng_ref, wuq_ref, gq_ref, *rest, expand_kv):
    if expand_kv:
        wukv_ref, gk_ref, q_ref, lat_ref, kpe_ref, k_ref, v_ref = rest
    else:
        q_ref, lat_ref, kpe_ref = rest
    cos = cos_ref[...]
    sin = sin_ref[...]
    cq = a_ref[:, :MLA_Q_LORA]
    cqn = cq * lax.rsqrt(jnp.mean(cq * cq, axis=-1, keepdims=True) + EPS) * qng_ref[...]
    qp = _bdot(cqn, wuq_ref[...])
    for h in range(MLA_HEADS):
        nope = qp[:, h * 384:h * 384 + 128]
        ro = qp[:, h * 384 + 128:h * 384 + 256] * cos + qp[:, h * 384 + 256:h * 384 + 384] * sin
        ms = (jnp.sum(nope * nope, axis=-1, keepdims=True) + jnp.sum(ro * ro, axis=-1, keepdims=True)) / MLA_QK
        rinv = lax.rsqrt(ms + EPS)
        q_ref[:, h * 256:h * 256 + 128] = (nope * rinv * gq_ref[:, :128]).astype(q_ref.dtype)
        q_ref[:, h * 256 + 128:(h + 1) * 256] = (ro * rinv * gq_ref[:, 128:]).astype(q_ref.dtype)
    ckv = b_ref[:, :MLA_KV_LORA]
    lat = ckv * lax.rsqrt(jnp.mean(ckv * ckv, axis=-1, keepdims=True) + EPS) * kvng_ref[...]
    lat_ref[...] = lat
    kpe = a_ref[:, MLA_Q_LORA:] * cos + b_ref[:, MLA_KV_LORA:MLA_KV_LORA + 128] * sin
    kpe_ref[...] = kpe[:, :MLA_ROPE]
    if expand_kv:
        kv = _bdot(lat, wukv_ref[...])
        pe_sq = jnp.sum(kpe * kpe, axis=-1, keepdims=True)
        for h in range(MLA_HEADS):
            kn = kv[:, h * 256:h * 256 + 128]
            rinv = lax.rsqrt((jnp.sum(kn * kn, axis=-1, keepdims=True) + pe_sq) / MLA_QK + EPS)
            k_ref[:, h * 256:h * 256 + 128] = (kn * rinv * gk_ref[:, :128]).astype(BF16)
            k_ref[:, h * 256 + 128:(h + 1) * 256] = (kpe * rinv * gk_ref[:, 128:]).astype(BF16)
            v_ref[:, h * 256:h * 256 + 128] = kv[:, h * 256 + 128:(h + 1) * 256].astype(BF16)
            v_ref[:, h * 256 + 128:(h + 1) * 256] = jnp.ones((kn.shape[0], 128), BF16)


def _pad_gain(g, scale=1.0):
    return jnp.concatenate([g * scale, jnp.zeros((256 - MLA_QK,), F32)]).reshape(1, 256)


def _mla_pre(proj, cos, sin, p, *, tm, pos_blocks, expand_kv):
    t = proj.shape[0]
    half = MLA_ROPE // 2
    wq = p['mla_w_uq'].reshape(MLA_Q_LORA, MLA_HEADS, MLA_QK)
    nope, ro = wq[..., :MLA_NOPE], wq[..., MLA_NOPE:]
    zeros = jnp.zeros((MLA_Q_LORA, MLA_HEADS, 128 - MLA_ROPE), F32)
    partner = jnp.concatenate([-ro[..., half:], ro[..., :half]], axis=-1)
    wuq = jnp.concatenate([nope, ro, zeros, partner, zeros], axis=-1).reshape(MLA_Q_LORA, MLA_HEADS * 384).astype(BF16)
    row = lambda blk: pl.BlockSpec((tm, BLK), lambda i, blk=blk: (i, blk))
    tab = pl.BlockSpec((tm, 128), lambda i: (i % pos_blocks, 0))
    full = lambda shape: pl.BlockSpec(shape, lambda i: (0, 0))
    in_specs = [row(B_MLAQ), row(B_MLAKV), tab, tab, full((1, MLA_Q_LORA)), full((1, MLA_KV_LORA)),
                full((MLA_Q_LORA, MLA_HEADS * 384)), full((1, 256))]
    args = [proj, proj, cos, sin, p['mla_q_norm_g'].reshape(1, -1), p['mla_kv_norm_g'].reshape(1, -1),
            wuq, _pad_gain(p['mla_q_g'], MLA_QK ** -0.5)]
    out_specs = [pl.BlockSpec((tm, 1024), lambda i: (i, 0)), pl.BlockSpec((tm, MLA_KV_LORA), lambda i: (i, 0)),
                 pl.BlockSpec((tm, MLA_ROPE), lambda i: (i, 0))]
    out_shape = [jax.ShapeDtypeStruct((t, 1024), BF16 if expand_kv else F32),
                 jax.ShapeDtypeStruct((t, MLA_KV_LORA), F32), jax.ShapeDtypeStruct((t, MLA_ROPE), F32)]
    if expand_kv:
        in_specs += [full((MLA_KV_LORA, MLA_HEADS * 256)), full((1, 256))]
        args += [p['mla_w_ukv'].astype(BF16), _pad_gain(p['mla_k_g'])]
        out_specs += [pl.BlockSpec((tm, 1024), lambda i: (i, 0)), pl.BlockSpec((tm, 1024), lambda i: (i, 0))]
        out_shape += [jax.ShapeDtypeStruct((t, 1024), BF16), jax.ShapeDtypeStruct((t, 1024), BF16)]
    return pl.pallas_call(
        functools.partial(_mla_pre_kernel, expand_kv=expand_kv),
        grid=(t // tm,),
        in_specs=in_specs, out_specs=out_specs, out_shape=out_shape,
        compiler_params=_cparams(("arbitrary",)),
        name="mla_pre",
    )(*args)


def _rope_tables(pos):
    half = MLA_ROPE // 2
    inv = ROPE_THETA ** (-jnp.arange(half, dtype=F32) / half)
    ang = pos[:, None] * inv[None, :]
    z = jnp.zeros((pos.shape[0], 128 - MLA_ROPE), F32)
    cos, sin = jnp.cos(ang), jnp.sin(ang)
    return jnp.concatenate([cos, cos, z], axis=1), jnp.concatenate([sin, sin, z], axis=1)


def _flash_kernel(q_ref, k_ref, v_ref, z_ref, o_ref, m_ref, acc_ref, *, tq, tk):
    i = pl.program_id(2)
    m_ref[...] = jnp.full(m_ref.shape, NEG, F32)
    acc_ref[...] = jnp.zeros(acc_ref.shape, F32)
    q = q_ref[...]
    jd = (i * tq) // tk
    row = _iota((tq, tk), 0) + (i * tq - jd * tk)
    col = _iota((tq, tk), 1)

    def step(j, masked):
        k = k_ref[pl.ds(pl.multiple_of(j * tk, tk), tk), :]
        v = v_ref[pl.ds(pl.multiple_of(j * tk, tk), tk), :]
        s = lax.dot_general(q, k, (((1,), (1,)), ((), ())), preferred_element_type=F32)
        if masked:
            s = jnp.where(col <= row, s, NEG)
        m_prev = m_ref[...]
        m_new = jnp.maximum(m_prev, jnp.max(s, axis=-1, keepdims=True))
        pr = jnp.exp(s - m_new)
        corr = jnp.exp(m_prev - m_new)
        acc_ref[...] = acc_ref[...] * corr + jnp.dot(pr.astype(BF16), v, preferred_element_type=F32)
        m_ref[...] = m_new

    def body(j, carry):
        step(j, False)
        return carry

    lax.fori_loop(0, jd, body, 0)
    step(jd, True)
    o_ref[...] = (acc_ref[:, :MLA_V] / acc_ref[:, MLA_V:] * _silu(z_ref[...])).astype(BF16)


def _flash(q, k, v, proj, *, nbatch, seq, tq, tk):
    t = q.shape[0]
    nq = seq // tq
    kern = functools.partial(_flash_kernel, tq=tq, tk=tk)
    return pl.pallas_call(
        kern,
        grid=(nbatch, MLA_HEADS, nq),
        in_specs=[pl.BlockSpec((tq, 256), lambda b, h, i: (b * nq + i, h)),
                  pl.BlockSpec((seq, 256), lambda b, h, i: (b, h)),
                  pl.BlockSpec((seq, 2 * MLA_V), lambda b, h, i: (b, h)),
                  pl.BlockSpec((tq, MLA_V), lambda b, h, i: (b * nq + i, B_MLAZ * (BLK // MLA_V) + h))],
        out_specs=pl.BlockSpec((tq, MLA_V), lambda b, h, i: (b * nq + i, h)),
        out_shape=jax.ShapeDtypeStruct((t, BRANCH), BF16),
        scratch_shapes=[pltpu.VMEM((tq, 1), F32), pltpu.VMEM((tq, 2 * MLA_V), F32)],
        compiler_params=_cparams(("arbitrary", "arbitrary", "arbitrary")),
        name="flash",
    )(q, k, v, proj)


def _decode_kernel(pt_ref, q_ref, latn_ref, pen_ref, wukt_ref, gk_ref, lat_hbm, pet_hbm, o_ref,
                   latbuf, pebuf, sem, *, li, pps, ts, n_pages):
    b = pl.program_id(0)
    nseq = pl.num_programs(0)
    ng = n_pages // pps
    nrow = MLA_HEADS * ROWS_S
    nk = MLA_HEADS * MLA_NOPE
    per = ts // PAGE_SIZE

    def copies(seq, g, slot, k):
        ph = pt_ref[seq, g * pps + k]
        return (pltpu.make_async_copy(lat_hbm.at[li, ph], latbuf.at[slot, k], sem.at[slot]),
                pltpu.make_async_copy(pet_hbm.at[li, ph], pebuf.at[slot, k], sem.at[slot]))

    def fetch(seq, g, slot, pages=None):
        for k in (range(pps) if pages is None else pages):
            for cp in copies(seq, g, slot, k):
                cp.start(priority=k % 2)

    def wait(g, slot):
        for k in range(pps):
            for cp in copies(b, g, slot, k):
                cp.wait()

    @pl.when(b == 0)
    def _():
        fetch(0, 0, 0)

    q = q_ref[...]
    gk = gk_ref[...]
    qa, qpe = [], []
    for h in range(MLA_HEADS):
        qn = q[:, h * 256:h * 256 + 128] * gk[:, :128]
        qa.append(_bdot(qn, wukt_ref[h * 128:(h + 1) * 128, :]))
        qpe.append(q[:, h * 256 + 128:h * 256 + 128 + MLA_ROPE] * gk[:, 128:128 + MLA_ROPE])
    amats = [jnp.concatenate([wukt_ref[hp * 2 * MLA_NOPE:(hp + 1) * 2 * MLA_NOPE, :],
                              qa[2 * hp].astype(BF16), qa[2 * hp + 1].astype(BF16)], axis=0)
             for hp in range(MLA_HEADS // 2)]
    qpe = jnp.concatenate(qpe, axis=0).astype(BF16)

    amat = jnp.concatenate(amats, axis=0)
    npair = 2 * MLA_NOPE + 2 * ROWS_S

    def project(lat_b):
        n = lat_b.shape[0]
        if n < 512:
            return [lax.dot_general(a, lat_b, (((1,), (1,)), ((), ())), preferred_element_type=F32) for a in amats]
        r = jnp.concatenate([lax.dot_general(amat, lat_b[i * (n // 2):(i + 1) * (n // 2)], (((1,), (1,)), ((), ())),
                                             preferred_element_type=F32) for i in range(2)], axis=1)
        return [r[hp * npair:(hp + 1) * npair] for hp in range(MLA_HEADS // 2)]

    def scores(rs, s_pe, pe_sq):
        s = []
        for h in range(MLA_HEADS):
            r = rs[h // 2]
            o = (h % 2) * MLA_NOPE
            kn = r[o:o + MLA_NOPE]
            rinv = lax.rsqrt((jnp.sum(kn * kn, axis=0, keepdims=True) + pe_sq) / MLA_QK + EPS)
            sq = r[2 * MLA_NOPE + (h % 2) * ROWS_S:2 * MLA_NOPE + (h % 2 + 1) * ROWS_S]
            s.append((sq + s_pe[h * ROWS_S:(h + 1) * ROWS_S]) * rinv)
        return jnp.concatenate(s, axis=0)

    def partial_softmax(s, lat_b):
        m = jnp.max(s, axis=-1, keepdims=True)
        pr = jnp.exp(s - m)
        return m, jnp.sum(pr, axis=-1, keepdims=True), jnp.dot(pr.astype(BF16), lat_b, preferred_element_type=F32)

    def merge(carry, parts):
        m, l, acc = carry
        m_new = m
        for pm, _, _ in parts:
            m_new = jnp.maximum(m_new, pm)
        corr = jnp.exp(m - m_new)
        l, acc = l * corr, acc * corr
        for pm, pl_, pv in parts:
            w = jnp.exp(pm - m_new)
            l, acc = l + pl_ * w, acc + pv * w
        return m_new, l, acc

    lat_n = latn_ref[...].astype(BF16)
    pe_n = pen_ref[...]
    s_pe_n = _nt(qpe, pe_n)
    pe_sq_n = _nt(jnp.ones((8, MLA_ROPE), F32), pe_n * pe_n)[0:1]
    qrow = _iota((nrow, ROWS_S), 0) % ROWS_S
    s_n = jnp.where(_iota((nrow, ROWS_S), 1) <= qrow, scores(project(lat_n), s_pe_n, pe_sq_n), NEG)
    carry = (jnp.full((nrow, 1), NEG, F32), jnp.zeros((nrow, 1), F32), jnp.zeros((nrow, MLA_KV_LORA), F32))
    carry = merge(carry, [partial_softmax(s_n, lat_n)])

    def group(g, slot, carry, nxt):
        wait(g, slot)
        tiles = range(pps // per)
        lat_b = [latbuf[slot, k * per:(k + 1) * per].reshape(ts, MLA_KV_LORA).astype(BF16) for k in tiles]
        rs = []
        for k in tiles:
            fetch(*nxt, pages=range(k * per, (k + 1) * per))
            rs.append(project(lat_b[k]))
        pet = [jnp.concatenate([pebuf[slot, k * per + i] for i in range(per)], axis=1) for k in tiles]
        s_pe = [jnp.dot(qpe, x.astype(BF16), preferred_element_type=F32) for x in pet]
        pe_sq = [jnp.sum(x * x, axis=0, keepdims=True) for x in pet]
        ss = [scores(rs[k], s_pe[k], pe_sq[k]) for k in tiles]
        return merge(carry, [partial_softmax(ss[k], lat_b[k]) for k in tiles])

    def two_groups(gg, carry):
        carry = group(2 * gg, 0, carry, (b, 2 * gg + 1, 1))
        last = 2 * gg + 2 >= ng
        nxt = (jnp.where(last, jnp.minimum(b + 1, nseq - 1), b), jnp.where(last, 0, 2 * gg + 2), 0)
        return group(2 * gg + 1, 1, carry, nxt)

    m, l, acc = lax.fori_loop(0, ng // 2, two_groups, carry)
    o_ref[0] = acc / l

    @pl.when(b == nseq - 1)
    def _():
        wait(0, 0)


def _decode(page_table, q, lat_new, pe_new, cache_lat, cache_pet, p, *, li, pps, ts):
    nseq, n_pages = page_table.shape
    nrow = MLA_HEADS * ROWS_S
    wukv = p['mla_w_ukv'].reshape(MLA_KV_LORA, MLA_HEADS, MLA_NOPE + MLA_V)
    wukt = wukv[..., :MLA_NOPE].transpose(1, 2, 0).reshape(MLA_HEADS * MLA_NOPE, MLA_KV_LORA).astype(BF16)
    kern = functools.partial(_decode_kernel, li=li, pps=pps, ts=ts, n_pages=n_pages)
    grid_spec = pltpu.PrefetchScalarGridSpec(
        num_scalar_prefetch=1,
        grid=(nseq,),
        in_specs=[pl.BlockSpec((ROWS_S, 1024), lambda b, pt: (b, 0)),
                  pl.BlockSpec((ROWS_S, MLA_KV_LORA), lambda b, pt: (b, 0)),
                  pl.BlockSpec((ROWS_S, MLA_ROPE), lambda b, pt: (b, 0)),
                  pl.BlockSpec((MLA_HEADS * MLA_NOPE, MLA_KV_LORA), lambda b, pt: (0, 0)),
                  pl.BlockSpec((1, 256), lambda b, pt: (0, 0)),
                  pl.BlockSpec(memory_space=pl.ANY),
                  pl.BlockSpec(memory_space=pl.ANY)],
        out_specs=pl.BlockSpec((1, nrow, MLA_KV_LORA), lambda b, pt: (b, 0, 0)),
        scratch_shapes=[pltpu.VMEM((2, pps, PAGE_SIZE, MLA_KV_LORA), F32),
                        pltpu.VMEM((2, pps, MLA_ROPE, PAGE_SIZE), F32),
                        pltpu.SemaphoreType.DMA((2,))],
    )
    return pl.pallas_call(
        kern,
        grid_spec=grid_spec,
        out_shape=jax.ShapeDtypeStruct((nseq, nrow, MLA_KV_LORA), F32),
        compiler_params=_cparams(("arbitrary",)),
        name="decode",
    )(page_table, q, lat_new, pe_new, wukt, _pad_gain(p['mla_k_g']), cache_lat, cache_pet)


def _mla_post_kernel(o_ref, z_ref, wuv_ref, y_ref, *, sb):
    for h in range(MLA_HEADS):
        x = o_ref[:, h * ROWS_S:(h + 1) * ROWS_S, :].reshape(sb * ROWS_S, MLA_KV_LORA)
        y = _bdot(x, wuv_ref[h]) * _silu(z_ref[:, h * MLA_V:(h + 1) * MLA_V])
        y_ref[:, h * MLA_V:(h + 1) * MLA_V] = y.astype(BF16)


def _mla_post(o_lat, proj, p, *, sb):
    nseq = o_lat.shape[0]
    nrow = MLA_HEADS * ROWS_S
    wukv = p['mla_w_ukv'].reshape(MLA_KV_LORA, MLA_HEADS, MLA_NOPE + MLA_V)
    wuv = wukv[..., MLA_NOPE:].transpose(1, 0, 2).astype(BF16)
    return pl.pallas_call(
        functools.partial(_mla_post_kernel, sb=sb),
        grid=(nseq // sb,),
        in_specs=[pl.BlockSpec((sb, nrow, MLA_KV_LORA), lambda i: (i, 0, 0)),
                  pl.BlockSpec((sb * ROWS_S, BLK), lambda i: (i, B_MLAZ)),
                  pl.BlockSpec((MLA_HEADS, MLA_KV_LORA, MLA_V), lambda i: (0, 0, 0))],
        out_specs=pl.BlockSpec((sb * ROWS_S, BLK), lambda i: (i, 0)),
        out_shape=jax.ShapeDtypeStruct((nseq * ROWS_S, BLK), BF16),
        compiler_params=_cparams(("arbitrary",)),
        name="mla_post",
    )(o_lat, proj, wuv)


def _outproj_kernel(b0_ref, b1_ref, b2_ref, b3_ref, x_ref, gate_ref, w_ref, y_ref):
    mixed = _bdot(b0_ref[...], w_ref[0:BLK, :])
    for k, br in enumerate((b1_ref, b2_ref, b3_ref), start=1):
        mixed = mixed + _bdot(br[...], w_ref[k * BLK:(k + 1) * BLK, :])
    y_ref[...] = x_ref[...] + gate_ref[0] * mixed.reshape(x_ref.shape)


def _out_proj(branches, x, mod4, w, *, li, nb, rows, mod_row0):
    d = x.shape[-1]
    tm = nb * rows
    x_map, m_row, grid_m = _row_maps(x.shape, nb, rows, mod_row0)
    br = pl.BlockSpec((tm, BLK), lambda i: (i, 0))
    return pl.pallas_call(
        _outproj_kernel,
        grid=(grid_m,),
        in_specs=[br, br, br, br, pl.BlockSpec((nb, rows, d), x_map),
                  pl.BlockSpec((1, nb, 1, d), lambda i: (li, m_row(i), 0, 2)),
                  pl.BlockSpec((d, d), lambda i: (0, 0))],
        out_specs=pl.BlockSpec((nb, rows, d), x_map),
        out_shape=jax.ShapeDtypeStruct(x.shape, F32),
        compiler_params=_cparams(("arbitrary",)),
        name="out_proj",
    )(*[b.reshape(-1, BLK) for b in branches], x, mod4, w)


def _small_arrays(proj3, nch, L):
    nbatch = proj3.shape[0]
    small = proj3[:, :, B_MLAQ * BLK + SMALL_OFF:B_MLAQ * BLK + SMALL_OFF + 16]
    small_t = small.reshape(nbatch, nch, L, 16).transpose(0, 1, 3, 2)
    return small, small_t


def _pad_rows(a):
    return jnp.pad(a, ((0, 0), (8 - (CONV_K - 1), 0), (0, 0)))


def _s5_state_out(hl):
    nt, nb, _ = hl.shape
    h = hl.reshape(nt, nb, S5_GT, 2, S5_STATE).transpose(1, 0, 2, 4, 3)
    return h.reshape(nb, S5_GROUPS, S5_STATE, 2)


def _layer_prompt(x, mod4, p, w_in, w_out, tabs, li, mod_row0):
    nbatch, seq, _ = x.shape
    t = nbatch * seq
    tm = min(512, seq)
    proj = _in_proj(x, mod4, p['norm_g'], w_in, li=li, nb=1, rows=min(1024, seq), mod_row0=mod_row0)
    p3 = proj.reshape(nbatch, seq, NCOL)
    nch5 = seq // S5_CHUNK
    nt = S5_GROUPS // S5_GT
    u_g = proj[:, :BLK].reshape(nbatch * nch5, S5_CHUNK, nt, 128).transpose(2, 0, 1, 3)
    u_g = u_g.reshape(nt, nbatch * nch5, S5_CHUNK * 128).astype(BF16)
    h0 = jnp.zeros((nt, 8, S5_LANES), F32)
    y5, hl5 = _s5_core(u_g, h0, tabs['s5_prompt'], nbatch=nbatch, nch=nch5, scan=True)
    y5 = y5.reshape(nt, nbatch * nch5, S5_CHUNK, 128).transpose(1, 2, 0, 3).reshape(t, BLK)
    s5_y = _s5_post(y5, proj, p, tm=tm)
    s5_h = _s5_state_out(hl5)
    nch = seq // SSD_CHUNK
    small, small_t = _small_arrays(p3, nch, SSD_CHUNK)
    ssd_y, ssd_h = _ssd(p3, small, small_t, jnp.zeros((nbatch, 8, SSD_CONV_DIM), F32),
                        jnp.zeros((1, nbatch, SSD_HEADS, SSD_HEADDIM, SSD_STATE), F32), p,
                        li=0, nb=nbatch, nch=nch, L=SSD_CHUNK, nvalid=SSD_CHUNK)
    ssd_buf = p3[:, seq - (CONV_K - 1):, B_SSDX * BLK:B_SSDX * BLK + SSD_CONV_DIM]
    q, lat, kpe, k, v = _mla_pre(proj, tabs['cos_p'], tabs['sin_p'], p, tm=tm, pos_blocks=seq // tm, expand_kv=True)
    mla_y = _flash(q, k, v, proj, nbatch=nbatch, seq=seq, tq=tm, tk=min(1024, seq))
    nchd = seq // DN_CHUNK
    small, small_t = _small_arrays(p3, nchd, DN_CHUNK)
    dn_y, dn_s = _dn(p3, small, small_t, jnp.zeros((nbatch, 8, DN_CONV_DIM), F32),
                     jnp.zeros((1, nbatch, DN_HEADS, DN_DK, DN_DV), F32), p,
                     li=0, nb=nbatch, nch=nchd, L=DN_CHUNK, nvalid=DN_CHUNK)
    dn_buf = p3[:, seq - (CONV_K - 1):, B_DNQ * BLK:B_DNQ * BLK + DN_CONV_DIM]
    y = _out_proj((s5_y, ssd_y, mla_y, dn_y), x, mod4, w_out, li=li, nb=1, rows=tm, mod_row0=mod_row0)
    states = (lat.reshape(nbatch, seq, -1), kpe.reshape(nbatch, seq, -1), s5_h, ssd_h, ssd_buf, dn_s, dn_buf)
    return y, states


def _layer_sample(x, mod4, p, w_in, w_out, tabs, li, st, caches, page_table, td):
    nseq = x.shape[0]
    t = nseq * ROWS_S
    sb = min(32, nseq)
    tm = sb * ROWS_S
    nb = min(8, nseq)
    s5_h0, ssd_h0, ssd_buf, dn_s0, dn_buf = st
    proj = _in_proj(x, mod4, p['norm_g'], w_in, li=li, nb=sb, rows=ROWS_S, mod_row0=0)
    p3 = proj.reshape(nseq, ROWS_S, NCOL)
    nt = S5_GROUPS // S5_GT
    u_g = p3[:, :td, :BLK].reshape(nseq, td, nt, 128).transpose(2, 0, 1, 3)
    u_g = u_g.reshape(nt, nseq, td * 128).astype(BF16)
    h0 = jnp.concatenate([s5_h0[li, ..., 0], s5_h0[li, ..., 1]], axis=-1)
    h0 = h0.reshape(nseq, nt, S5_LANES).transpose(1, 0, 2)
    y5, hl5 = _s5_core(u_g, h0, tabs['s5_sample'], nbatch=nseq, nch=1, scan=False)
    y5 = y5.reshape(nt, nseq, td, 128).transpose(1, 2, 0, 3).reshape(nseq, td, BLK)
    y5 = jnp.pad(y5, ((0, 0), (0, ROWS_S - td), (0, 0))).reshape(t, BLK)
    s5_y = _s5_post(y5, proj, p, tm=tm)
    s5_h = _s5_state_out(hl5)
    small, small_t = _small_arrays(p3, 1, ROWS_S)
    ssd_y, ssd_h = _ssd(p3, small, small_t, _pad_rows(ssd_buf[li]), ssd_h0, p,
                        li=li, nb=nb, nch=1, L=ROWS_S, nvalid=td)
    ssd_buf_new = p3[:, td - (CONV_K - 1):td, B_SSDX * BLK:B_SSDX * BLK + SSD_CONV_DIM]
    dn_y, dn_s = _dn(p3, small, small_t, _pad_rows(dn_buf[li]), dn_s0, p,
                     li=li, nb=nb, nch=1, L=ROWS_S, nvalid=td)
    dn_buf_new = p3[:, td - (CONV_K - 1):td, B_DNQ * BLK:B_DNQ * BLK + DN_CONV_DIM]
    q, lat, kpe = _mla_pre(proj, tabs['cos_s'], tabs['sin_s'], p, tm=tm, pos_blocks=1, expand_kv=False)
    n_pages = page_table.shape[1]
    pps = min(32, n_pages // 2)
    o_lat = _decode(page_table, q, lat, kpe, caches[0], caches[1], p, li=li, pps=pps, ts=min(1024, pps * PAGE_SIZE))
    mla_y = _mla_post(o_lat, proj, p, sb=sb)
    y = _out_proj((s5_y, ssd_y, mla_y, dn_y), x, mod4, w_out, li=li, nb=sb, rows=ROWS_S, mod_row0=0)
    lat3 = lat.reshape(nseq, ROWS_S, -1)[:, :td]
    kpe3 = kpe.reshape(nseq, ROWS_S, -1)[:, :td]
    return y, (lat3, kpe3, s5_h, ssd_h, ssd_buf_new, dn_s, dn_buf_new)


def kernel(x_prompt, x_sample, c_prompt, c_sample, cache_kv_latent, cache_k_rope, state_s5, state_ssd, state_ssd_conv, state_dn, state_dn_conv, page_table, norm_g, ada_w, ada_b, w_in, w_out, s5_lam_re, s5_lam_im, s5_log_dt, s5_b_re, s5_b_im, s5_c_re, s5_c_im, s5_d, s5_glu_w, s5_glu_b, ssd_conv_w, ssd_conv_b, ssd_dt_bias, ssd_a_log, ssd_d, ssd_norm_g, mla_q_norm_g, mla_kv_norm_g, mla_w_uq, mla_w_ukv, mla_q_g, mla_k_g, dn_conv_w, dn_a_log, dn_dt_bias, dn_norm_g):
    weights = dict(
        norm_g=norm_g, s5_lam_re=s5_lam_re, s5_lam_im=s5_lam_im, s5_log_dt=s5_log_dt,
        s5_b_re=s5_b_re, s5_b_im=s5_b_im, s5_c_re=s5_c_re, s5_c_im=s5_c_im,
        s5_d=s5_d, s5_glu_w=s5_glu_w, s5_glu_b=s5_glu_b,
        ssd_conv_w=ssd_conv_w, ssd_conv_b=ssd_conv_b, ssd_dt_bias=ssd_dt_bias,
        ssd_a_log=ssd_a_log, ssd_d=ssd_d, ssd_norm_g=ssd_norm_g,
        mla_q_norm_g=mla_q_norm_g, mla_kv_norm_g=mla_kv_norm_g, mla_w_uq=mla_w_uq,
        mla_w_ukv=mla_w_ukv, mla_q_g=mla_q_g, mla_k_g=mla_k_g,
        dn_conv_w=dn_conv_w, dn_a_log=dn_a_log, dn_dt_bias=dn_dt_bias, dn_norm_g=dn_norm_g)
    depth = w_in.shape[0]
    bp, tp, d = x_prompt.shape
    nseq, td, _ = x_sample.shape
    past_len = page_table.shape[1] * PAGE_SIZE

    c_all = jnp.concatenate([c_sample, c_prompt, jnp.zeros((8 - bp, d), F32)], axis=0)
    mod4 = _ada(c_all, ada_w, ada_b).reshape(depth, nseq + 8, 1, 3 * d)

    cos_p, sin_p = _rope_tables(jnp.arange(tp, dtype=F32))
    cos_s, sin_s = _rope_tables(jnp.arange(ROWS_S, dtype=F32) + past_len)
    reps = min(32, nseq)
    tabs_pos = dict(cos_p=cos_p, sin_p=sin_p, cos_s=jnp.tile(cos_s, (reps, 1)), sin_s=jnp.tile(sin_s, (reps, 1)))
    caches = (cache_kv_latent, jnp.swapaxes(cache_k_rope, 2, 3))
    st = (state_s5, state_ssd, state_ssd_conv, state_dn, state_dn_conv)

    y_p = x_prompt
    y_s = jnp.pad(x_sample, ((0, 0), (0, ROWS_S - td), (0, 0)))
    p_states, s_states = [], []
    nsteps = max(1, int(math.log2(tp // S5_CHUNK)))
    for li in range(depth):
        p = {name: w[li] for name, w in weights.items()}
        w_in_l = _prep_w_in(w_in[li])
        w_out_l = w_out[li].astype(BF16)
        tabs = dict(tabs_pos, s5_prompt=_s5_tables(p, S5_CHUNK, nsteps), s5_sample=_s5_tables(p, td, 1))
        y_p, st_p = _layer_prompt(y_p, mod4, p, w_in_l, w_out_l, tabs, li, nseq)
        p_states.append(st_p)
        y_s, st_s = _layer_sample(y_s, mod4, p, w_in_l, w_out_l, tabs, li, st, caches, page_table, td)
        s_states.append(st_s)

    stack = lambda states, i: jnp.stack([s[i] for s in states], axis=0)
    return ((y_p, y_s[:, :td])
            + tuple(stack(p_states, i) for i in range(7))
            + tuple(stack(s_states, i) for i in range(7)))
```
